```python
import math
import jax, jax.numpy as jnp
from jax import lax
import numpy as np

D_MODEL = 1024
BATCH = 32
SEQ = 2048
DEPTH = 1

GLA_HEADS = 4
GLA_DK = 64
GLA_DV = 128
GLA_RANK = 16
GLA_TAU = 16.0
GLA_CHUNK = 64
DIFF_HEADS = 4
DIFF_DQK = 64
DIFF_DV = 128
Q_BLOCK = 128
MEM_LEN = 256
CROSS_HEADS = 4
CROSS_DH = D_MODEL // CROSS_HEADS
N_GROUPS = 4
EXPERTS_PER_GROUP = 4
N_EXPERTS = N_GROUPS * EXPERTS_PER_GROUP
TOP_K = 2
D_EXPERT = 512
MOE_BLOCK = 128
EPS = 1e-6

GLA_QK_W = GLA_HEADS * GLA_DK
GLA_V_W = GLA_HEADS * GLA_DV
DIFF_QK_W = DIFF_HEADS * 2 * DIFF_DQK
DIFF_V_W = DIFF_HEADS * DIFF_DV
D_MIX = GLA_V_W + DIFF_V_W
IN_SPLITS = (GLA_QK_W, GLA_QK_W, GLA_V_W, GLA_RANK, GLA_V_W, DIFF_QK_W, DIFF_QK_W, DIFF_V_W)
D_IN = sum(IN_SPLITS)

kernel_name = "hybrid_gla_diffattn_hmoe_block"


def rmsnorm(x, g):
    xf = x.astype(jnp.float32)
    y = xf * lax.rsqrt(jnp.mean(xf * xf, axis=-1, keepdims=True) + EPS)
    return (y * g.astype(jnp.float32)).astype(x.dtype)


def alibi_slopes(n_heads):
    return jnp.asarray(2.0 ** (-8.0 * np.arange(1, n_heads + 1) / n_heads), dtype=jnp.float32)


def gla_chunked(q, k, v, log_a):
    b_, s_, h_, dk = q.shape
    dv = v.shape[-1]
    n = s_ // GLA_CHUNK

    def chunks(t):
        return t.reshape(b_, n, GLA_CHUNK, h_, t.shape[-1]).transpose(0, 3, 1, 2, 4)

    q, k, v, la = chunks(q), chunks(k), chunks(v), chunks(log_a)
    cum = jnp.cumsum(la, axis=3)
    cum_last = cum[:, :, :, -1:, :]
    q_e = q * jnp.exp(cum)
    k_e = k * jnp.exp(-cum)
    k_s = k * jnp.exp(cum_last - cum)
    causal = jnp.tril(jnp.ones((GLA_CHUNK, GLA_CHUNK), dtype=bool))
    att = jnp.where(causal, jnp.einsum('bhncd,bhnsd->bhncs', q_e, k_e), 0.0)
    o_intra = jnp.einsum('bhncs,bhnse->bhnce', att, v)
    kv = jnp.einsum('bhncd,bhnce->bhnde', k_s, v)
    decay = jnp.exp(cum_last[:, :, :, 0, :])

    def step(state, inp):
        dec, kv_n = inp
        return dec[..., None] * state + kv_n, state

    _, s_prev = lax.scan(step, jnp.zeros((b_, h_, dk, dv), q.dtype),
                         (jnp.moveaxis(decay, 2, 0), jnp.moveaxis(kv, 2, 0)))
    s_prev = jnp.moveaxis(s_prev, 0, 2)
    o = o_intra + jnp.einsum('bhncd,bhnde->bhnce', q_e, s_prev)
    return o.transpose(0, 2, 3, 1, 4).reshape(b_, s_, h_, dv)


def diff_attention(q, k, v, lam):
    b_, h_, _, s_, dqk = q.shape
    nb = s_ // Q_BLOCK
    slopes = alibi_slopes(h_)
    pos = jnp.arange(s_)
    scale = dqk ** -0.5
    q_blocks = jnp.moveaxis(q.reshape(b_, h_, 2, nb, Q_BLOCK, dqk), 3, 0)
    starts = jnp.arange(nb) * Q_BLOCK

    def block(args):
        qb, t0 = args
        t = t0 + jnp.arange(Q_BLOCK)
        dist = (t[:, None] - pos[None, :]).astype(jnp.float32)
        s = jnp.einsum('bhmqd,bhmkd->bhmqk', qb, k).astype(jnp.float32) * scale
        s = s - slopes[None, :, None, None, None] * dist
        s = jnp.where(dist >= 0, s, -jnp.inf)
        p = jax.nn.softmax(s, axis=-1)
        attn = p[:, :, 0] - lam * p[:, :, 1]
        return jnp.einsum('bhqk,bhkd->bhqd', attn.astype(v.dtype), v)

    o = lax.map(block, (q_blocks, starts))
    return o.transpose(1, 0, 3, 2, 4).reshape(b_, s_, h_, v.shape[-1])


def hybrid_mixer(h, w_in, w_gla_a2, b_gla_a, gla_norm_g, diff_norm_g,
                 lambda_q1, lambda_k1, lambda_q2, lambda_k2, w_out, lambda_init):
    b_, s_, _ = h.shape
    split_idx = [int(i) for i in np.cumsum(IN_SPLITS)[:-1]]
    q_g, k_g, v_g, a_lr, r_g, q_d, k_d, v_d = jnp.split(h @ w_in, split_idx, axis=-1)

    log_a = jax.nn.log_sigmoid((a_lr @ w_gla_a2 + b_gla_a).astype(jnp.float32)) / GLA_TAU
    o_g = gla_chunked(
        q_g.reshape(b_, s_, GLA_HEADS, GLA_DK).astype(jnp.float32) * (GLA_DK ** -0.5),
        k_g.reshape(b_, s_, GLA_HEADS, GLA_DK).astype(jnp.float32),
        v_g.reshape(b_, s_, GLA_HEADS, GLA_DV).astype(jnp.float32),
        log_a.reshape(b_, s_, GLA_HEADS, GLA_DK))
    o_g = rmsnorm(o_g, gla_norm_g).astype(h.dtype) * jax.nn.silu(r_g).reshape(b_, s_, GLA_HEADS, GLA_DV)
    o_g = o_g.reshape(b_, s_, GLA_V_W)

    lam = (jnp.exp(jnp.sum(lambda_q1.astype(jnp.float32) * lambda_k1.astype(jnp.float32)))
           - jnp.exp(jnp.sum(lambda_q2.astype(jnp.float32) * lambda_k2.astype(jnp.float32)))
           + lambda_init)
    qd = q_d.reshape(b_, s_, DIFF_HEADS, 2, DIFF_DQK).transpose(0, 2, 3, 1, 4)
    kd = k_d.reshape(b_, s_, DIFF_HEADS, 2, DIFF_DQK).transpose(0, 2, 3, 1, 4)
    vd = v_d.reshape(b_, s_, DIFF_HEADS, DIFF_DV).transpose(0, 2, 1, 3)
    o_d = diff_attention(qd, kd, vd, lam)
    o_d = (rmsnorm(o_d, diff_norm_g) * (1.0 - lambda_init)).reshape(b_, s_, DIFF_V_W)

    return jnp.concatenate([o_g, o_d], axis=-1) @ w_out


def memory_cross_attention(h, mem_n, w_cq, w_ckv, w_co):
    b_, s_, _ = h.shape
    q = (h @ w_cq).reshape(b_, s_, CROSS_HEADS, CROSS_DH)
    k, v = jnp.split(mem_n @ w_ckv, 2, axis=-1)
    k = k.reshape(b_, -1, CROSS_HEADS, CROSS_DH)
    v = v.reshape(b_, -1, CROSS_HEADS, CROSS_DH)
    s = jnp.einsum('bshd,bmhd->bhsm', q, k).astype(jnp.float32) * (CROSS_DH ** -0.5)
    p = jax.nn.softmax(s, axis=-1).astype(v.dtype)
    o = jnp.einsum('bhsm,bmhd->bshd', p, v).reshape(b_, s_, D_MODEL)
    return o @ w_co


def hierarchical_moe(h, w_rg, b_rg, w_re, b_re, w_g, w_u, w_d):
    t_ = h.shape[0]
    p_grp = jax.nn.softmax((h @ w_rg + b_rg).astype(jnp.float32), axis=-1)
    p_g, g_sel = lax.top_k(p_grp, 1)
    p_g, g_sel = p_g[:, 0], g_sel[:, 0]
    logits_e = (h @ w_re + b_re).astype(jnp.float32).reshape(t_, N_GROUPS, EXPERTS_PER_GROUP)
    p_in = jax.nn.softmax(logits_e[jnp.arange(t_), g_sel], axis=-1)
    w_top, i_top = lax.top_k(p_in, TOP_K)
    gate = p_g[:, None] * w_top / jnp.sum(w_top, axis=-1, keepdims=True)
    e_id = g_sel[:, None] * EXPERTS_PER_GROUP + i_top

    a_ = t_ * TOP_K
    e_flat = e_id.reshape(a_)
    tok_flat = jnp.arange(a_) // TOP_K
    gate_flat = gate.reshape(a_)
    order = jnp.argsort(e_flat)
    e_s, tok_s, gate_s = e_flat[order], tok_flat[order], gate_flat[order]
    counts = jax.ops.segment_sum(jnp.ones((a_,), jnp.int32), e_flat, num_segments=N_EXPERTS)
    padded = ((counts + MOE_BLOCK - 1) // MOE_BLOCK) * MOE_BLOCK
    pad_end = jnp.cumsum(padded)
    pad_start = pad_end - padded
    start = jnp.cumsum(counts) - counts
    dest = pad_start[e_s] + (jnp.arange(a_) - start[e_s])
    n_blocks = (a_ + MOE_BLOCK - 1) // MOE_BLOCK + N_EXPERTS
    n_rows = n_blocks * MOE_BLOCK
    xs = jnp.zeros((n_rows, h.shape[1]), h.dtype).at[dest].set(h[tok_s])
    blk_e = jnp.minimum(jnp.searchsorted(pad_end, jnp.arange(n_blocks) * MOE_BLOCK, side='right'),
                        N_EXPERTS - 1)

    def expert_block(args):
        xb, e = args
        hid = jax.nn.silu(xb @ w_g[e]) * (xb @ w_u[e])
        return hid @ w_d[e]

    ys = lax.map(expert_block, (xs.reshape(n_blocks, MOE_BLOCK, -1), blk_e)).reshape(n_rows, -1)
    contrib = ys[dest] * gate_s[:, None].astype(ys.dtype)
    return jax.ops.segment_sum(contrib, tok_s, num_segments=t_)


def setup_inputs(seed: int = 0) -> dict:
    key = jax.random.key(seed)
    ks = jax.random.split(key, 27)
    L = DEPTH

    def nrm(k, shape, scale):
        return jax.random.normal(k, shape, jnp.float32) * scale

    def gain(k, shape):
        return 1.0 + nrm(k, shape, 0.02)

    return {
        "x": nrm(ks[0], (BATCH, SEQ, D_MODEL), 1.0),
        "mem": nrm(ks[1], (BATCH, MEM_LEN, D_MODEL), 1.0),
        "norm_mix_g": gain(ks[2], (L, D_MODEL)),
        "w_in": nrm(ks[3], (L, D_MODEL, D_IN), D_MODEL ** -0.5),
        "w_gla_a2": nrm(ks[4], (L, GLA_RANK, GLA_QK_W), GLA_RANK ** -0.5),
        "b_gla_a": nrm(ks[5], (L, GLA_QK_W), 0.1),
        "gla_norm_g": gain(ks[6], (L, GLA_DV)),
        "diff_norm_g": gain(ks[7], (L, DIFF_DV)),
        "lambda_q1": nrm(ks[8], (L, DIFF_DQK), 0.1),
        "lambda_k1": nrm(ks[9], (L, DIFF_DQK), 0.1),
        "lambda_q2": nrm(ks[10], (L, DIFF_DQK), 0.1),
        "lambda_k2": nrm(ks[11], (L, DIFF_DQK), 0.1),
        "w_out": nrm(ks[12], (L, D_MIX, D_MODEL), D_MIX ** -0.5),
        "norm_cross_g": gain(ks[13], (L, D_MODEL)),
        "norm_mem_g": gain(ks[14], (L, D_MODEL)),
        "w_cq": nrm(ks[15], (L, D_MODEL, D_MODEL), D_MODEL ** -0.5),
        "w_ckv": nrm(ks[16], (L, D_MODEL, 2 * D_MODEL), D_MODEL ** -0.5),
        "w_co": nrm(ks[17], (L, D_MODEL, D_MODEL), D_MODEL ** -0.5),
        "norm_ffn_g": gain(ks[18], (L, D_MODEL)),
        "w_router_grp": nrm(ks[19], (L, D_MODEL, N_GROUPS), D_MODEL ** -0.5),
        "b_router_grp": nrm(ks[20], (L, N_GROUPS), 0.01),
        "w_router_exp": nrm(ks[21], (L, D_MODEL, N_EXPERTS), D_MODEL ** -0.5),
        "b_router_exp": nrm(ks[22], (L, N_EXPERTS), 0.01),
        "w_e_gate": nrm(ks[23], (L, N_EXPERTS, D_MODEL, D_EXPERT), D_MODEL ** -0.5),
        "w_e_up": nrm(ks[24], (L, N_EXPERTS, D_MODEL, D_EXPERT), D_MODEL ** -0.5),
        "w_e_down": nrm(ks[25], (L, N_EXPERTS, D_EXPERT, D_MODEL), D_EXPERT ** -0.5),
        "norm_final_g": gain(ks[26], (D_MODEL,)),
    }


def reference(x, mem, norm_mix_g, w_in, w_gla_a2, b_gla_a, gla_norm_g, diff_norm_g,
              lambda_q1, lambda_k1, lambda_q2, lambda_k2, w_out, norm_cross_g, norm_mem_g,
              w_cq, w_ckv, w_co, norm_ffn_g, w_router_grp, b_router_grp, w_router_exp,
              b_router_exp, w_e_gate, w_e_up, w_e_down, norm_final_g):
    b_, s_, d_ = x.shape
    for layer in range(DEPTH):
        lambda_init = 0.8 - 0.6 * math.exp(-0.3 * layer)
        x = x + hybrid_mixer(rmsnorm(x, norm_mix_g[layer]), w_in[layer], w_gla_a2[layer],
                             b_gla_a[layer], gla_norm_g[layer], diff_norm_g[layer],
                             lambda_q1[layer], lambda_k1[layer], lambda_q2[layer],
                             lambda_k2[layer], w_out[layer], lambda_init)
        x = x + memory_cross_attention(rmsnorm(x, norm_cross_g[layer]),
                                       rmsnorm(mem, norm_mem_g[layer]),
                                       w_cq[layer], w_ckv[layer], w_co[layer])
        y = hierarchical_moe(rmsnorm(x, norm_ffn_g[layer]).reshape(b_ * s_, d_),
                             w_router_grp[layer], b_router_grp[layer], w_router_exp[layer],
                             b_router_exp[layer], w_e_gate[layer], w_e_up[layer], w_e_down[layer])
        x = x + y.reshape(b_, s_, d_)
    return rmsnorm(x, norm_final_g)
```

```python
import functools
import math

import jax
import jax.numpy as jnp
import numpy as np
from jax import lax
from jax.experimental import pallas as pl
from jax.experimental.pallas import tpu as pltpu

F32 = jnp.float32
BF16 = jnp.bfloat16
U32 = jnp.uint32

EPS = 1e-6
D_MODEL = 1024
GLA_HEADS = 4
GLA_DK = 64
GLA_DV = 128
GLA_RANK = 16
GLA_TAU = 16.0
GLA_CHUNK = 64
GLA_QK_W = GLA_HEADS * GLA_DK
GLA_V_W = GLA_HEADS * GLA_DV
DIFF_HEADS = 4
DIFF_DQK = 64
DIFF_DV = 128
DIFF_QK_W = DIFF_HEADS * 2 * DIFF_DQK
DIFF_V_W = DIFF_HEADS * DIFF_DV
CROSS_HEADS = 4
CROSS_DH = D_MODEL // CROSS_HEADS
N_GROUPS = 4
EXPERTS_PER_GROUP = 4
N_EXPERTS = N_GROUPS * EXPERTS_PER_GROUP
TOP_K = 2
D_EXPERT = 512
LAMBDA_INIT = 0.8 - 0.6 * math.exp(-0.3 * 0)

LANES = 128
A_PAD = LANES
GLA_IN_W = 2 * GLA_QK_W + 2 * GLA_V_W + A_PAD
DIFF_IN_W = 2 * DIFF_QK_W + DIFF_V_W
HALF = D_MODEL // 2

ROW_TILE = 512
GLA_ROWS = 512
ATT_TILE = 256
MOE_ROWS = 512
GATHER_ROWS = 512
VMEM_LIMIT = 56 * 1024 * 1024


def _params(sem, vmem=VMEM_LIMIT):
    return pltpu.CompilerParams(dimension_semantics=sem, vmem_limit_bytes=vmem)


def _rms(x, g):
    ms = jnp.mean(x * x, axis=-1, keepdims=True)
    return x * lax.rsqrt(ms + EPS) * g


def _dot(a, b):
    return jnp.dot(a, b, preferred_element_type=F32)


def _dot_nt(a, b):
    return lax.dot_general(a, b, (((1,), (1,)), ((), ())), preferred_element_type=F32)


def _dot_tn(a, b):
    return lax.dot_general(a, b, (((0,), (0,)), ((), ())), preferred_element_type=F32)


def _pack_halves(lo, hi):
    lo_b = lax.bitcast_convert_type(lo.astype(BF16).astype(F32), U32)
    hi_b = lax.bitcast_convert_type(hi.astype(BF16).astype(F32), U32)
    return (lo_b >> 16) | (hi_b & jnp.uint32(0xFFFF0000))


def _unpack_halves(w):
    lo = lax.bitcast_convert_type(w << 16, F32)
    hi = lax.bitcast_convert_type(w & jnp.uint32(0xFFFF0000), F32)
    return lo, hi


def _norm_matmul_body(x_ref, g_ref, w_ref, *o_refs):
    h = _rms(x_ref[...], g_ref[...]).astype(BF16)
    off = 0
    for o_ref in o_refs:
        n = o_ref.shape[-1]
        o_ref[...] = _dot(h, w_ref[:, off:off + n]).astype(o_ref.dtype)
        off += n


def _norm_matmul(x, g, w, splits, name):
    t, d = x.shape
    n = w.shape[1]
    assert sum(splits) == n and t % ROW_TILE == 0
    return pl.pallas_call(
        _norm_matmul_body,
        grid=(t // ROW_TILE,),
        in_specs=[
            pl.BlockSpec((ROW_TILE, d), lambda i: (i, 0)),
            pl.BlockSpec((1, d), lambda i: (0, 0)),
            pl.BlockSpec((d, n), lambda i: (0, 0)),
        ],
        out_specs=[pl.BlockSpec((ROW_TILE, s), lambda i: (i, 0)) for s in splits],
        out_shape=[jax.ShapeDtypeStruct((t, s), BF16) for s in splits],
        compiler_params=_params(("parallel",)),
        name=name,
    )(x, g.reshape(1, d), w)


def _split3(x):
    hi = x.astype(BF16)
    r1 = x - hi.astype(F32)
    mid = r1.astype(BF16)
    lo = (r1 - mid.astype(F32)).astype(BF16)
    return hi, mid, lo


def _gla_body(in_ref, wa_ref, ba_ref, gn_ref, o_ref, s_ref):
    c_ = GLA_CHUNK

    @pl.when(pl.program_id(1) == 0)
    def _():
        s_ref[...] = jnp.zeros_like(s_ref)

    row = lax.broadcasted_iota(jnp.int32, (c_, c_), 0)
    col = lax.broadcasted_iota(jnp.int32, (c_, c_), 1)
    causal = row >= col
    tri = causal.astype(BF16)
    ones = jnp.ones((c_, GLA_DV), BF16)
    gn = gn_ref[...]
    q0, k0, v0, r0, a0 = 0, GLA_QK_W, 2 * GLA_QK_W, 2 * GLA_QK_W + GLA_V_W, 2 * GLA_QK_W + 2 * GLA_V_W

    for c in range(GLA_ROWS // c_):
        rows = pl.ds(c * c_, c_)
        q = in_ref[rows, q0:q0 + GLA_QK_W].astype(F32)
        k = in_ref[rows, k0:k0 + GLA_QK_W].astype(F32)
        v = in_ref[rows, v0:v0 + GLA_V_W]
        r = in_ref[rows, r0:r0 + GLA_V_W].astype(F32)
        a = in_ref[rows, a0:a0 + A_PAD]
        z = _dot(a, wa_ref[...]) + ba_ref[...]
        log_a = jax.nn.log_sigmoid(z) / GLA_TAU
        parts = _split3(log_a)
        cum = _dot(tri, parts[0]) + _dot(tri, parts[1]) + _dot(tri, parts[2])
        cum_last = cum[c_ - 1:c_, :]
        q_e = q * (GLA_DK ** -0.5) * jnp.exp(cum)
        k_e = k * jnp.exp(-cum)
        k_s = k * jnp.exp(cum_last - cum)
        for h in range(GLA_HEADS):
            ks_ = slice(h * GLA_DK, (h + 1) * GLA_DK)
            vs_ = slice(h * GLA_DV, (h + 1) * GLA_DV)
            qe_h = q_e[:, ks_].astype(BF16)
            ke_h = k_e[:, ks_].astype(BF16)
            ksc_h = k_s[:, ks_].astype(BF16)
            v_h = v[:, vs_]
            att = jnp.where(causal, _dot_nt(qe_h, ke_h), 0.0).astype(BF16)
            s_old = s_ref[h]
            o_h = _dot(att, v_h) + _dot(qe_h, s_old.astype(BF16))
            dsum = (_dot_tn(parts[0][:, ks_], ones) + _dot_tn(parts[1][:, ks_], ones)
                    + _dot_tn(parts[2][:, ks_], ones))
            s_ref[h] = jnp.exp(dsum) * s_old + _dot_tn(ksc_h, v_h)
            o_ref[rows, vs_] = (_rms(o_h, gn) * jax.nn.silu(r[:, vs_])).astype(o_ref.dtype)


def _gla(gla_in, w_a2p, b_a, gn, batch, seq):
    t = gla_in.shape[0]
    assert seq % GLA_ROWS == 0 and GLA_ROWS % GLA_CHUNK == 0
    nsb = seq // GLA_ROWS
    return pl.pallas_call(
        _gla_body,
        grid=(batch, nsb),
        in_specs=[
            pl.BlockSpec((GLA_ROWS, GLA_IN_W), lambda b, j: (b * nsb + j, 0)),
            pl.BlockSpec((A_PAD, GLA_QK_W), lambda b, j: (0, 0)),
            pl.BlockSpec((1, GLA_QK_W), lambda b, j: (0, 0)),
            pl.BlockSpec((1, GLA_DV), lambda b, j: (0, 0)),
        ],
        out_specs=pl.BlockSpec((GLA_ROWS, GLA_V_W), lambda b, j: (b * nsb + j, 0)),
        out_shape=jax.ShapeDtypeStruct((t, GLA_V_W), BF16),
        scratch_shapes=[pltpu.VMEM((GLA_HEADS, GLA_DK, GLA_DV), F32)],
        compiler_params=_params(("parallel", "arbitrary")),
        name="gla",
    )(gla_in, w_a2p, b_a.reshape(1, -1), gn.reshape(1, -1))


def _diff_body(slope_ref, q_ref, k_ref, v_ref, lam_ref, gn_ref, o_ref, m_ref, l_ref, acc_ref):
    tq = ATT_TILE
    h = pl.program_id(1)
    qi = pl.program_id(2)
    slope = slope_ref[h]
    q = q_ref[...]
    lane = lax.broadcasted_iota(jnp.int32, q.shape, 1)
    q_maps = (jnp.where(lane < DIFF_DQK, q, jnp.zeros_like(q)), jnp.where(lane >= DIFF_DQK, q, jnp.zeros_like(q)))
    row = lax.broadcasted_iota(jnp.int32, (tq, tq), 0)
    col = lax.broadcasted_iota(jnp.int32, (tq, tq), 1)
    rel = (row - col).astype(F32)

    m_ref[...] = jnp.full_like(m_ref, -jnp.inf)
    l_ref[...] = jnp.zeros_like(l_ref)
    acc_ref[...] = jnp.zeros_like(acc_ref)

    def kv_step(j, masked):
        ks = k_ref[pl.ds(pl.multiple_of(j * tq, tq), tq), :]
        vs = v_ref[pl.ds(pl.multiple_of(j * tq, tq), tq), :]
        dist = rel + ((qi - j) * tq).astype(F32)
        for m in range(2):
            s = _dot_nt(q_maps[m], ks) * (DIFF_DQK ** -0.5) - slope * dist
            if masked:
                s = jnp.where(dist >= 0, s, -jnp.inf)
            m_old = m_ref[m]
            m_new = jnp.maximum(m_old, jnp.max(s, axis=-1, keepdims=True))
            alpha = jnp.exp(m_old - m_new)
            p = jnp.exp(s - m_new)
            l_ref[m] = alpha * l_ref[m] + jnp.sum(p, axis=-1, keepdims=True)
            acc_ref[m] = alpha * acc_ref[m] + _dot(p.astype(BF16), vs)
            m_ref[m] = m_new

    def full_step(j, carry):
        kv_step(j, False)
        return carry

    lax.fori_loop(0, qi, full_step, 0)
    kv_step(qi, True)

    lq1, lk1, lq2, lk2 = (lam_ref[i:i + 1, :] for i in range(4))
    lam = (jnp.exp(jnp.sum(lq1 * lk1, axis=-1, keepdims=True))
           - jnp.exp(jnp.sum(lq2 * lk2, axis=-1, keepdims=True)) + LAMBDA_INIT)
    o = acc_ref[0] / l_ref[0] - lam * (acc_ref[1] / l_ref[1])
    o_ref[...] = (_rms(o, gn_ref[...]) * (1.0 - LAMBDA_INIT)).astype(o_ref.dtype)


def _diff_attn(diff_in, lam_params, gn, batch, seq):
    t = diff_in.shape[0]
    tq = ATT_TILE
    assert seq % tq == 0
    nq = seq // tq
    slopes = jnp.asarray(2.0 ** (-8.0 * np.arange(1, DIFF_HEADS + 1) / DIFF_HEADS), dtype=F32)
    grid_spec = pltpu.PrefetchScalarGridSpec(
        num_scalar_prefetch=1,
        grid=(batch, DIFF_HEADS, nq),
        in_specs=[
            pl.BlockSpec((tq, 2 * DIFF_DQK), lambda b, h, i, s: (b * nq + i, h)),
            pl.BlockSpec((seq, 2 * DIFF_DQK), lambda b, h, i, s: (b, DIFF_HEADS + h)),
            pl.BlockSpec((seq, DIFF_DV), lambda b, h, i, s: (b, 2 * DIFF_HEADS + h)),
            pl.BlockSpec((4, DIFF_DQK), lambda b, h, i, s: (0, 0)),
            pl.BlockSpec((1, DIFF_DV), lambda b, h, i, s: (0, 0)),
        ],
        out_specs=pl.BlockSpec((tq, DIFF_DV), lambda b, h, i, s: (b * nq + i, h)),
        scratch_shapes=[
            pltpu.VMEM((2, tq, 1), F32),
            pltpu.VMEM((2, tq, 1), F32),
            pltpu.VMEM((2, tq, DIFF_DV), F32),
        ],
    )
    return pl.pallas_call(
        _diff_body,
        grid_spec=grid_spec,
        out_shape=jax.ShapeDtypeStruct((t, DIFF_V_W), BF16),
        compiler_params=_params(("parallel", "parallel", "arbitrary")),
        name="diff_attn",
    )(slopes, diff_in, diff_in, diff_in, lam_params, gn.reshape(1, -1))


def _route(logits):
    lane = lax.broadcasted_iota(jnp.int32, logits.shape, 1)
    big = jnp.int32(LANES)
    neg = -jnp.inf

    def first_argmax(vals, vmax):
        return jnp.min(jnp.where(vals == vmax, lane, big), axis=-1, keepdims=True)

    is_grp = (lane >= N_EXPERTS) & (lane < N_EXPERTS + N_GROUPS)
    lg = jnp.where(is_grp, logits, neg)
    mg = jnp.max(lg, axis=-1, keepdims=True)
    p_g = 1.0 / jnp.sum(jnp.exp(lg - mg), axis=-1, keepdims=True)
    g_sel = first_argmax(lg, mg) - N_EXPERTS
    in_grp = (lane >= g_sel * EXPERTS_PER_GROUP) & (lane < (g_sel + 1) * EXPERTS_PER_GROUP)
    le = jnp.where(in_grp, logits, neg)
    m1 = jnp.max(le, axis=-1, keepdims=True)
    i1 = first_argmax(le, m1)
    le2 = jnp.where(lane == i1, neg, le)
    m2 = jnp.max(le2, axis=-1, keepdims=True)
    i2 = first_argmax(le2, m2)
    den = jnp.sum(jnp.exp(le - m1), axis=-1, keepdims=True)
    w1 = 1.0 / den
    w2 = jnp.exp(m2 - m1) / den
    wsum = w1 + w2
    g1 = p_g * w1 / wsum
    g2 = p_g * w2 / wsum
    out = jnp.where(lane == 0, g1, jnp.where(lane == 1, g2, jnp.where(lane == 2, i1.astype(F32), i2.astype(F32))))
    return jnp.where(lane < 4, out, 0.0)


def _post_mix_body(x_ref, og_ref, od_ref, wo_ref, gc_ref, wq_ref, km_ref, vm_ref, wco_ref, gf_ref, wr_ref, br_ref,
                   x2_ref, h3_ref, route_ref):
    x1 = x_ref[...] + _dot(og_ref[...], wo_ref[0:GLA_V_W, :]) + _dot(od_ref[...], wo_ref[GLA_V_W:, :])
    h2 = _rms(x1, gc_ref[...]).astype(BF16)
    qc = _dot(h2, wq_ref[...]).astype(BF16)
    heads = []
    for h in range(CROSS_HEADS):
        sl = slice(h * CROSS_DH, (h + 1) * CROSS_DH)
        s = _dot_nt(qc[:, sl], km_ref[:, sl]) * (CROSS_DH ** -0.5)
        s = s - jnp.max(s, axis=-1, keepdims=True)
        p = jnp.exp(s)
        p = p / jnp.sum(p, axis=-1, keepdims=True)
        heads.append(_dot(p.astype(BF16), vm_ref[:, sl]).astype(BF16))
    oc = jnp.concatenate(heads, axis=-1)
    x2 = x1 + _dot(oc, wco_ref[...])
    x2_ref[...] = x2
    h3 = _rms(x2, gf_ref[...])
    h3_ref[...] = _pack_halves(h3[:, :HALF], h3[:, HALF:])
    route_ref[...] = _route(_dot(h3.astype(BF16), wr_ref[...]) + br_ref[...])


def _post_mix(x, og, od, w_out, g_cross, w_cq, kmem, vmem, w_co, g_ffn, w_r, b_r, seq, mem_len):
    t, d = x.shape
    tm = ROW_TILE
    assert seq % tm == 0
    per_b = seq // tm
    full = lambda shape: pl.BlockSpec(shape, lambda i: (0, 0))
    rows = lambda w: pl.BlockSpec((tm, w), lambda i: (i, 0))
    return pl.pallas_call(
        _post_mix_body,
        grid=(t // tm,),
        in_specs=[
            rows(d), rows(GLA_V_W), rows(DIFF_V_W),
            full((d, d)), full((1, d)), full((d, d)),
            pl.BlockSpec((mem_len, d), lambda i: (i // per_b, 0)),
            pl.BlockSpec((mem_len, d), lambda i: (i // per_b, 0)),
            full((d, d)), full((1, d)), full((d, LANES)), full((1, LANES)),
        ],
        out_specs=[rows(d), rows(HALF), rows(LANES)],
        out_shape=[
            jax.ShapeDtypeStruct((t, d), F32),
            jax.ShapeDtypeStruct((t, HALF), U32),
            jax.ShapeDtypeStruct((t, LANES), F32),
        ],
        compiler_params=_params(("parallel",)),
        name="post_mix",
    )(x, og, od, w_out, g_cross.reshape(1, d), w_cq, kmem, vmem, w_co, g_ffn.reshape(1, d), w_r, b_r)


def _gather_body(idx_ref, src_ref, o_ref, sem):
    n = GATHER_ROWS

    def row_copy(r):
        return pltpu.make_async_copy(src_ref.at[pl.ds(idx_ref[0, 0, r], 1)], o_ref.at[pl.ds(r, 1)], sem)

    def issue(r8, carry):
        for u in range(8):
            row_copy(r8 * 8 + u).start()
        return carry

    lax.fori_loop(0, n // 8, issue, 0)
    pltpu.make_async_copy(src_ref.at[pl.ds(0, n)], o_ref, sem).wait()


def _gather_rows(src, idx, name):
    m = idx.shape[0]
    w = src.shape[1]
    assert m % GATHER_ROWS == 0
    nb = m // GATHER_ROWS
    return pl.pallas_call(
        _gather_body,
        grid=(nb,),
        in_specs=[
            pl.BlockSpec((1, 1, GATHER_ROWS), lambda i: (i, 0, 0), memory_space=pltpu.SMEM),
            pl.BlockSpec(memory_space=pl.ANY),
        ],
        out_specs=pl.BlockSpec((GATHER_ROWS, w), lambda i: (i, 0)),
        out_shape=jax.ShapeDtypeStruct((m, w), src.dtype),
        scratch_shapes=[pltpu.SemaphoreType.DMA(())],
        compiler_params=_params(("arbitrary",)),
        name=name,
    )(idx.reshape(nb, 1, GATHER_ROWS), src)


def _expert_body(blk_e_ref, nvalid_ref, xs_ref, wg_ref, wu_ref, wd_ref, ys_ref):
    i = pl.program_id(0)

    @pl.when(i < nvalid_ref[0])
    def _():
        lo, hi = _unpack_halves(xs_ref[...])
        lo = lo.astype(BF16)
        hi = hi.astype(BF16)
        g = _dot(lo, wg_ref[0:HALF, :]) + _dot(hi, wg_ref[HALF:, :])
        u = _dot(lo, wu_ref[0:HALF, :]) + _dot(hi, wu_ref[HALF:, :])
        hid = (jax.nn.silu(g) * u).astype(BF16)
        y = _dot(hid, wd_ref[...])
        ys_ref[...] = _pack_halves(y[:, :HALF], y[:, HALF:])

    @pl.when(i >= nvalid_ref[0])
    def _():
        ys_ref[...] = jnp.zeros_like(ys_ref)


def _experts(xs, blk_e, nvalid, w_g, w_u, w_d):
    n_rows = xs.shape[0]
    nb = n_rows // MOE_ROWS
    d, de = w_g.shape[1], w_g.shape[2]
    grid_spec = pltpu.PrefetchScalarGridSpec(
        num_scalar_prefetch=2,
        grid=(nb,),
        in_specs=[
            pl.BlockSpec((MOE_ROWS, HALF), lambda i, be, nv: (i, 0)),
            pl.BlockSpec((None, d, de), lambda i, be, nv: (be[i], 0, 0)),
            pl.BlockSpec((None, d, de), lambda i, be, nv: (be[i], 0, 0)),
            pl.BlockSpec((None, de, d), lambda i, be, nv: (be[i], 0, 0)),
        ],
        out_specs=pl.BlockSpec((MOE_ROWS, HALF), lambda i, be, nv: (i, 0)),
    )
    return pl.pallas_call(
        _expert_body,
        grid_spec=grid_spec,
        out_shape=jax.ShapeDtypeStruct((n_rows, HALF), U32),
        compiler_params=_params(("arbitrary",)),
        name="experts",
    )(blk_e, nvalid, xs, w_g, w_u, w_d)


def _combine_body(x2_ref, y0_ref, y1_ref, route_ref, g_ref, o_ref):
    route = route_ref[...]
    g0 = route[:, 0:1]
    g1 = route[:, 1:2]
    lo0, hi0 = _unpack_halves(y0_ref[...])
    lo1, hi1 = _unpack_halves(y1_ref[...])
    x2 = x2_ref[...]
    x_lo = x2[:, :HALF] + (lo0 * g0 + lo1 * g1)
    x_hi = x2[:, HALF:] + (hi0 * g0 + hi1 * g1)
    ms = (jnp.sum(x_lo * x_lo, axis=-1, keepdims=True) + jnp.sum(x_hi * x_hi, axis=-1, keepdims=True)) / D_MODEL
    inv = lax.rsqrt(ms + EPS)
    g = g_ref[...]
    o_ref[:, :HALF] = x_lo * inv * g[:, :HALF]
    o_ref[:, HALF:] = x_hi * inv * g[:, HALF:]


def _combine(x2, yk, route, g_final):
    t, d = x2.shape
    tm = ROW_TILE
    nt = t // tm
    return pl.pallas_call(
        _combine_body,
        grid=(nt,),
        in_specs=[
            pl.BlockSpec((tm, d), lambda i: (i, 0)),
            pl.BlockSpec((tm, HALF), lambda i: (i, 0)),
            pl.BlockSpec((tm, HALF), lambda i: (i + nt, 0)),
            pl.BlockSpec((tm, LANES), lambda i: (i, 0)),
            pl.BlockSpec((1, d), lambda i: (0, 0)),
        ],
        out_specs=pl.BlockSpec((tm, d), lambda i: (i, 0)),
        out_shape=jax.ShapeDtypeStruct((t, d), F32),
        compiler_params=_params(("parallel",)),
        name="combine",
    )(x2, yk, yk, route, g_final.reshape(1, d))


def _dispatch_plan(route, t):
    e_id = route[:, 2:4].astype(jnp.int32)
    e_flat = e_id.T.reshape(-1)
    onehot = (e_flat[:, None] == jnp.arange(N_EXPERTS, dtype=jnp.int32)[None, :]).astype(jnp.int32)
    ranks = jnp.cumsum(onehot, axis=0) - onehot
    rank = jnp.sum(ranks * onehot, axis=1)
    counts = jnp.sum(onehot, axis=0)
    padded = ((counts + MOE_ROWS - 1) // MOE_ROWS) * MOE_ROWS
    pad_end = jnp.cumsum(padded)
    pad_start = pad_end - padded
    dest = pad_start[e_flat] + rank
    a_ = TOP_K * t
    n_blocks = a_ // MOE_ROWS + N_EXPERTS
    n_rows = n_blocks * MOE_ROWS
    tok = jnp.tile(jnp.arange(t, dtype=jnp.int32), TOP_K)
    src_tok = jnp.zeros((n_rows,), jnp.int32).at[dest].set(tok)
    blk_start = jnp.arange(n_blocks, dtype=jnp.int32) * MOE_ROWS
    blk_e = jnp.minimum(jnp.searchsorted(pad_end, blk_start, side='right'), N_EXPERTS - 1).astype(jnp.int32)
    nvalid = (pad_end[-1] // MOE_ROWS).astype(jnp.int32).reshape(1)
    return src_tok, dest.astype(jnp.int32), blk_e, nvalid


def _regroup_w_in(w_in):
    splits = np.cumsum([GLA_QK_W, GLA_QK_W, GLA_V_W, GLA_RANK, GLA_V_W, DIFF_QK_W, DIFF_QK_W, DIFF_V_W])[:-1]
    q_g, k_g, v_g, a_lr, r_g, q_d, k_d, v_d = jnp.split(w_in, [int(i) for i in splits], axis=1)
    a_pad = jnp.pad(a_lr, ((0, 0), (0, A_PAD - GLA_RANK)))
    return jnp.concatenate([q_g, k_g, v_g, r_g, a_pad, q_d, k_d, v_d], axis=1).astype(BF16)


def kernel(x, mem, norm_mix_g, w_in, w_gla_a2, b_gla_a, gla_norm_g, diff_norm_g, lambda_q1, lambda_k1, lambda_q2,
           lambda_k2, w_out, norm_cross_g, norm_mem_g, w_cq, w_ckv, w_co, norm_ffn_g, w_router_grp, b_router_grp,
           w_router_exp, b_router_exp, w_e_gate, w_e_up, w_e_down, norm_final_g):
    b_, s_, d_ = x.shape
    m_ = mem.shape[1]
    t = b_ * s_
    xf = x.reshape(t, d_)

    kmem, vmem = _norm_matmul(mem.reshape(b_ * m_, d_), norm_mem_g[0], w_ckv[0].astype(BF16), (d_, d_), "mem_kv")

    gla_in, diff_in = _norm_matmul(xf, norm_mix_g[0], _regroup_w_in(w_in[0]), (GLA_IN_W, DIFF_IN_W), "in_proj")

    w_a2p = jnp.pad(w_gla_a2[0], ((0, A_PAD - GLA_RANK), (0, 0))).astype(BF16)
    o_g = _gla(gla_in, w_a2p, b_gla_a[0], gla_norm_g[0], b_, s_)

    lam_params = jnp.stack([lambda_q1[0], lambda_k1[0], lambda_q2[0], lambda_k2[0]]).astype(F32)
    o_d = _diff_attn(diff_in, lam_params, diff_norm_g[0], b_, s_)

    pad_r = LANES - N_EXPERTS - N_GROUPS
    w_r = jnp.pad(jnp.concatenate([w_router_exp[0], w_router_grp[0]], axis=1), ((0, 0), (0, pad_r))).astype(BF16)
    b_r = jnp.pad(jnp.concatenate([b_router_exp[0], b_router_grp[0]]), (0, pad_r)).reshape(1, LANES)
    x2, h3, route = _post_mix(xf, o_g, o_d, w_out[0].astype(BF16), norm_cross_g[0], w_cq[0].astype(BF16), kmem, vmem,
                              w_co[0].astype(BF16), norm_ffn_g[0], w_r, b_r, s_, m_)

    src_tok, dest, blk_e, nvalid = _dispatch_plan(route, t)
    xs = _gather_rows(h3, src_tok, "dispatch")
    ys = _experts(xs, blk_e, nvalid, w_e_gate[0].astype(BF16), w_e_up[0].astype(BF16), w_e_down[0].astype(BF16))
    yk = _gather_rows(ys, dest, "collect")

    out = _combine(x2, yk, route, norm_final_g)
    return out.reshape(b_, s_, d_)
```

```python
import functools
import math

import jax
import jax.numpy as jnp
import numpy as np
from jax import lax
from jax.experimental import pallas as pl
from jax.experimental.pallas import tpu as pltpu

F32 = jnp.float32
BF16 = jnp.bfloat16
U32 = jnp.uint32

EPS = 1e-6
D_MODEL = 1024
GLA_HEADS = 4
GLA_DK = 64
GLA_DV = 128
GLA_RANK = 16
GLA_TAU = 16.0
GLA_CHUNK = 64
GLA_QK_W = GLA_HEADS * GLA_DK
GLA_V_W = GLA_HEADS * GLA_DV
DIFF_HEADS = 4
DIFF_DQK = 64
DIFF_DV = 128
DIFF_QK_W = DIFF_HEADS * 2 * DIFF_DQK
DIFF_V_W = DIFF_HEADS * DIFF_DV
CROSS_HEADS = 4
CROSS_DH = D_MODEL // CROSS_HEADS
N_GROUPS = 4
EXPERTS_PER_GROUP = 4
N_EXPERTS = N_GROUPS * EXPERTS_PER_GROUP
TOP_K = 2
D_EXPERT = 512
LAMBDA_INIT = 0.8 - 0.6 * math.exp(-0.3 * 0)

LANES = 128
SUBLANES = 8
A_PAD = LANES
GLA_IN_W = 2 * GLA_QK_W + 2 * GLA_V_W + A_PAD
ROW_W = D_MODEL + LANES
PAIR_LO = (0, 0, 0, 1, 1, 2)
PAIR_HI = (1, 2, 3, 2, 3, 3)
N_PAIRS = len(PAIR_LO)
N_CLASSES = N_GROUPS * N_PAIRS

ROW_TILE = 512
GLA_ROWS = 512
ATT_TILE = 512
ATT_ONES = 16
MOE_ROWS = 512
GATHER_ROWS = 512
VMEM_LIMIT = 56 * 1024 * 1024


def _params(sem, vmem=VMEM_LIMIT):
    return pltpu.CompilerParams(dimension_semantics=sem, vmem_limit_bytes=vmem)


def _rms(x, g):
    ms = jnp.mean(x * x, axis=-1, keepdims=True)
    return x * lax.rsqrt(ms + EPS) * g


def _dot(a, b):
    return jnp.dot(a, b, preferred_element_type=F32)


def _dot_nt(a, b):
    return lax.dot_general(a, b, (((1,), (1,)), ((), ())), preferred_element_type=F32)


def _dot_tn(a, b):
    return lax.dot_general(a, b, (((0,), (0,)), ((), ())), preferred_element_type=F32)


def _norm_matmul_body(x_ref, g_ref, w_ref, *o_refs):
    h = _rms(x_ref[...], g_ref[...]).astype(BF16)
    off = 0
    for o_ref in o_refs:
        n = o_ref.shape[-1]
        o_ref[...] = _dot(h, w_ref[:, off:off + n]).astype(o_ref.dtype)
        off += n


def _norm_matmul(x, g, w, splits, name):
    t, d = x.shape
    n = w.shape[1]
    assert sum(splits) == n and t % ROW_TILE == 0
    return pl.pallas_call(
        _norm_matmul_body,
        grid=(t // ROW_TILE,),
        in_specs=[
            pl.BlockSpec((ROW_TILE, d), lambda i: (i, 0)),
            pl.BlockSpec((1, d), lambda i: (0, 0)),
            pl.BlockSpec((d, n), lambda i: (0, 0)),
        ],
        out_specs=[pl.BlockSpec((ROW_TILE, s), lambda i: (i, 0)) for s in splits],
        out_shape=[jax.ShapeDtypeStruct((t, s), BF16) for s in splits],
        compiler_params=_params(("parallel",)),
        name=name,
    )(x, g.reshape(1, d), w)


def _in_proj_body(x_ref, g_ref, w_ref, wvt_ref, gla_ref, dqk_ref, vt_ref):
    h = _rms(x_ref[...], g_ref[...]).astype(BF16)
    gla_ref[...] = _dot(h, w_ref[:, 0:GLA_IN_W]).astype(gla_ref.dtype)
    dqk_ref[...] = _dot(h, w_ref[:, GLA_IN_W:]).astype(dqk_ref.dtype)
    vt_ref[...] = _dot_nt(wvt_ref[...], h).astype(vt_ref.dtype)


def _in_proj(x, g, w, w_vt):
    t, d = x.shape
    n = w.shape[1]
    tm = ROW_TILE
    assert n == GLA_IN_W + 2 * DIFF_QK_W and t % tm == 0
    return pl.pallas_call(
        _in_proj_body,
        grid=(t // tm,),
        in_specs=[
            pl.BlockSpec((tm, d), lambda i: (i, 0)),
            pl.BlockSpec((1, d), lambda i: (0, 0)),
            pl.BlockSpec((d, n), lambda i: (0, 0)),
            pl.BlockSpec((DIFF_V_W, d), lambda i: (0, 0)),
        ],
        out_specs=[
            pl.BlockSpec((tm, GLA_IN_W), lambda i: (i, 0)),
            pl.BlockSpec((tm, 2 * DIFF_QK_W), lambda i: (i, 0)),
            pl.BlockSpec((DIFF_V_W, tm), lambda i: (0, i)),
        ],
        out_shape=[
            jax.ShapeDtypeStruct((t, GLA_IN_W), BF16),
            jax.ShapeDtypeStruct((t, 2 * DIFF_QK_W), BF16),
            jax.ShapeDtypeStruct((DIFF_V_W, t), BF16),
        ],
        compiler_params=_params(("parallel",)),
        name="in_proj",
    )(x, g.reshape(1, d), w, w_vt)


def _split3(x):
    hi = x.astype(BF16)
    r1 = x - hi.astype(F32)
    mid = r1.astype(BF16)
    lo = (r1 - mid.astype(F32)).astype(BF16)
    return hi, mid, lo


def _gla_body(in_ref, wa_ref, ba_ref, gn_ref, o_ref, s_ref):
    c_ = GLA_CHUNK

    @pl.when(pl.program_id(1) == 0)
    def _():
        s_ref[...] = jnp.zeros_like(s_ref)

    row = lax.broadcasted_iota(jnp.int32, (c_, c_), 0)
    col = lax.broadcasted_iota(jnp.int32, (c_, c_), 1)
    causal = row >= col
    tri = causal.astype(BF16)
    ones = jnp.ones((c_, GLA_DV), BF16)
    gn = gn_ref[...]
    q0, k0, v0, r0, a0 = 0, GLA_QK_W, 2 * GLA_QK_W, 2 * GLA_QK_W + GLA_V_W, 2 * GLA_QK_W + 2 * GLA_V_W

    for c in range(GLA_ROWS // c_):
        rows = pl.ds(c * c_, c_)
        q = in_ref[rows, q0:q0 + GLA_QK_W].astype(F32)
        k = in_ref[rows, k0:k0 + GLA_QK_W].astype(F32)
        v = in_ref[rows, v0:v0 + GLA_V_W]
        r = in_ref[rows, r0:r0 + GLA_V_W].astype(F32)
        a = in_ref[rows, a0:a0 + A_PAD]
        z = _dot(a, wa_ref[...]) + ba_ref[...]
        log_a = jax.nn.log_sigmoid(z) / GLA_TAU
        parts = _split3(log_a)
        cum = _dot(tri, parts[0]) + _dot(tri, parts[1]) + _dot(tri, parts[2])
        cum_last = cum[c_ - 1:c_, :]
        q_e = q * (GLA_DK ** -0.5) * jnp.exp(cum)
        k_e = k * jnp.exp(-cum)
        k_s = k * jnp.exp(cum_last - cum)
        for h in range(GLA_HEADS):
            ks_ = slice(h * GLA_DK, (h + 1) * GLA_DK)
            vs_ = slice(h * GLA_DV, (h + 1) * GLA_DV)
            qe_h = q_e[:, ks_].astype(BF16)
            ke_h = k_e[:, ks_].astype(BF16)
            ksc_h = k_s[:, ks_].astype(BF16)
            v_h = v[:, vs_]
            att = jnp.where(causal, _dot_nt(qe_h, ke_h), 0.0).astype(BF16)
            s_old = s_ref[h]
            o_h = _dot(att, v_h) + _dot(qe_h, s_old.astype(BF16))
            dsum = (_dot_tn(parts[0][:, ks_], ones) + _dot_tn(parts[1][:, ks_], ones)
                    + _dot_tn(parts[2][:, ks_], ones))
            s_ref[h] = jnp.exp(dsum) * s_old + _dot_tn(ksc_h, v_h)
            o_ref[rows, vs_] = (_rms(o_h, gn) * jax.nn.silu(r[:, vs_])).astype(o_ref.dtype)


def _gla(gla_in, w_a2p, b_a, gn, batch, seq):
    t = gla_in.shape[0]
    assert seq % GLA_ROWS == 0 and GLA_ROWS % GLA_CHUNK == 0
    nsb = seq // GLA_ROWS
    return pl.pallas_call(
        _gla_body,
        grid=(batch, nsb),
        in_specs=[
            pl.BlockSpec((GLA_ROWS, GLA_IN_W), lambda b, j: (b * nsb + j, 0)),
            pl.BlockSpec((A_PAD, GLA_QK_W), lambda b, j: (0, 0)),
            pl.BlockSpec((1, GLA_QK_W), lambda b, j: (0, 0)),
            pl.BlockSpec((1, GLA_DV), lambda b, j: (0, 0)),
        ],
        out_specs=pl.BlockSpec((GLA_ROWS, GLA_V_W), lambda b, j: (b * nsb + j, 0)),
        out_shape=jax.ShapeDtypeStruct((t, GLA_V_W), BF16),
        scratch_shapes=[pltpu.VMEM((GLA_HEADS, GLA_DK, GLA_DV), F32)],
        compiler_params=_params(("parallel", "arbitrary")),
        name="gla",
    )(gla_in, w_a2p, b_a.reshape(1, -1), gn.reshape(1, -1))


def _alibi_features(seq):
    j = np.arange(seq) % ATT_TILE
    j_lo = j % 256
    j_hi = j - j_lo
    kf = np.zeros((seq, 2 * DIFF_DQK), np.float32)
    for base in (0, DIFF_DQK):
        for f in range(3):
            kf[:, base + f] = j_lo
            kf[:, base + 3 + f] = j_hi
    slopes = 2.0 ** (-8.0 * np.arange(1, DIFF_HEADS + 1) / DIFF_HEADS)
    c = jnp.asarray(slopes * math.log2(math.e), dtype=F32)
    c1 = c.astype(BF16)
    c2 = (c - c1.astype(F32)).astype(BF16)
    c3 = (c - c1.astype(F32) - c2.astype(F32)).astype(BF16)
    terms = jnp.stack([c1, c2, c3, c1, c2, c3], axis=1)
    qf = jnp.zeros((DIFF_HEADS, 2 * DIFF_DQK), BF16)
    qf = qf.at[:, 0:6].set(terms).at[:, DIFF_DQK:DIFF_DQK + 6].set(terms)
    qf = jnp.repeat(qf, 8, axis=0)
    return jnp.asarray(kf, dtype=BF16), qf, c


def _diff_body(c_ref, q_ref, k_ref, vt_ref, qf_ref, kf_ref, lam_ref, gn_ref, o_ref, kaug_ref, vta_ref, m_ref, acc_ref):
    tq = ATT_TILE
    h = pl.program_id(1)
    qi = pl.program_id(2)
    seq = k_ref.shape[0]
    c = c_ref[h]

    @pl.when(qi == 0)
    def _():
        k = k_ref[...]
        kf = kf_ref[...]
        lane = lax.broadcasted_iota(jnp.int32, k.shape, 1)
        kaug_ref[0] = jnp.where(lane < DIFF_DQK, k, kf)
        kaug_ref[1] = jnp.where(lane >= DIFF_DQK, k, kf)
        for jj in range(seq // tq):
            vta_ref[jj, 0:DIFF_DV, :] = vt_ref[:, jj * tq:(jj + 1) * tq]
            vta_ref[jj, DIFF_DV:, :] = jnp.ones((ATT_ONES, tq), BF16)

    q = q_ref[...]
    qf = jnp.broadcast_to(qf_ref[0:1, :], q.shape)
    lane = lax.broadcasted_iota(jnp.int32, q.shape, 1)
    q_maps = (jnp.where(lane < DIFF_DQK, q, qf), jnp.where(lane >= DIFF_DQK, q, qf))
    row = lax.broadcasted_iota(jnp.int32, (tq, tq), 0)
    col = lax.broadcasted_iota(jnp.int32, (tq, tq), 1)
    visible = row <= col

    m_ref[...] = jnp.full_like(m_ref, -jnp.inf)
    acc_ref[...] = jnp.zeros_like(acc_ref)

    def kv_step(j, masked):
        koff = pl.multiple_of(j * tq, tq)
        vblk = vta_ref[j]
        base = c * (j * tq).astype(F32)
        for m in range(2):
            s_t = _dot_nt(kaug_ref[m, pl.ds(koff, tq), :], q_maps[m])
            if masked:
                s_t = jnp.where(visible, s_t, -jnp.inf)
            m_old = m_ref[m]
            m_new = jnp.maximum(m_old, jnp.max(s_t, axis=0, keepdims=True) + base)
            alpha = jnp.exp2(m_old - m_new)
            p_t = jnp.exp2(s_t - (m_new - base)).astype(BF16)
            acc_ref[m] = alpha * acc_ref[m] + _dot(vblk, p_t)
            m_ref[m] = m_new

    def full_step(j, carry):
        kv_step(j, False)
        return carry

    lax.fori_loop(0, qi, full_step, 0)
    kv_step(qi, True)

    lq1, lk1, lq2, lk2 = (lam_ref[i:i + 1, :] for i in range(4))
    lam = (jnp.exp(jnp.sum(lq1 * lk1, axis=-1, keepdims=True))
           - jnp.exp(jnp.sum(lq2 * lk2, axis=-1, keepdims=True)) + LAMBDA_INIT)
    acc0 = acc_ref[0]
    acc1 = acc_ref[1]
    o_t = (acc0[0:DIFF_DV] / acc0[DIFF_DV:DIFF_DV + 1] - lam * (acc1[0:DIFF_DV] / acc1[DIFF_DV:DIFF_DV + 1]))
    o = o_t.T
    o_ref[...] = (_rms(o, gn_ref[...]) * (1.0 - LAMBDA_INIT)).astype(o_ref.dtype)


def _diff_attn(dqk, v_t, lam_params, gn, batch, seq):
    t = dqk.shape[0]
    tq = ATT_TILE
    assert seq % tq == 0
    nq = seq // tq
    kfeat, qfeat, c = _alibi_features(seq)
    rows = DIFF_DV + ATT_ONES
    grid_spec = pltpu.PrefetchScalarGridSpec(
        num_scalar_prefetch=1,
        grid=(batch, DIFF_HEADS, nq),
        in_specs=[
            pl.BlockSpec((tq, 2 * DIFF_DQK), lambda b, h, i, s: (b * nq + i, h)),
            pl.BlockSpec((seq, 2 * DIFF_DQK), lambda b, h, i, s: (b, DIFF_HEADS + h)),
            pl.BlockSpec((DIFF_DV, seq), lambda b, h, i, s: (h, b)),
            pl.BlockSpec((8, 2 * DIFF_DQK), lambda b, h, i, s: (h, 0)),
            pl.BlockSpec((seq, 2 * DIFF_DQK), lambda b, h, i, s: (0, 0)),
            pl.BlockSpec((4, DIFF_DQK), lambda b, h, i, s: (0, 0)),
            pl.BlockSpec((1, DIFF_DV), lambda b, h, i, s: (0, 0)),
        ],
        out_specs=pl.BlockSpec((tq, DIFF_DV), lambda b, h, i, s: (b * nq + i, h)),
        scratch_shapes=[
            pltpu.VMEM((2, seq, 2 * DIFF_DQK), BF16),
            pltpu.VMEM((nq, rows, tq), BF16),
            pltpu.VMEM((2, 1, tq), F32),
            pltpu.VMEM((2, rows, tq), F32),
        ],
    )
    return pl.pallas_call(
        _diff_body,
        grid_spec=grid_spec,
        out_shape=jax.ShapeDtypeStruct((t, DIFF_V_W), BF16),
        compiler_params=_params(("parallel", "parallel", "arbitrary")),
        name="diff_attn",
    )(c, dqk, dqk, v_t, qfeat, kfeat, lam_params, gn.reshape(1, -1))


def _route(logits):
    lane = lax.broadcasted_iota(jnp.int32, logits.shape, 1)
    big = jnp.int32(LANES)
    neg = -jnp.inf

    def first_argmax(vals, vmax):
        return jnp.min(jnp.where(vals == vmax, lane, big), axis=-1, keepdims=True)

    is_grp = (lane >= N_EXPERTS) & (lane < N_EXPERTS + N_GROUPS)
    lg = jnp.where(is_grp, logits, neg)
    mg = jnp.max(lg, axis=-1, keepdims=True)
    p_g = 1.0 / jnp.sum(jnp.exp(lg - mg), axis=-1, keepdims=True)
    g_sel = first_argmax(lg, mg) - N_EXPERTS
    in_grp = (lane >= g_sel * EXPERTS_PER_GROUP) & (lane < (g_sel + 1) * EXPERTS_PER_GROUP)
    le = jnp.where(in_grp, logits, neg)
    m1 = jnp.max(le, axis=-1, keepdims=True)
    i1 = first_argmax(le, m1)
    le2 = jnp.where(lane == i1, neg, le)
    m2 = jnp.max(le2, axis=-1, keepdims=True)
    i2 = first_argmax(le2, m2)
    den = jnp.sum(jnp.exp(le - m1), axis=-1, keepdims=True)
    w1 = 1.0 / den
    w2 = jnp.exp(m2 - m1) / den
    wsum = w1 + w2
    g1 = p_g * w1 / wsum
    g2 = p_g * w2 / wsum
    first_is_lo = i1 < i2
    g_lo = jnp.where(first_is_lo, g1, g2)
    g_hi = jnp.where(first_is_lo, g2, g1)
    a = jnp.minimum(i1, i2) - g_sel * EXPERTS_PER_GROUP
    b = jnp.maximum(i1, i2) - g_sel * EXPERTS_PER_GROUP
    pair = 3 * a - jnp.where(a == 2, 1, 0) + (b - a - 1)
    cls = (g_sel * N_PAIRS + pair).astype(F32)
    out = jnp.where(lane == 0, g_lo, jnp.where(lane == 1, g_hi, cls))
    return jnp.where(lane < 3, out, 0.0)


def _post_mix_body(x_ref, og_ref, od_ref, wo_ref, gc_ref, wq_ref, km_ref, vm_ref, wco_ref, gf_ref, wr_ref, br_ref,
                   x2e_ref):
    x1 = x_ref[...] + _dot(og_ref[...], wo_ref[0:GLA_V_W, :]) + _dot(od_ref[...], wo_ref[GLA_V_W:, :])
    h2 = _rms(x1, gc_ref[...]).astype(BF16)
    qc = _dot(h2, wq_ref[...]).astype(BF16)
    heads = []
    for h in range(CROSS_HEADS):
        sl = slice(h * CROSS_DH, (h + 1) * CROSS_DH)
        s = _dot_nt(qc[:, sl], km_ref[:, sl]) * (CROSS_DH ** -0.5)
        s = s - jnp.max(s, axis=-1, keepdims=True)
        p = jnp.exp(s)
        p = p / jnp.sum(p, axis=-1, keepdims=True)
        heads.append(_dot(p.astype(BF16), vm_ref[:, sl]).astype(BF16))
    oc = jnp.concatenate(heads, axis=-1)
    x2 = x1 + _dot(oc, wco_ref[...])
    h3 = _rms(x2, gf_ref[...]).astype(BF16)
    x2e_ref[:, 0:D_MODEL] = x2
    x2e_ref[:, D_MODEL:] = _route(_dot(h3, wr_ref[...]) + br_ref[...])


def _post_mix(x, og, od, w_out, g_cross, w_cq, kmem, vmem, w_co, g_ffn, w_r, b_r, seq, mem_len):
    t, d = x.shape
    tm = ROW_TILE
    assert seq % tm == 0
    per_b = seq // tm
    full = lambda shape: pl.BlockSpec(shape, lambda i: (0, 0))
    rows = lambda w: pl.BlockSpec((tm, w), lambda i: (i, 0))
    return pl.pallas_call(
        _post_mix_body,
        grid=(t // tm,),
        in_specs=[
            rows(d), rows(GLA_V_W), rows(DIFF_V_W),
            full((d, d)), full((1, d)), full((d, d)),
            pl.BlockSpec((mem_len, d), lambda i: (i // per_b, 0)),
            pl.BlockSpec((mem_len, d), lambda i: (i // per_b, 0)),
            full((d, d)), full((1, d)), full((d, LANES)), full((1, LANES)),
        ],
        out_specs=rows(ROW_W),
        out_shape=jax.ShapeDtypeStruct((t, ROW_W), F32),
        compiler_params=_params(("parallel",)),
        name="post_mix",
    )(x, og, od, w_out, g_cross.reshape(1, d), w_cq, kmem, vmem, w_co, g_ffn.reshape(1, d), w_r, b_r)


def _dispatch_body(fill_ref, dest_ref, src_ref, dst_ref, sem):
    n = GATHER_ROWS
    i = pl.program_id(0)
    last = pl.num_programs(0) - 1
    base = i * n

    def issue(r8, carry):
        for u in range(8):
            r = r8 * 8 + u
            pltpu.make_async_copy(src_ref.at[pl.ds(base + r, 1)], dst_ref.at[pl.ds(dest_ref[0, 0, r], 1)],
                                  sem.at[0]).start()
        return carry

    lax.fori_loop(0, n // 8, issue, 0)

    def fill_padding(start):
        def fill_copy(r, size):
            cp = pltpu.make_async_copy(src_ref.at[pl.ds(0, size)], dst_ref.at[pl.ds(r, size)], sem.at[1])
            cp.start() if start else cp.wait()

        def per_class(c, carry):
            lo = fill_ref[3 * c]
            mid = fill_ref[3 * c + 1]
            hi = fill_ref[3 * c + 2]
            lax.fori_loop(lo, mid, lambda r, cc: (fill_copy(r, 1), cc)[1], 0)
            chunks = (hi - mid) // SUBLANES
            off = mid
            for bit in reversed(range((MOE_ROWS // SUBLANES).bit_length() - 1)):
                size = SUBLANES << bit
                has = (chunks & (1 << bit)) != 0

                @pl.when(has)
                def _(off=off, size=size):
                    fill_copy(pl.multiple_of(off, SUBLANES), size)

                off = off + jnp.where(has, size, 0)
            return carry

        lax.fori_loop(0, N_CLASSES, per_class, 0)
        n_blocks = dst_ref.shape[0] // MOE_ROWS
        lax.fori_loop(fill_ref[3 * N_CLASSES], n_blocks,
                      lambda b, cc: (fill_copy(pl.multiple_of(b * MOE_ROWS, MOE_ROWS), MOE_ROWS), cc)[1], 0)

    @pl.when(i == 0)
    def _():
        fill_padding(True)
        fill_padding(False)

    def wait_block():
        pltpu.make_async_copy(src_ref.at[pl.ds(0, n)], dst_ref.at[pl.ds(0, n)], sem.at[0]).wait()

    @pl.when(i > 0)
    def _():
        wait_block()

    @pl.when(i == last)
    def _():
        wait_block()


def _dispatch(src, dest, fill, n_rows):
    t, w = src.shape
    assert t % GATHER_ROWS == 0
    nb = t // GATHER_ROWS
    grid_spec = pltpu.PrefetchScalarGridSpec(
        num_scalar_prefetch=1,
        grid=(nb,),
        in_specs=[
            pl.BlockSpec((1, 1, GATHER_ROWS), lambda i, f: (i, 0, 0), memory_space=pltpu.SMEM),
            pl.BlockSpec(memory_space=pl.ANY),
        ],
        out_specs=pl.BlockSpec(memory_space=pl.ANY),
        scratch_shapes=[pltpu.SemaphoreType.DMA((2,))],
    )
    return pl.pallas_call(
        _dispatch_body,
        grid_spec=grid_spec,
        out_shape=jax.ShapeDtypeStruct((n_rows, w), src.dtype),
        compiler_params=_params(("arbitrary",)),
        name="dispatch",
    )(fill, dest.reshape(nb, 1, GATHER_ROWS), src)


def _expert_body(lo_ref, hi_ref, nvalid_ref, xs_ref, gf_ref, wg_a, wu_a, wd_a, wg_b, wu_b, wd_b, ys_ref):
    i = pl.program_id(0)

    @pl.when(i < nvalid_ref[0])
    def _():
        xb = xs_ref[...]
        h = _rms(xb[:, 0:D_MODEL], gf_ref[...]).astype(BF16)

        def ffn(wg, wu, wd):
            hid = (jax.nn.silu(_dot(h, wg[...])) * _dot(h, wu[...])).astype(BF16)
            return _dot(hid, wd[...])

        g_lo = xb[:, D_MODEL:D_MODEL + 1]
        g_hi = xb[:, D_MODEL + 1:D_MODEL + 2]
        ys_ref[...] = g_lo * ffn(wg_a, wu_a, wd_a) + g_hi * ffn(wg_b, wu_b, wd_b)

    @pl.when(i >= nvalid_ref[0])
    def _():
        ys_ref[...] = jnp.zeros_like(ys_ref)


def _experts(xs, g_ffn, blk_lo, blk_hi, nvalid, w_g, w_u, w_d):
    n_rows = xs.shape[0]
    nb = n_rows // MOE_ROWS
    d, de = w_g.shape[1], w_g.shape[2]
    up = lambda sel: pl.BlockSpec((None, d, de), lambda i, lo, hi, nv: (sel(lo, hi)[i], 0, 0))
    down = lambda sel: pl.BlockSpec((None, de, d), lambda i, lo, hi, nv: (sel(lo, hi)[i], 0, 0))
    first = lambda lo, hi: lo
    second = lambda lo, hi: hi
    grid_spec = pltpu.PrefetchScalarGridSpec(
        num_scalar_prefetch=3,
        grid=(nb,),
        in_specs=[
            pl.BlockSpec((MOE_ROWS, ROW_W), lambda i, lo, hi, nv: (jnp.minimum(i, nv[0] - 1), 0)),
            pl.BlockSpec((1, d), lambda i, lo, hi, nv: (0, 0)),
            up(first), up(first), down(first), up(second), up(second), down(second),
        ],
        out_specs=pl.BlockSpec((MOE_ROWS, d), lambda i, lo, hi, nv: (i, 0)),
    )
    return pl.pallas_call(
        _expert_body,
        grid_spec=grid_spec,
        out_shape=jax.ShapeDtypeStruct((n_rows, d), F32),
        compiler_params=_params(("arbitrary",)),
        name="experts",
    )(blk_lo, blk_hi, nvalid, xs, g_ffn.reshape(1, d), w_g, w_u, w_d, w_g, w_u, w_d)


def _combine_body(cur_ref, nxt_ref, x2_ref, ys_ref, g_ref, o_ref, ybuf, sem):
    n = ROW_TILE
    i = pl.program_id(0)
    last = pl.num_programs(0) - 1
    slot = i % 2

    def issue_tile(idx_ref, s):
        def issue(r8, carry):
            for u in range(8):
                r = r8 * 8 + u
                pltpu.make_async_copy(ys_ref.at[pl.ds(idx_ref[0, 0, r], 1)], ybuf.at[s, pl.ds(r, 1)],
                                      sem.at[s]).start()
            return carry

        lax.fori_loop(0, n // 8, issue, 0)

    @pl.when(i == 0)
    def _():
        issue_tile(cur_ref, 0)

    @pl.when(i < last)
    def _():
        issue_tile(nxt_ref, 1 - slot)

    pltpu.make_async_copy(ys_ref.at[pl.ds(0, n)], ybuf.at[slot], sem.at[slot]).wait()
    o_ref[...] = _rms(x2_ref[...] + ybuf[slot], g_ref[...])


def _combine(x2e, ys, dest, g_final):
    t = x2e.shape[0]
    d = D_MODEL
    tm = ROW_TILE
    nt = t // tm
    idx = dest.reshape(nt, 1, tm)
    return pl.pallas_call(
        _combine_body,
        grid=(nt,),
        in_specs=[
            pl.BlockSpec((1, 1, tm), lambda i: (i, 0, 0), memory_space=pltpu.SMEM),
            pl.BlockSpec((1, 1, tm), lambda i: (jnp.minimum(i + 1, nt - 1), 0, 0), memory_space=pltpu.SMEM),
            pl.BlockSpec((tm, d), lambda i: (i, 0)),
            pl.BlockSpec(memory_space=pl.ANY),
            pl.BlockSpec((1, d), lambda i: (0, 0)),
        ],
        out_specs=pl.BlockSpec((tm, d), lambda i: (i, 0)),
        out_shape=jax.ShapeDtypeStruct((t, d), F32),
        scratch_shapes=[pltpu.VMEM((2, tm, d), F32), pltpu.SemaphoreType.DMA((2,))],
        compiler_params=_params(("arbitrary",)),
        name="combine",
    )(idx, idx, x2e, ys, g_final.reshape(1, d))


def _dispatch_plan(cls, t):
    onehot = (cls[:, None] == jnp.arange(N_CLASSES, dtype=jnp.int32)[None, :]).astype(jnp.int32)
    ranks = jnp.cumsum(onehot, axis=0) - onehot
    counts = jnp.sum(onehot, axis=0)
    padded = ((counts + MOE_ROWS - 1) // MOE_ROWS) * MOE_ROWS
    pad_end = jnp.cumsum(padded)
    pad_start = pad_end - padded
    dest = jnp.sum(onehot * (ranks + pad_start[None, :]), axis=1).astype(jnp.int32)
    fill_lo = pad_start + counts
    fill_mid = jnp.minimum(((fill_lo + SUBLANES - 1) // SUBLANES) * SUBLANES, pad_end)
    nvalid = (pad_end[-1] // MOE_ROWS).astype(jnp.int32).reshape(1)
    fill = jnp.concatenate([jnp.stack([fill_lo, fill_mid, pad_end], axis=1).reshape(-1), nvalid]).astype(jnp.int32)
    n_blocks = t // MOE_ROWS + N_CLASSES
    blk_start = jnp.arange(n_blocks, dtype=jnp.int32) * MOE_ROWS
    blk_cls = jnp.minimum(jnp.sum((blk_start[:, None] >= pad_end[None, :]).astype(jnp.int32), axis=1), N_CLASSES - 1)
    grp = blk_cls // N_PAIRS
    pair_onehot = ((blk_cls % N_PAIRS)[:, None] == jnp.arange(N_PAIRS, dtype=jnp.int32)[None, :]).astype(jnp.int32)
    blk_lo = grp * EXPERTS_PER_GROUP + jnp.sum(pair_onehot * jnp.asarray(PAIR_LO, jnp.int32)[None, :], axis=1)
    blk_hi = grp * EXPERTS_PER_GROUP + jnp.sum(pair_onehot * jnp.asarray(PAIR_HI, jnp.int32)[None, :], axis=1)
    return dest, fill, blk_lo.astype(jnp.int32), blk_hi.astype(jnp.int32), nvalid, n_blocks * MOE_ROWS


def _regroup_w_in(w_in):
    splits = np.cumsum([GLA_QK_W, GLA_QK_W, GLA_V_W, GLA_RANK, GLA_V_W, DIFF_QK_W, DIFF_QK_W, DIFF_V_W])[:-1]
    q_g, k_g, v_g, a_lr, r_g, q_d, k_d, v_d = jnp.split(w_in, [int(i) for i in splits], axis=1)
    a_pad = jnp.pad(a_lr, ((0, 0), (0, A_PAD - GLA_RANK)))
    q_d = q_d * (DIFF_DQK ** -0.5 * math.log2(math.e))
    return jnp.concatenate([q_g, k_g, v_g, r_g, a_pad, q_d, k_d], axis=1).astype(BF16), v_d.T.astype(BF16)


def kernel(x, mem, norm_mix_g, w_in, w_gla_a2, b_gla_a, gla_norm_g, diff_norm_g, lambda_q1, lambda_k1, lambda_q2,
           lambda_k2, w_out, norm_cross_g, norm_mem_g, w_cq, w_ckv, w_co, norm_ffn_g, w_router_grp, b_router_grp,
           w_router_exp, b_router_exp, w_e_gate, w_e_up, w_e_down, norm_final_g):
    b_, s_, d_ = x.shape
    m_ = mem.shape[1]
    t = b_ * s_
    xf = x.reshape(t, d_)

    kmem, vmem = _norm_matmul(mem.reshape(b_ * m_, d_), norm_mem_g[0], w_ckv[0].astype(BF16), (d_, d_), "mem_kv")

    w_cols, w_vt = _regroup_w_in(w_in[0])
    gla_in, dqk, v_t = _in_proj(xf, norm_mix_g[0], w_cols, w_vt)

    w_a2p = jnp.pad(w_gla_a2[0], ((0, A_PAD - GLA_RANK), (0, 0))).astype(BF16)
    o_g = _gla(gla_in, w_a2p, b_gla_a[0], gla_norm_g[0], b_, s_)

    lam_params = jnp.stack([lambda_q1[0], lambda_k1[0], lambda_q2[0], lambda_k2[0]]).astype(F32)
    o_d = _diff_attn(dqk, v_t, lam_params, diff_norm_g[0], b_, s_)

    pad_r = LANES - N_EXPERTS - N_GROUPS
    w_r = jnp.pad(jnp.concatenate([w_router_exp[0], w_router_grp[0]], axis=1), ((0, 0), (0, pad_r))).astype(BF16)
    b_r = jnp.pad(jnp.concatenate([b_router_exp[0], b_router_grp[0]]), (0, pad_r)).reshape(1, LANES)
    x2e = _post_mix(xf, o_g, o_d, w_out[0].astype(BF16), norm_cross_g[0], w_cq[0].astype(BF16), kmem, vmem,
                    w_co[0].astype(BF16), norm_ffn_g[0], w_r, b_r, s_, m_)

    cls = x2e[:, D_MODEL + 2].astype(jnp.int32)
    dest, fill, blk_lo, blk_hi, nvalid, n_rows = _dispatch_plan(cls, t)
    xs = _dispatch(x2e, dest, fill, n_rows)
    ys = _experts(xs, norm_ffn_g[0], blk_lo, blk_hi, nvalid, w_e_gate[0].astype(BF16), w_e_up[0].astype(BF16),
                  w_e_down[0].astype(BF16))

    out = _combine(x2e, ys, dest, norm_final_g)
    return out.reshape(b_, s_, d_)
```

```python
import functools
import math

import jax
import jax.numpy as jnp
import numpy as np
from jax import lax
from jax.experimental import pallas as pl
from jax.experimental.pallas import tpu as pltpu

F32 = jnp.float32
BF16 = jnp.bfloat16
U32 = jnp.uint32

EPS = 1e-6
D_MODEL = 1024
GLA_HEADS = 4
GLA_DK = 64
GLA_DV = 128
GLA_RANK = 16
GLA_TAU = 16.0
GLA_CHUNK = 64
GLA_QK_W = GLA_HEADS * GLA_DK
GLA_V_W = GLA_HEADS * GLA_DV
DIFF_HEADS = 4
DIFF_DQK = 64
DIFF_DV = 128
DIFF_QK_W = DIFF_HEADS * 2 * DIFF_DQK
DIFF_V_W = DIFF_HEADS * DIFF_DV
CROSS_HEADS = 4
CROSS_DH = D_MODEL // CROSS_HEADS
N_GROUPS = 4
EXPERTS_PER_GROUP = 4
N_EXPERTS = N_GROUPS * EXPERTS_PER_GROUP
TOP_K = 2
D_EXPERT = 512
LAMBDA_INIT = 0.8 - 0.6 * math.exp(-0.3 * 0)

LANES = 128
SUBLANES = 8
A_PAD = LANES
GLA_IN_W = 2 * GLA_QK_W + 2 * GLA_V_W + A_PAD
ROW_W = D_MODEL + LANES
PAIR_LO = (0, 0, 0, 1, 1, 2)
PAIR_HI = (1, 2, 3, 2, 3, 3)
N_PAIRS = len(PAIR_LO)
N_CLASSES = N_GROUPS * N_PAIRS

ROW_TILE = 512
GLA_ROWS = 512
ATT_TILE = 512
ATT_ONES = 16
MOE_ROWS = 512
GATHER_ROWS = 1024
VMEM_LIMIT = 56 * 1024 * 1024


def _params(sem, vmem=VMEM_LIMIT):
    return pltpu.CompilerParams(dimension_semantics=sem, vmem_limit_bytes=vmem)


def _rms(x, g):
    ms = jnp.mean(x * x, axis=-1, keepdims=True)
    return x * lax.rsqrt(ms + EPS) * g


def _dot(a, b):
    return jnp.dot(a, b, preferred_element_type=F32)


def _dot_nt(a, b):
    return lax.dot_general(a, b, (((1,), (1,)), ((), ())), preferred_element_type=F32)


def _norm_matmul_body(x_ref, g_ref, w_ref, *o_refs):
    h = _rms(x_ref[...], g_ref[...]).astype(BF16)
    off = 0
    for o_ref in o_refs:
        n = o_ref.shape[-1]
        o_ref[...] = _dot(h, w_ref[:, off:off + n]).astype(o_ref.dtype)
        off += n


def _norm_matmul(x, g, w, splits, name):
    t, d = x.shape
    n = w.shape[1]
    assert sum(splits) == n and t % ROW_TILE == 0
    return pl.pallas_call(
        _norm_matmul_body,
        grid=(t // ROW_TILE,),
        in_specs=[
            pl.BlockSpec((ROW_TILE, d), lambda i: (i, 0)),
            pl.BlockSpec((1, d), lambda i: (0, 0)),
            pl.BlockSpec((d, n), lambda i: (0, 0)),
        ],
        out_specs=[pl.BlockSpec((ROW_TILE, s), lambda i: (i, 0)) for s in splits],
        out_shape=[jax.ShapeDtypeStruct((t, s), BF16) for s in splits],
        compiler_params=_params(("parallel",)),
        name=name,
    )(x, g.reshape(1, d), w)


def _in_proj_body(x_ref, g_ref, w_ref, wvt_ref, gla_ref, dqk_ref, vt_ref):
    h = _rms(x_ref[...], g_ref[...]).astype(BF16)
    gla_ref[...] = _dot(h, w_ref[:, 0:GLA_IN_W]).astype(gla_ref.dtype)
    dqk_ref[...] = _dot(h, w_ref[:, GLA_IN_W:]).astype(dqk_ref.dtype)
    vt_ref[...] = _dot_nt(wvt_ref[...], h).astype(vt_ref.dtype)


def _in_proj(x, g, w, w_vt):
    t, d = x.shape
    n = w.shape[1]
    tm = ROW_TILE
    assert n == GLA_IN_W + 2 * DIFF_QK_W and t % tm == 0
    return pl.pallas_call(
        _in_proj_body,
        grid=(t // tm,),
        in_specs=[
            pl.BlockSpec((tm, d), lambda i: (i, 0)),
            pl.BlockSpec((1, d), lambda i: (0, 0)),
            pl.BlockSpec((d, n), lambda i: (0, 0)),
            pl.BlockSpec((DIFF_V_W, d), lambda i: (0, 0)),
        ],
        out_specs=[
            pl.BlockSpec((tm, GLA_IN_W), lambda i: (i, 0)),
            pl.BlockSpec((tm, 2 * DIFF_QK_W), lambda i: (i, 0)),
            pl.BlockSpec((DIFF_V_W, tm), lambda i: (0, i)),
        ],
        out_shape=[
            jax.ShapeDtypeStruct((t, GLA_IN_W), BF16),
            jax.ShapeDtypeStruct((t, 2 * DIFF_QK_W), BF16),
            jax.ShapeDtypeStruct((DIFF_V_W, t), BF16),
        ],
        compiler_params=_params(("parallel",)),
        name="in_proj",
    )(x, g.reshape(1, d), w, w_vt)


def _chunk_prefix_operator():
    i = np.arange(GLA_ROWS)
    same_chunk = (i[:, None] // GLA_CHUNK) == (i[None, :] // GLA_CHUNK)
    return jnp.asarray(same_chunk & (i[None, :] <= i[:, None]), dtype=BF16)


def _gla_body(in_ref, tri_ref, wa_ref, ba_ref, gn_ref, o_ref, s_ref):
    c_ = GLA_CHUNK
    n_c = GLA_ROWS // c_

    @pl.when(pl.program_id(1) == 0)
    def _():
        s_ref[...] = jnp.zeros_like(s_ref)

    q0, k0, v0, r0, a0 = 0, GLA_QK_W, 2 * GLA_QK_W, 2 * GLA_QK_W + GLA_V_W, 2 * GLA_QK_W + 2 * GLA_V_W
    z = _dot(in_ref[:, a0:a0 + A_PAD], wa_ref[...]) + ba_ref[...]
    log_a = jax.nn.log_sigmoid(z) / GLA_TAU
    la_hi = log_a.astype(BF16)
    la_lo = (log_a - la_hi.astype(F32)).astype(BF16)
    cum = _dot(tri_ref[...], la_hi) + _dot(tri_ref[...], la_lo)
    q = in_ref[:, q0:q0 + GLA_QK_W].astype(F32)
    k = in_ref[:, k0:k0 + GLA_QK_W].astype(F32)
    q_e = (q * (GLA_DK ** -0.5) * jnp.exp(cum)).astype(BF16)
    k_e = (k * jnp.exp(-cum)).astype(BF16)
    tots = [cum[(c + 1) * c_ - 1:(c + 1) * c_, :] for c in range(n_c)]
    k_s = jnp.concatenate([(k[c * c_:(c + 1) * c_] * jnp.exp(tots[c] - cum[c * c_:(c + 1) * c_])).astype(BF16)
                           for c in range(n_c)], axis=0)

    causal = (lax.broadcasted_iota(jnp.int32, (c_, c_), 0) >= lax.broadcasted_iota(jnp.int32, (c_, c_), 1))
    first_half = {n: lax.broadcasted_iota(jnp.int32, (n, LANES), 1) < GLA_DK for n in (GLA_ROWS, GLA_DV)}
    gn = gn_ref[...]

    for h in range(GLA_HEADS):
        tile = slice((h // 2) * LANES, (h // 2 + 1) * LANES)
        vs_ = slice(h * GLA_DV, (h + 1) * GLA_DV)
        head_lanes = first_half[GLA_ROWS] if h % 2 == 0 else ~first_half[GLA_ROWS]
        qe_m = jnp.where(head_lanes, q_e[:, tile], jnp.zeros((), BF16))
        ke_t = k_e[:, tile]
        ks_t = k_s[:, tile]
        v_h = in_ref[:, v0 + h * GLA_DV:v0 + (h + 1) * GLA_DV]
        vt_h = v_h.astype(F32).T.astype(BF16)
        o_intra, kv_t = [], []
        for c in range(n_c):
            rows = slice(c * c_, (c + 1) * c_)
            att = jnp.where(causal, _dot_nt(qe_m[rows], ke_t[rows]), 0.0).astype(BF16)
            o_intra.append(_dot(att, v_h[rows]))
            pair = slice((c // 2) * LANES, (c // 2 + 1) * LANES)
            chunk_lanes = first_half[GLA_DV] if c % 2 == 0 else ~first_half[GLA_DV]
            kv_t.append(_dot(jnp.where(chunk_lanes, vt_h[:, pair], jnp.zeros((), BF16)), ks_t[pair]))
        st = s_ref[h]
        s_prev = []
        for c in range(n_c):
            s_prev.append(st.astype(BF16))
            st = st * jnp.exp(tots[c][:, tile]) + kv_t[c]
        s_ref[h] = st
        r_h = in_ref[:, r0 + h * GLA_DV:r0 + (h + 1) * GLA_DV].astype(F32)
        for c in range(n_c):
            rows = slice(c * c_, (c + 1) * c_)
            o_h = o_intra[c] + _dot_nt(qe_m[rows], s_prev[c])
            o_ref[rows, vs_] = (_rms(o_h, gn) * jax.nn.silu(r_h[rows])).astype(o_ref.dtype)


def _gla(gla_in, w_a2p, b_a, gn, batch, seq):
    t = gla_in.shape[0]
    assert seq % GLA_ROWS == 0 and GLA_ROWS % (2 * GLA_CHUNK) == 0 and 2 * GLA_DK == LANES and 2 * GLA_CHUNK == LANES
    nsb = seq // GLA_ROWS
    return pl.pallas_call(
        _gla_body,
        grid=(batch, nsb),
        in_specs=[
            pl.BlockSpec((GLA_ROWS, GLA_IN_W), lambda b, j: (b * nsb + j, 0)),
            pl.BlockSpec((GLA_ROWS, GLA_ROWS), lambda b, j: (0, 0)),
            pl.BlockSpec((A_PAD, GLA_QK_W), lambda b, j: (0, 0)),
            pl.BlockSpec((1, GLA_QK_W), lambda b, j: (0, 0)),
            pl.BlockSpec((1, GLA_DV), lambda b, j: (0, 0)),
        ],
        out_specs=pl.BlockSpec((GLA_ROWS, GLA_V_W), lambda b, j: (b * nsb + j, 0)),
        out_shape=jax.ShapeDtypeStruct((t, GLA_V_W), BF16),
        scratch_shapes=[pltpu.VMEM((GLA_HEADS, GLA_DV, LANES), F32)],
        compiler_params=_params(("parallel", "arbitrary")),
        name="gla",
    )(gla_in, _chunk_prefix_operator(), w_a2p, b_a.reshape(1, -1), gn.reshape(1, -1))


def _alibi_features(seq):
    j = np.arange(seq) % ATT_TILE
    j_lo = j % 256
    j_hi = j - j_lo
    kf = np.zeros((seq, 2 * DIFF_DQK), np.float32)
    for base in (0, DIFF_DQK):
        for f in range(3):
            kf[:, base + f] = j_lo
            kf[:, base + 3 + f] = j_hi
    slopes = 2.0 ** (-8.0 * np.arange(1, DIFF_HEADS + 1) / DIFF_HEADS)
    c = jnp.asarray(slopes * math.log2(math.e), dtype=F32)
    c1 = c.astype(BF16)
    c2 = (c - c1.astype(F32)).astype(BF16)
    c3 = (c - c1.astype(F32) - c2.astype(F32)).astype(BF16)
    terms = jnp.stack([c1, c2, c3, c1, c2, c3], axis=1)
    qf = jnp.zeros((DIFF_HEADS, 2 * DIFF_DQK), BF16)
    qf = qf.at[:, 0:6].set(terms).at[:, DIFF_DQK:DIFF_DQK + 6].set(terms)
    qf = jnp.repeat(qf, 8, axis=0)
    return jnp.asarray(kf, dtype=BF16), qf, c


def _diff_body(c_ref, q_ref, k_ref, vt_ref, qf_ref, kf_ref, lam_ref, gn_ref, o_ref, kaug_ref, vta_ref, m_ref, acc_ref,
               s_ref):
    tq = ATT_TILE
    h = pl.program_id(1)
    qi = pl.program_id(2)
    seq = k_ref.shape[0]
    c = c_ref[h]

    @pl.when(qi == 0)
    def _():
        k = k_ref[...]
        kf = kf_ref[...]
        lane = lax.broadcasted_iota(jnp.int32, k.shape, 1)
        kaug_ref[0] = jnp.where(lane < DIFF_DQK, k, kf)
        kaug_ref[1] = jnp.where(lane >= DIFF_DQK, k, kf)
        for jj in range(seq // tq):
            vta_ref[jj, 0:DIFF_DV, :] = vt_ref[:, jj * tq:(jj + 1) * tq]
            vta_ref[jj, DIFF_DV:, :] = jnp.ones((ATT_ONES, tq), BF16)

    q = q_ref[...]
    qf = jnp.broadcast_to(qf_ref[0:1, :], q.shape)
    lane = lax.broadcasted_iota(jnp.int32, q.shape, 1)
    q_maps = (jnp.where(lane < DIFF_DQK, q, qf), jnp.where(lane >= DIFF_DQK, q, qf))
    row = lax.broadcasted_iota(jnp.int32, (tq, tq), 0)
    col = lax.broadcasted_iota(jnp.int32, (tq, tq), 1)
    visible = row <= col

    m_ref[...] = jnp.full_like(m_ref, -jnp.inf)
    acc_ref[...] = jnp.zeros_like(acc_ref)

    def scores(j):
        koff = pl.multiple_of(j * tq, tq)
        return [_dot_nt(kaug_ref[m, pl.ds(koff, tq), :], q_maps[m]) for m in range(2)]

    def consume(j, s_both, masked):
        vblk = vta_ref[j]
        base = c * (j * tq).astype(F32)
        for m in range(2):
            s_t = s_both[m]
            if masked:
                s_t = jnp.where(visible, s_t, -jnp.inf)
            m_old = m_ref[m]
            m_new = jnp.maximum(m_old, jnp.max(s_t, axis=0, keepdims=True) + base)
            alpha = jnp.exp2(m_old - m_new)
            p_t = jnp.exp2(s_t - (m_new - base)).astype(BF16)
            acc_ref[m] = alpha * acc_ref[m] + _dot(vblk, p_t)
            m_ref[m] = m_new

    s_ref[0], s_ref[1] = scores(0)

    def full_step(j, carry):
        s_cur = [s_ref[0], s_ref[1]]
        s_nxt = scores(j + 1)
        consume(j, s_cur, False)
        s_ref[0], s_ref[1] = s_nxt
        return carry

    lax.fori_loop(0, qi, full_step, 0)
    consume(qi, [s_ref[0], s_ref[1]], True)

    lq1, lk1, lq2, lk2 = (lam_ref[i:i + 1, :] for i in range(4))
    lam = (jnp.exp(jnp.sum(lq1 * lk1, axis=-1, keepdims=True))
           - jnp.exp(jnp.sum(lq2 * lk2, axis=-1, keepdims=True)) + LAMBDA_INIT)
    acc0 = acc_ref[0]
    acc1 = acc_ref[1]
    o_t = (acc0[0:DIFF_DV] / acc0[DIFF_DV:DIFF_DV + 1] - lam * (acc1[0:DIFF_DV] / acc1[DIFF_DV:DIFF_DV + 1]))
    o = o_t.T
    o_ref[...] = (_rms(o, gn_ref[...]) * (1.0 - LAMBDA_INIT)).astype(o_ref.dtype)


def _diff_attn(dqk, v_t, lam_params, gn, batch, seq):
    t = dqk.shape[0]
    tq = ATT_TILE
    assert seq % tq == 0
    nq = seq // tq
    kfeat, qfeat, c = _alibi_features(seq)
    rows = DIFF_DV + ATT_ONES
    grid_spec = pltpu.PrefetchScalarGridSpec(
        num_scalar_prefetch=1,
        grid=(batch, DIFF_HEADS, nq),
        in_specs=[
            pl.BlockSpec((tq, 2 * DIFF_DQK), lambda b, h, i, s: (b * nq + i, h)),
            pl.BlockSpec((seq, 2 * DIFF_DQK), lambda b, h, i, s: (b, DIFF_HEADS + h)),
            pl.BlockSpec((DIFF_DV, seq), lambda b, h, i, s: (h, b)),
            pl.BlockSpec((8, 2 * DIFF_DQK), lambda b, h, i, s: (h, 0)),
            pl.BlockSpec((seq, 2 * DIFF_DQK), lambda b, h, i, s: (0, 0)),
            pl.BlockSpec((4, DIFF_DQK), lambda b, h, i, s: (0, 0)),
            pl.BlockSpec((1, DIFF_DV), lambda b, h, i, s: (0, 0)),
        ],
        out_specs=pl.BlockSpec((tq, DIFF_DV), lambda b, h, i, s: (b * nq + i, h)),
        scratch_shapes=[
            pltpu.VMEM((2, seq, 2 * DIFF_DQK), BF16),
            pltpu.VMEM((nq, rows, tq), BF16),
            pltpu.VMEM((2, 1, tq), F32),
            pltpu.VMEM((2, rows, tq), F32),
            pltpu.VMEM((2, tq, tq), F32),
        ],
    )
    return pl.pallas_call(
        _diff_body,
        grid_spec=grid_spec,
        out_shape=jax.ShapeDtypeStruct((t, DIFF_V_W), BF16),
        compiler_params=_params(("parallel", "parallel", "arbitrary")),
        name="diff_attn",
    )(c, dqk, dqk, v_t, qfeat, kfeat, lam_params, gn.reshape(1, -1))


def _route(logits):
    lane = lax.broadcasted_iota(jnp.int32, logits.shape, 1)
    big = jnp.int32(LANES)
    neg = -jnp.inf

    def first_argmax(vals, vmax):
        return jnp.min(jnp.where(vals == vmax, lane, big), axis=-1, keepdims=True)

    is_grp = (lane >= N_EXPERTS) & (lane < N_EXPERTS + N_GROUPS)
    lg = jnp.where(is_grp, logits, neg)
    mg = jnp.max(lg, axis=-1, keepdims=True)
    p_g = 1.0 / jnp.sum(jnp.exp(lg - mg), axis=-1, keepdims=True)
    g_sel = first_argmax(lg, mg) - N_EXPERTS
    in_grp = (lane >= g_sel * EXPERTS_PER_GROUP) & (lane < (g_sel + 1) * EXPERTS_PER_GROUP)
    le = jnp.where(in_grp, logits, neg)
    m1 = jnp.max(le, axis=-1, keepdims=True)
    i1 = first_argmax(le, m1)
    le2 = jnp.where(lane == i1, neg, le)
    m2 = jnp.max(le2, axis=-1, keepdims=True)
    i2 = first_argmax(le2, m2)
    den = jnp.sum(jnp.exp(le - m1), axis=-1, keepdims=True)
    w1 = 1.0 / den
    w2 = jnp.exp(m2 - m1) / den
    wsum = w1 + w2
    g1 = p_g * w1 / wsum
    g2 = p_g * w2 / wsum
    first_is_lo = i1 < i2
    g_lo = jnp.where(first_is_lo, g1, g2)
    g_hi = jnp.where(first_is_lo, g2, g1)
    a = jnp.minimum(i1, i2) - g_sel * EXPERTS_PER_GROUP
    b = jnp.maximum(i1, i2) - g_sel * EXPERTS_PER_GROUP
    pair = 3 * a - jnp.where(a == 2, 1, 0) + (b - a - 1)
    cls = (g_sel * N_PAIRS + pair).astype(F32)
    out = jnp.where(lane == 0, g_lo, jnp.where(lane == 1, g_hi, cls))
    return jnp.where(lane < 3, out, 0.0)


def _post_mix_body(x_ref, og_ref, od_ref, wo_ref, gc_ref, wq_ref, km_ref, vm_ref, wco_ref, gf_ref, wr_ref, br_ref,
                   x2e_ref):
    x1 = x_ref[...] + _dot(og_ref[...], wo_ref[0:GLA_V_W, :]) + _dot(od_ref[...], wo_ref[GLA_V_W:, :])
    h2 = _rms(x1, gc_ref[...]).astype(BF16)
    qc = _dot(h2, wq_ref[...]).astype(BF16)
    heads = []
    for h in range(CROSS_HEADS):
        sl = slice(h * CROSS_DH, (h + 1) * CROSS_DH)
        s = _dot_nt(qc[:, sl], km_ref[:, sl]) * (CROSS_DH ** -0.5)
        s = s - jnp.max(s, axis=-1, keepdims=True)
        p = jnp.exp(s)
        p = p / jnp.sum(p, axis=-1, keepdims=True)
        heads.append(_dot(p.astype(BF16), vm_ref[:, sl]).astype(BF16))
    oc = jnp.concatenate(heads, axis=-1)
    x2 = x1 + _dot(oc, wco_ref[...])
    h3 = _rms(x2, gf_ref[...]).astype(BF16)
    x2e_ref[:, 0:D_MODEL] = x2
    x2e_ref[:, D_MODEL:] = _route(_dot(h3, wr_ref[...]) + br_ref[...])


def _post_mix(x, og, od, w_out, g_cross, w_cq, kmem, vmem, w_co, g_ffn, w_r, b_r, seq, mem_len):
    t, d = x.shape
    tm = ROW_TILE
    assert seq % tm == 0
    per_b = seq // tm
    full = lambda shape: pl.BlockSpec(shape, lambda i: (0, 0))
    rows = lambda w: pl.BlockSpec((tm, w), lambda i: (i, 0))
    return pl.pallas_call(
        _post_mix_body,
        grid=(t // tm,),
        in_specs=[
            rows(d), rows(GLA_V_W), rows(DIFF_V_W),
            full((d, d)), full((1, d)), full((d, d)),
            pl.BlockSpec((mem_len, d), lambda i: (i // per_b, 0)),
            pl.BlockSpec((mem_len, d), lambda i: (i // per_b, 0)),
            full((d, d)), full((1, d)), full((d, LANES)), full((1, LANES)),
        ],
        out_specs=rows(ROW_W),
        out_shape=jax.ShapeDtypeStruct((t, ROW_W), F32),
        compiler_params=_params(("parallel",)),
        name="post_mix",
    )(x, og, od, w_out, g_cross.reshape(1, d), w_cq, kmem, vmem, w_co, g_ffn.reshape(1, d), w_r, b_r)


def _dispatch_body(fill_ref, dest_ref, src_ref, dst_ref, sem):
    n = GATHER_ROWS
    i = pl.program_id(0)

    def issue(r8, carry):
        for u in range(SUBLANES):
            r = r8 * SUBLANES + u
            pltpu.make_async_copy(src_ref.at[pl.ds(r, 1)], dst_ref.at[pl.ds(dest_ref[0, 0, r], 1)],
                                  sem.at[0]).start(priority=u % 2)
        return carry

    lax.fori_loop(0, n // SUBLANES, issue, 0)

    def fill_padding(start):
        def fill_copy(r, size):
            cp = pltpu.make_async_copy(src_ref.at[pl.ds(0, size)], dst_ref.at[pl.ds(r, size)], sem.at[1])
            cp.start() if start else cp.wait()

        def per_class(c, carry):
            lo = fill_ref[3 * c]
            mid = fill_ref[3 * c + 1]
            hi = fill_ref[3 * c + 2]
            lax.fori_loop(lo, mid, lambda r, cc: (fill_copy(r, 1), cc)[1], 0)
            chunks = (hi - mid) // SUBLANES
            off = mid
            for bit in reversed(range((MOE_ROWS // SUBLANES).bit_length() - 1)):
                size = SUBLANES << bit
                has = (chunks & (1 << bit)) != 0

                @pl.when(has)
                def _(off=off, size=size):
                    fill_copy(pl.multiple_of(off, SUBLANES), size)

                off = off + jnp.where(has, size, 0)
            return carry

        lax.fori_loop(0, N_CLASSES, per_class, 0)
        n_blocks = dst_ref.shape[0] // MOE_ROWS
        lax.fori_loop(fill_ref[3 * N_CLASSES], n_blocks,
                      lambda b, cc: (fill_copy(pl.multiple_of(b * MOE_ROWS, MOE_ROWS), MOE_ROWS), cc)[1], 0)

    @pl.when(i == 0)
    def _():
        fill_padding(True)
        fill_padding(False)

    pltpu.make_async_copy(src_ref, dst_ref.at[pl.ds(0, n)], sem.at[0]).wait()


def _dispatch(src, dest, fill, n_rows):
    t, w = src.shape
    assert t % GATHER_ROWS == 0 and GATHER_ROWS >= MOE_ROWS
    nb = t // GATHER_ROWS
    grid_spec = pltpu.PrefetchScalarGridSpec(
        num_scalar_prefetch=1,
        grid=(nb,),
        in_specs=[
            pl.BlockSpec((1, 1, GATHER_ROWS), lambda i, f: (i, 0, 0), memory_space=pltpu.SMEM),
            pl.BlockSpec((GATHER_ROWS, w), lambda i, f: (i, 0)),
        ],
        out_specs=pl.BlockSpec(memory_space=pl.ANY),
        scratch_shapes=[pltpu.SemaphoreType.DMA((2,))],
    )
    return pl.pallas_call(
        _dispatch_body,
        grid_spec=grid_spec,
        out_shape=jax.ShapeDtypeStruct((n_rows, w), src.dtype),
        compiler_params=_params(("arbitrary",)),
        name="dispatch",
    )(fill, dest.reshape(nb, 1, GATHER_ROWS), src)


def _expert_body(lo_ref, hi_ref, nvalid_ref, xs_ref, gf_ref, wg_a, wu_a, wd_a, wg_b, wu_b, wd_b, ys_ref):
    i = pl.program_id(0)

    @pl.when(i < nvalid_ref[0])
    def _():
        xb = xs_ref[...]
        h = _rms(xb[:, 0:D_MODEL], gf_ref[...]).astype(BF16)

        def ffn(wg, wu, wd):
            hid = (jax.nn.silu(_dot(h, wg[...])) * _dot(h, wu[...])).astype(BF16)
            return _dot(hid, wd[...])

        g_lo = xb[:, D_MODEL:D_MODEL + 1]
        g_hi = xb[:, D_MODEL + 1:D_MODEL + 2]
        ys_ref[...] = g_lo * ffn(wg_a, wu_a, wd_a) + g_hi * ffn(wg_b, wu_b, wd_b)

    @pl.when(i >= nvalid_ref[0])
    def _():
        ys_ref[...] = jnp.zeros_like(ys_ref)


def _experts(xs, g_ffn, blk_lo, blk_hi, nvalid, w_g, w_u, w_d):
    n_rows = xs.shape[0]
    nb = n_rows // MOE_ROWS
    d, de = w_g.shape[1], w_g.shape[2]
    up = lambda sel: pl.BlockSpec((None, d, de), lambda i, lo, hi, nv: (sel(lo, hi)[i], 0, 0))
    down = lambda sel: pl.BlockSpec((None, de, d), lambda i, lo, hi, nv: (sel(lo, hi)[i], 0, 0))
    first = lambda lo, hi: lo
    second = lambda lo, hi: hi
    grid_spec = pltpu.PrefetchScalarGridSpec(
        num_scalar_prefetch=3,
        grid=(nb,),
        in_specs=[
            pl.BlockSpec((MOE_ROWS, ROW_W), lambda i, lo, hi, nv: (jnp.minimum(i, nv[0] - 1), 0)),
            pl.BlockSpec((1, d), lambda i, lo, hi, nv: (0, 0)),
            up(first), up(first), down(first), up(second), up(second), down(second),
        ],
        out_specs=pl.BlockSpec((MOE_ROWS, d), lambda i, lo, hi, nv: (i, 0)),
    )
    return pl.pallas_call(
        _expert_body,
        grid_spec=grid_spec,
        out_shape=jax.ShapeDtypeStruct((n_rows, d), F32),
        compiler_params=_params(("arbitrary",)),
        name="experts",
    )(blk_lo, blk_hi, nvalid, xs, g_ffn.reshape(1, d), w_g, w_u, w_d, w_g, w_u, w_d)


def _combine_body(cur_ref, nxt_ref, x2_ref, ys_ref, g_ref, o_ref, ybuf, sem):
    n = ROW_TILE
    i = pl.program_id(0)
    last = pl.num_programs(0) - 1
    slot = i % 2

    def issue_tile(idx_ref, s):
        def issue(r8, carry):
            for u in range(SUBLANES):
                r = r8 * SUBLANES + u
                pltpu.make_async_copy(ys_ref.at[pl.ds(idx_ref[0, 0, r], 1)], ybuf.at[s, pl.ds(r, 1)],
                                      sem.at[s]).start(priority=u % 2)
            return carry

        lax.fori_loop(0, n // SUBLANES, issue, 0)

    @pl.when(i == 0)
    def _():
        issue_tile(cur_ref, 0)

    @pl.when(i < last)
    def _():
        issue_tile(nxt_ref, 1 - slot)

    pltpu.make_async_copy(ys_ref.at[pl.ds(0, n)], ybuf.at[slot], sem.at[slot]).wait()
    o_ref[...] = _rms(x2_ref[...] + ybuf[slot], g_ref[...])


def _combine(x2e, ys, dest, g_final):
    t = x2e.shape[0]
    d = D_MODEL
    tm = ROW_TILE
    nt = t // tm
    idx = dest.reshape(nt, 1, tm)
    return pl.pallas_call(
        _combine_body,
        grid=(nt,),
        in_specs=[
            pl.BlockSpec((1, 1, tm), lambda i: (i, 0, 0), memory_space=pltpu.SMEM),
            pl.BlockSpec((1, 1, tm), lambda i: (jnp.minimum(i + 1, nt - 1), 0, 0), memory_space=pltpu.SMEM),
            pl.BlockSpec((tm, d), lambda i: (i, 0)),
            pl.BlockSpec(memory_space=pl.ANY),
            pl.BlockSpec((1, d), lambda i: (0, 0)),
        ],
        out_specs=pl.BlockSpec((tm, d), lambda i: (i, 0)),
        out_shape=jax.ShapeDtypeStruct((t, d), F32),
        scratch_shapes=[pltpu.VMEM((2, tm, d), F32), pltpu.SemaphoreType.DMA((2,))],
        compiler_params=_params(("arbitrary",)),
        name="combine",
    )(idx, idx, x2e, ys, g_final.reshape(1, d))


def _dispatch_plan(cls, t):
    onehot = (cls[:, None] == jnp.arange(N_CLASSES, dtype=jnp.int32)[None, :]).astype(jnp.int32)
    ranks = jnp.cumsum(onehot, axis=0) - onehot
    counts = jnp.sum(onehot, axis=0)
    padded = ((counts + MOE_ROWS - 1) // MOE_ROWS) * MOE_ROWS
    pad_end = jnp.cumsum(padded)
    pad_start = pad_end - padded
    dest = jnp.sum(onehot * (ranks + pad_start[None, :]), axis=1).astype(jnp.int32)
    fill_lo = pad_start + counts
    fill_mid = jnp.minimum(((fill_lo + SUBLANES - 1) // SUBLANES) * SUBLANES, pad_end)
    nvalid = (pad_end[-1] // MOE_ROWS).astype(jnp.int32).reshape(1)
    fill = jnp.concatenate([jnp.stack([fill_lo, fill_mid, pad_end], axis=1).reshape(-1), nvalid]).astype(jnp.int32)
    n_blocks = t // MOE_ROWS + N_CLASSES
    blk_start = jnp.arange(n_blocks, dtype=jnp.int32) * MOE_ROWS
    blk_cls = jnp.minimum(jnp.sum((blk_start[:, None] >= pad_end[None, :]).astype(jnp.int32), axis=1), N_CLASSES - 1)
    grp = blk_cls // N_PAIRS
    pair_onehot = ((blk_cls % N_PAIRS)[:, None] == jnp.arange(N_PAIRS, dtype=jnp.int32)[None, :]).astype(jnp.int32)
    blk_lo = grp * EXPERTS_PER_GROUP + jnp.sum(pair_onehot * jnp.asarray(PAIR_LO, jnp.int32)[None, :], axis=1)
    blk_hi = grp * EXPERTS_PER_GROUP + jnp.sum(pair_onehot * jnp.asarray(PAIR_HI, jnp.int32)[None, :], axis=1)
    return dest, fill, blk_lo.astype(jnp.int32), blk_hi.astype(jnp.int32), nvalid, n_blocks * MOE_ROWS


def _regroup_w_in(w_in):
    splits = np.cumsum([GLA_QK_W, GLA_QK_W, GLA_V_W, GLA_RANK, GLA_V_W, DIFF_QK_W, DIFF_QK_W, DIFF_V_W])[:-1]
    q_g, k_g, v_g, a_lr, r_g, q_d, k_d, v_d = jnp.split(w_in, [int(i) for i in splits], axis=1)
    a_pad = jnp.pad(a_lr, ((0, 0), (0, A_PAD - GLA_RANK)))
    q_d = q_d * (DIFF_DQK ** -0.5 * math.log2(math.e))
    return jnp.concatenate([q_g, k_g, v_g, r_g, a_pad, q_d, k_d], axis=1).astype(BF16), v_d.T.astype(BF16)


def kernel(x, mem, norm_mix_g, w_in, w_gla_a2, b_gla_a, gla_norm_g, diff_norm_g, lambda_q1, lambda_k1, lambda_q2,
           lambda_k2, w_out, norm_cross_g, norm_mem_g, w_cq, w_ckv, w_co, norm_ffn_g, w_router_grp, b_router_grp,
           w_router_exp, b_router_exp, w_e_gate, w_e_up, w_e_down, norm_final_g):
    b_, s_, d_ = x.shape
    m_ = mem.shape[1]
    t = b_ * s_
    xf = x.reshape(t, d_)

    kmem, vmem = _norm_matmul(mem.reshape(b_ * m_, d_), norm_mem_g[0], w_ckv[0].astype(BF16), (d_, d_), "mem_kv")

    w_cols, w_vt = _regroup_w_in(w_in[0])
    gla_in, dqk, v_t = _in_proj(xf, norm_mix_g[0], w_cols, w_vt)

    w_a2p = jnp.pad(w_gla_a2[0], ((0, A_PAD - GLA_RANK), (0, 0))).astype(BF16)
    o_g = _gla(gla_in, w_a2p, b_gla_a[0], gla_norm_g[0], b_, s_)

    lam_params = jnp.stack([lambda_q1[0], lambda_k1[0], lambda_q2[0], lambda_k2[0]]).astype(F32)
    o_d = _diff_attn(dqk, v_t, lam_params, diff_norm_g[0], b_, s_)

    pad_r = LANES - N_EXPERTS - N_GROUPS
    w_r = jnp.pad(jnp.concatenate([w_router_exp[0], w_router_grp[0]], axis=1), ((0, 0), (0, pad_r))).astype(BF16)
    b_r = jnp.pad(jnp.concatenate([b_router_exp[0], b_router_grp[0]]), (0, pad_r)).reshape(1, LANES)
    x2e = _post_mix(xf, o_g, o_d, w_out[0].astype(BF16), norm_cross_g[0], w_cq[0].astype(BF16), kmem, vmem,
                    w_co[0].astype(BF16), norm_ffn_g[0], w_r, b_r, s_, m_)

    cls = x2e[:, D_MODEL + 2].astype(jnp.int32)
    dest, fill, blk_lo, blk_hi, nvalid, n_rows = _dispatch_plan(cls, t)
    xs = _dispatch(x2e, dest, fill, n_rows)
    ys = _experts(xs, norm_ffn_g[0], blk_lo, blk_hi, nvalid, w_e_gate[0].astype(BF16), w_e_up[0].astype(BF16),
                  w_e_down[0].astype(BF16))

    out = _combine(x2e, ys, dest, norm_final_g)
    return out.reshape(b_, s_, d_)
```

```python
import functools
import math

import jax
import jax.numpy as jnp
import numpy as np
from jax import lax
from jax.experimental import pallas as pl
from jax.experimental.pallas import tpu as pltpu

F32 = jnp.float32
BF16 = jnp.bfloat16
U32 = jnp.uint32

EPS = 1e-6
D_MODEL = 1024
GLA_HEADS = 4
GLA_DK = 64
GLA_DV = 128
GLA_RANK = 16
GLA_TAU = 16.0
GLA_CHUNK = 64
GLA_QK_W = GLA_HEADS * GLA_DK
GLA_V_W = GLA_HEADS * GLA_DV
DIFF_HEADS = 4
DIFF_DQK = 64
DIFF_DV = 128
DIFF_QK_W = DIFF_HEADS * 2 * DIFF_DQK
DIFF_V_W = DIFF_HEADS * DIFF_DV
CROSS_HEADS = 4
CROSS_DH = D_MODEL // CROSS_HEADS
N_GROUPS = 4
EXPERTS_PER_GROUP = 4
N_EXPERTS = N_GROUPS * EXPERTS_PER_GROUP
TOP_K = 2
D_EXPERT = 512
LAMBDA_INIT = 0.8 - 0.6 * math.exp(-0.3 * 0)

LANES = 128
SUBLANES = 8
A_PAD = LANES
GLA_IN_W = 2 * GLA_QK_W + 2 * GLA_V_W + A_PAD
ROW_W = D_MODEL + LANES
PAIR_LO = (0, 0, 0, 1, 1, 2)
PAIR_HI = (1, 2, 3, 2, 3, 3)
N_PAIRS = len(PAIR_LO)
N_CLASSES = N_GROUPS * N_PAIRS

ROW_TILE = 512
GLA_ROWS = 512
ATT_TILE = 512
ATT_ONES = 16
MOE_ROWS = 512
GATHER_ROWS = 1024
VMEM_LIMIT = 56 * 1024 * 1024


def _params(sem, vmem=VMEM_LIMIT, flags=None):
    return pltpu.CompilerParams(dimension_semantics=sem, vmem_limit_bytes=vmem, flags=flags)


def _rms(x, g):
    ms = jnp.mean(x * x, axis=-1, keepdims=True)
    return x * lax.rsqrt(ms + EPS) * g


def _dot(a, b):
    return jnp.dot(a, b, preferred_element_type=F32)


def _dot_nt(a, b):
    return lax.dot_general(a, b, (((1,), (1,)), ((), ())), preferred_element_type=F32)


def _norm_matmul_body(x_ref, g_ref, w_ref, *o_refs):
    h = _rms(x_ref[...], g_ref[...]).astype(BF16)
    off = 0
    for o_ref in o_refs:
        n = o_ref.shape[-1]
        o_ref[...] = _dot(h, w_ref[:, off:off + n]).astype(o_ref.dtype)
        off += n


def _norm_matmul(x, g, w, splits, name):
    t, d = x.shape
    n = w.shape[1]
    assert sum(splits) == n and t % ROW_TILE == 0
    return pl.pallas_call(
        _norm_matmul_body,
        grid=(t // ROW_TILE,),
        in_specs=[
            pl.BlockSpec((ROW_TILE, d), lambda i: (i, 0)),
            pl.BlockSpec((1, d), lambda i: (0, 0)),
            pl.BlockSpec((d, n), lambda i: (0, 0)),
        ],
        out_specs=[pl.BlockSpec((ROW_TILE, s), lambda i: (i, 0)) for s in splits],
        out_shape=[jax.ShapeDtypeStruct((t, s), BF16) for s in splits],
        compiler_params=_params(("parallel",)),
        name=name,
    )(x, g.reshape(1, d), w)


def _in_proj_body(x_ref, g_ref, w_ref, wvt_ref, gla_ref, dqk_ref, vt_ref):
    h = _rms(x_ref[...], g_ref[...]).astype(BF16)
    gla_ref[...] = _dot(h, w_ref[:, 0:GLA_IN_W]).astype(gla_ref.dtype)
    dqk_ref[...] = _dot(h, w_ref[:, GLA_IN_W:]).astype(dqk_ref.dtype)
    vt_ref[...] = _dot_nt(wvt_ref[...], h).astype(vt_ref.dtype)


def _in_proj(x, g, w, w_vt):
    t, d = x.shape
    n = w.shape[1]
    tm = ROW_TILE
    assert n == GLA_IN_W + 2 * DIFF_QK_W and t % tm == 0
    return pl.pallas_call(
        _in_proj_body,
        grid=(t // tm,),
        in_specs=[
            pl.BlockSpec((tm, d), lambda i: (i, 0)),
            pl.BlockSpec((1, d), lambda i: (0, 0)),
            pl.BlockSpec((d, n), lambda i: (0, 0)),
            pl.BlockSpec((DIFF_V_W, d), lambda i: (0, 0)),
        ],
        out_specs=[
            pl.BlockSpec((tm, GLA_IN_W), lambda i: (i, 0)),
            pl.BlockSpec((tm, 2 * DIFF_QK_W), lambda i: (i, 0)),
            pl.BlockSpec((DIFF_V_W, tm), lambda i: (0, i)),
        ],
        out_shape=[
            jax.ShapeDtypeStruct((t, GLA_IN_W), BF16),
            jax.ShapeDtypeStruct((t, 2 * DIFF_QK_W), BF16),
            jax.ShapeDtypeStruct((DIFF_V_W, t), BF16),
        ],
        compiler_params=_params(("parallel",)),
        name="in_proj",
    )(x, g.reshape(1, d), w, w_vt)


def _chunk_prefix_operator():
    i = np.arange(GLA_ROWS)
    same_chunk = (i[:, None] // GLA_CHUNK) == (i[None, :] // GLA_CHUNK)
    return jnp.asarray(same_chunk & (i[None, :] <= i[:, None]), dtype=BF16)


def _gla_body(in_ref, tri_ref, wa_ref, ba_ref, gn_ref, o_ref, s_ref):
    c_ = GLA_CHUNK
    n_c = GLA_ROWS // c_

    @pl.when(pl.program_id(1) == 0)
    def _():
        s_ref[...] = jnp.zeros_like(s_ref)

    q0, k0, v0, r0, a0 = 0, GLA_QK_W, 2 * GLA_QK_W, 2 * GLA_QK_W + GLA_V_W, 2 * GLA_QK_W + 2 * GLA_V_W
    z = _dot(in_ref[:, a0:a0 + A_PAD], wa_ref[...]) + ba_ref[...]
    log_a = jax.nn.log_sigmoid(z) / GLA_TAU
    la_hi = log_a.astype(BF16)
    la_lo = (log_a - la_hi.astype(F32)).astype(BF16)
    cum = _dot(tri_ref[...], la_hi) + _dot(tri_ref[...], la_lo)
    q = in_ref[:, q0:q0 + GLA_QK_W].astype(F32)
    k = in_ref[:, k0:k0 + GLA_QK_W].astype(F32)
    q_e = (q * (GLA_DK ** -0.5) * jnp.exp(cum)).astype(BF16)
    k_e = (k * jnp.exp(-cum)).astype(BF16)
    tots = [cum[(c + 1) * c_ - 1:(c + 1) * c_, :] for c in range(n_c)]
    k_s = jnp.concatenate([(k[c * c_:(c + 1) * c_] * jnp.exp(tots[c] - cum[c * c_:(c + 1) * c_])).astype(BF16)
                           for c in range(n_c)], axis=0)

    causal = (lax.broadcasted_iota(jnp.int32, (c_, c_), 0) >= lax.broadcasted_iota(jnp.int32, (c_, c_), 1))
    first_half = {n: lax.broadcasted_iota(jnp.int32, (n, LANES), 1) < GLA_DK for n in (GLA_ROWS, GLA_DV)}
    gn = gn_ref[...]

    for h in range(GLA_HEADS):
        tile = slice((h // 2) * LANES, (h // 2 + 1) * LANES)
        vs_ = slice(h * GLA_DV, (h + 1) * GLA_DV)
        head_lanes = first_half[GLA_ROWS] if h % 2 == 0 else ~first_half[GLA_ROWS]
        qe_m = jnp.where(head_lanes, q_e[:, tile], jnp.zeros((), BF16))
        ke_t = k_e[:, tile]
        ks_t = k_s[:, tile]
        v_h = in_ref[:, v0 + h * GLA_DV:v0 + (h + 1) * GLA_DV]
        vt_h = v_h.astype(F32).T.astype(BF16)
        o_intra, kv_t = [], []
        for c in range(n_c):
            rows = slice(c * c_, (c + 1) * c_)
            att = jnp.where(causal, _dot_nt(qe_m[rows], ke_t[rows]), 0.0).astype(BF16)
            o_intra.append(_dot(att, v_h[rows]))
            pair = slice((c // 2) * LANES, (c // 2 + 1) * LANES)
            chunk_lanes = first_half[GLA_DV] if c % 2 == 0 else ~first_half[GLA_DV]
            kv_t.append(_dot(jnp.where(chunk_lanes, vt_h[:, pair], jnp.zeros((), BF16)), ks_t[pair]))
        st = s_ref[h]
        s_prev = []
        for c in range(n_c):
            s_prev.append(st.astype(BF16))
            st = st * jnp.exp(tots[c][:, tile]) + kv_t[c]
        s_ref[h] = st
        r_h = in_ref[:, r0 + h * GLA_DV:r0 + (h + 1) * GLA_DV].astype(F32)
        for c in range(n_c):
            rows = slice(c * c_, (c + 1) * c_)
            o_h = o_intra[c] + _dot_nt(qe_m[rows], s_prev[c])
            o_ref[rows, vs_] = (_rms(o_h, gn) * jax.nn.silu(r_h[rows])).astype(o_ref.dtype)


def _gla(gla_in, w_a2p, b_a, gn, batch, seq):
    t = gla_in.shape[0]
    assert seq % GLA_ROWS == 0 and GLA_ROWS % (2 * GLA_CHUNK) == 0 and 2 * GLA_DK == LANES and 2 * GLA_CHUNK == LANES
    nsb = seq // GLA_ROWS
    return pl.pallas_call(
        _gla_body,
        grid=(batch, nsb),
        in_specs=[
            pl.BlockSpec((GLA_ROWS, GLA_IN_W), lambda b, j: (b * nsb + j, 0)),
            pl.BlockSpec((GLA_ROWS, GLA_ROWS), lambda b, j: (0, 0)),
            pl.BlockSpec((A_PAD, GLA_QK_W), lambda b, j: (0, 0)),
            pl.BlockSpec((1, GLA_QK_W), lambda b, j: (0, 0)),
            pl.BlockSpec((1, GLA_DV), lambda b, j: (0, 0)),
        ],
        out_specs=pl.BlockSpec((GLA_ROWS, GLA_V_W), lambda b, j: (b * nsb + j, 0)),
        out_shape=jax.ShapeDtypeStruct((t, GLA_V_W), BF16),
        scratch_shapes=[pltpu.VMEM((GLA_HEADS, GLA_DV, LANES), F32)],
        compiler_params=_params(("parallel", "arbitrary")),
        name="gla",
    )(gla_in, _chunk_prefix_operator(), w_a2p, b_a.reshape(1, -1), gn.reshape(1, -1))


def _alibi_features(seq):
    j = np.arange(seq) % ATT_TILE
    j_lo = j % 256
    j_hi = j - j_lo
    kf = np.zeros((seq, 2 * DIFF_DQK), np.float32)
    for base in (0, DIFF_DQK):
        for f in range(3):
            kf[:, base + f] = j_lo
            kf[:, base + 3 + f] = j_hi
    slopes = 2.0 ** (-8.0 * np.arange(1, DIFF_HEADS + 1) / DIFF_HEADS)
    c = jnp.asarray(slopes * math.log2(math.e), dtype=F32)
    c1 = c.astype(BF16)
    c2 = (c - c1.astype(F32)).astype(BF16)
    c3 = (c - c1.astype(F32) - c2.astype(F32)).astype(BF16)
    terms = jnp.stack([c1, c2, c3, c1, c2, c3], axis=1)
    qf = jnp.zeros((DIFF_HEADS, 2 * DIFF_DQK), BF16)
    qf = qf.at[:, 0:6].set(terms).at[:, DIFF_DQK:DIFF_DQK + 6].set(terms)
    qf = jnp.repeat(qf, 8, axis=0)
    return jnp.asarray(kf, dtype=BF16), qf, c


def _diff_body(c_ref, q_ref, k_ref, vt_ref, qf_ref, kf_ref, lam_ref, gn_ref, o_ref, kaug_ref, vta_ref):
    tq = ATT_TILE
    h = pl.program_id(1)
    seq = k_ref.shape[0]
    nq = seq // tq
    c = c_ref[h]

    k = k_ref[...]
    kf = kf_ref[...]
    lane_k = lax.broadcasted_iota(jnp.int32, k.shape, 1)
    kaug_ref[0] = jnp.where(lane_k < DIFF_DQK, k, kf)
    kaug_ref[1] = jnp.where(lane_k >= DIFF_DQK, k, kf)
    for jj in range(nq):
        vta_ref[jj, 0:DIFF_DV, :] = vt_ref[:, jj * tq:(jj + 1) * tq]
        vta_ref[jj, DIFF_DV:, :] = jnp.ones((ATT_ONES, tq), BF16)

    lq1, lk1, lq2, lk2 = (lam_ref[i:i + 1, :] for i in range(4))
    lam = (jnp.exp(jnp.sum(lq1 * lk1, axis=-1, keepdims=True))
           - jnp.exp(jnp.sum(lq2 * lk2, axis=-1, keepdims=True)) + LAMBDA_INIT)
    row = lax.broadcasted_iota(jnp.int32, (tq, tq), 0)
    col = lax.broadcasted_iota(jnp.int32, (tq, tq), 1)
    visible = row <= col
    lane = lax.broadcasted_iota(jnp.int32, (tq, 2 * DIFF_DQK), 1)
    qf = jnp.broadcast_to(qf_ref[0:1, :], (tq, 2 * DIFF_DQK))

    for qi in range(nq):
        q = q_ref[qi * tq:(qi + 1) * tq, :]
        q_maps = (jnp.where(lane < DIFF_DQK, q, qf), jnp.where(lane >= DIFF_DQK, q, qf))
        m_run = [None, None]
        acc = [None, None]
        for j in range(qi + 1):
            vblk = vta_ref[j]
            base = c * float(j * tq)
            for m in range(2):
                s_t = _dot_nt(kaug_ref[m, j * tq:(j + 1) * tq, :], q_maps[m])
                if j == qi:
                    s_t = jnp.where(visible, s_t, -jnp.inf)
                blk_max = jnp.max(s_t, axis=0, keepdims=True) + base
                if j == 0:
                    m_new = blk_max
                    acc[m] = _dot(vblk, jnp.exp2(s_t - (m_new - base)).astype(BF16))
                else:
                    m_new = jnp.maximum(m_run[m], blk_max)
                    alpha = jnp.exp2(m_run[m] - m_new)
                    acc[m] = alpha * acc[m] + _dot(vblk, jnp.exp2(s_t - (m_new - base)).astype(BF16))
                m_run[m] = m_new
        o_t = (acc[0][0:DIFF_DV] / acc[0][DIFF_DV:DIFF_DV + 1]
               - lam * (acc[1][0:DIFF_DV] / acc[1][DIFF_DV:DIFF_DV + 1]))
        o = o_t.T
        o_ref[qi * tq:(qi + 1) * tq, :] = (_rms(o, gn_ref[...]) * (1.0 - LAMBDA_INIT)).astype(o_ref.dtype)


def _diff_attn(dqk, v_t, lam_params, gn, batch, seq):
    t = dqk.shape[0]
    tq = ATT_TILE
    assert seq % tq == 0
    nq = seq // tq
    kfeat, qfeat, c = _alibi_features(seq)
    rows = DIFF_DV + ATT_ONES
    grid_spec = pltpu.PrefetchScalarGridSpec(
        num_scalar_prefetch=1,
        grid=(batch, DIFF_HEADS),
        in_specs=[
            pl.BlockSpec((seq, 2 * DIFF_DQK), lambda b, h, s: (b, h)),
            pl.BlockSpec((seq, 2 * DIFF_DQK), lambda b, h, s: (b, DIFF_HEADS + h)),
            pl.BlockSpec((DIFF_DV, seq), lambda b, h, s: (h, b)),
            pl.BlockSpec((8, 2 * DIFF_DQK), lambda b, h, s: (h, 0)),
            pl.BlockSpec((seq, 2 * DIFF_DQK), lambda b, h, s: (0, 0)),
            pl.BlockSpec((4, DIFF_DQK), lambda b, h, s: (0, 0)),
            pl.BlockSpec((1, DIFF_DV), lambda b, h, s: (0, 0)),
        ],
        out_specs=pl.BlockSpec((seq, DIFF_DV), lambda b, h, s: (b, h)),
        scratch_shapes=[
            pltpu.VMEM((2, seq, 2 * DIFF_DQK), BF16),
            pltpu.VMEM((nq, rows, tq), BF16),
        ],
    )
    return pl.pallas_call(
        _diff_body,
        grid_spec=grid_spec,
        out_shape=jax.ShapeDtypeStruct((t, DIFF_V_W), BF16),
        compiler_params=_params(("parallel", "parallel")),
        name="diff_attn",
    )(c, dqk, dqk, v_t, qfeat, kfeat, lam_params, gn.reshape(1, -1))


def _route(logits):
    lane = lax.broadcasted_iota(jnp.int32, logits.shape, 1)
    big = jnp.int32(LANES)
    neg = -jnp.inf

    def first_argmax(vals, vmax):
        return jnp.min(jnp.where(vals == vmax, lane, big), axis=-1, keepdims=True)

    is_grp = (lane >= N_EXPERTS) & (lane < N_EXPERTS + N_GROUPS)
    lg = jnp.where(is_grp, logits, neg)
    mg = jnp.max(lg, axis=-1, keepdims=True)
    p_g = 1.0 / jnp.sum(jnp.exp(lg - mg), axis=-1, keepdims=True)
    g_sel = first_argmax(lg, mg) - N_EXPERTS
    in_grp = (lane >= g_sel * EXPERTS_PER_GROUP) & (lane < (g_sel + 1) * EXPERTS_PER_GROUP)
    le = jnp.where(in_grp, logits, neg)
    m1 = jnp.max(le, axis=-1, keepdims=True)
    i1 = first_argmax(le, m1)
    le2 = jnp.where(lane == i1, neg, le)
    m2 = jnp.max(le2, axis=-1, keepdims=True)
    i2 = first_argmax(le2, m2)
    den = jnp.sum(jnp.exp(le - m1), axis=-1, keepdims=True)
    w1 = 1.0 / den
    w2 = jnp.exp(m2 - m1) / den
    wsum = w1 + w2
    g1 = p_g * w1 / wsum
    g2 = p_g * w2 / wsum
    first_is_lo = i1 < i2
    g_lo = jnp.where(first_is_lo, g1, g2)
    g_hi = jnp.where(first_is_lo, g2, g1)
    a = jnp.minimum(i1, i2) - g_sel * EXPERTS_PER_GROUP
    b = jnp.maximum(i1, i2) - g_sel * EXPERTS_PER_GROUP
    pair = 3 * a - jnp.where(a == 2, 1, 0) + (b - a - 1)
    cls = (g_sel * N_PAIRS + pair).astype(F32)
    out = jnp.where(lane == 0, g_lo, jnp.where(lane == 1, g_hi, cls))
    return jnp.where(lane < 3, out, 0.0)


def _post_mix_body(x_ref, og_ref, od_ref, wo_ref, gc_ref, wq_ref, km_ref, vm_ref, wco_ref, gf_ref, wr_ref, br_ref,
                   x2e_ref):
    x1 = x_ref[...] + _dot(og_ref[...], wo_ref[0:GLA_V_W, :]) + _dot(od_ref[...], wo_ref[GLA_V_W:, :])
    h2 = _rms(x1, gc_ref[...]).astype(BF16)
    qc = _dot(h2, wq_ref[...]).astype(BF16)
    heads = []
    for h in range(CROSS_HEADS):
        sl = slice(h * CROSS_DH, (h + 1) * CROSS_DH)
        s = _dot_nt(qc[:, sl], km_ref[:, sl]) * (CROSS_DH ** -0.5)
        s = s - jnp.max(s, axis=-1, keepdims=True)
        p = jnp.exp(s)
        p = p / jnp.sum(p, axis=-1, keepdims=True)
        heads.append(_dot(p.astype(BF16), vm_ref[:, sl]).astype(BF16))
    oc = jnp.concatenate(heads, axis=-1)
    x2 = x1 + _dot(oc, wco_ref[...])
    h3 = _rms(x2, gf_ref[...]).astype(BF16)
    x2e_ref[:, 0:D_MODEL] = x2
    x2e_ref[:, D_MODEL:] = _route(_dot(h3, wr_ref[...]) + br_ref[...])


def _post_mix(x, og, od, w_out, g_cross, w_cq, kmem, vmem, w_co, g_ffn, w_r, b_r, seq, mem_len):
    t, d = x.shape
    tm = ROW_TILE
    assert seq % tm == 0
    per_b = seq // tm
    full = lambda shape: pl.BlockSpec(shape, lambda i: (0, 0))
    rows = lambda w: pl.BlockSpec((tm, w), lambda i: (i, 0))
    return pl.pallas_call(
        _post_mix_body,
        grid=(t // tm,),
        in_specs=[
            rows(d), rows(GLA_V_W), rows(DIFF_V_W),
            full((d, d)), full((1, d)), full((d, d)),
            pl.BlockSpec((mem_len, d), lambda i: (i // per_b, 0)),
            pl.BlockSpec((mem_len, d), lambda i: (i // per_b, 0)),
            full((d, d)), full((1, d)), full((d, LANES)), full((1, LANES)),
        ],
        out_specs=rows(ROW_W),
        out_shape=jax.ShapeDtypeStruct((t, ROW_W), F32),
        compiler_params=_params(("parallel",)),
        name="post_mix",
    )(x, og, od, w_out, g_cross.reshape(1, d), w_cq, kmem, vmem, w_co, g_ffn.reshape(1, d), w_r, b_r)


def _dispatch_body(fill_ref, dest_ref, src_ref, dst_ref, sem):
    n8 = GATHER_ROWS // SUBLANES
    i = pl.program_id(0)

    def dst_row(d):
        return dst_ref.at[lax.shift_right_logical(d, 3), pl.ds(d & (SUBLANES - 1), 1)]

    def issue(r8, carry):
        for u in range(SUBLANES):
            d = dest_ref[0, 0, r8 * SUBLANES + u]
            pltpu.make_async_copy(src_ref.at[r8, pl.ds(u, 1)], dst_row(d), sem.at[0]).start(priority=u % 2)
        return carry

    lax.fori_loop(0, n8, issue, 0)

    def fill_padding(start):
        def go(cp):
            cp.start() if start else cp.wait()

        def fill_tiles(t0, nt):
            go(pltpu.make_async_copy(src_ref.at[pl.ds(0, nt)], dst_ref.at[pl.ds(t0, nt)], sem.at[1]))

        def per_class(c, carry):
            lo = fill_ref[3 * c]
            mid = fill_ref[3 * c + 1]
            hi = fill_ref[3 * c + 2]
            lax.fori_loop(lo, mid, lambda r, cc: (go(pltpu.make_async_copy(src_ref.at[0, pl.ds(0, 1)], dst_row(r),
                                                                             sem.at[1])), cc)[1], 0)
            tiles = lax.shift_right_logical(hi - mid, 3)
            off = lax.shift_right_logical(mid, 3)
            for bit in reversed(range((MOE_ROWS // SUBLANES).bit_length() - 1)):
                has = (tiles & (1 << bit)) != 0

                @pl.when(has)
                def _(off=off, bit=bit):
                    fill_tiles(off, 1 << bit)

                off = off + jnp.where(has, 1 << bit, 0)
            return carry

        lax.fori_loop(0, N_CLASSES, per_class, 0)
        blk8 = MOE_ROWS // SUBLANES
        n_blocks = dst_ref.shape[0] // blk8
        lax.fori_loop(fill_ref[3 * N_CLASSES], n_blocks, lambda b, cc: (fill_tiles(b * blk8, blk8), cc)[1], 0)

    @pl.when(i == 0)
    def _():
        fill_padding(True)
        fill_padding(False)

    pltpu.make_async_copy(src_ref, dst_ref.at[pl.ds(0, n8)], sem.at[0]).wait()


def _dispatch(src, dest, fill, n_rows):
    t, w = src.shape
    assert t % GATHER_ROWS == 0 and GATHER_ROWS >= MOE_ROWS and SUBLANES == 8
    nb = t // GATHER_ROWS
    n8 = GATHER_ROWS // SUBLANES
    grid_spec = pltpu.PrefetchScalarGridSpec(
        num_scalar_prefetch=1,
        grid=(nb,),
        in_specs=[
            pl.BlockSpec((1, 1, GATHER_ROWS), lambda i, f: (i, 0, 0), memory_space=pltpu.SMEM),
            pl.BlockSpec((n8, SUBLANES, w), lambda i, f: (i, 0, 0)),
        ],
        out_specs=pl.BlockSpec(memory_space=pl.ANY),
        scratch_shapes=[pltpu.SemaphoreType.DMA((2,))],
    )
    out = pl.pallas_call(
        _dispatch_body,
        grid_spec=grid_spec,
        out_shape=jax.ShapeDtypeStruct((n_rows // SUBLANES, SUBLANES, w), src.dtype),
        compiler_params=_params(("arbitrary",)),
        name="dispatch",
    )(fill, dest.reshape(nb, 1, GATHER_ROWS), src.reshape(t // SUBLANES, SUBLANES, w))
    return out.reshape(n_rows, w)


def _expert_body(lo_ref, hi_ref, nvalid_ref, xs_ref, gf_ref, wg_a, wu_a, wd_a, wg_b, wu_b, wd_b, ys_ref):
    i = pl.program_id(0)

    @pl.when(i < nvalid_ref[0])
    def _():
        xb = xs_ref[...]
        h = _rms(xb[:, 0:D_MODEL], gf_ref[...]).astype(BF16)

        def ffn(wg, wu, wd):
            hid = (jax.nn.silu(_dot(h, wg[...])) * _dot(h, wu[...])).astype(BF16)
            return _dot(hid, wd[...])

        g_lo = xb[:, D_MODEL:D_MODEL + 1]
        g_hi = xb[:, D_MODEL + 1:D_MODEL + 2]
        ys_ref[...] = g_lo * ffn(wg_a, wu_a, wd_a) + g_hi * ffn(wg_b, wu_b, wd_b)

    @pl.when(i >= nvalid_ref[0])
    def _():
        ys_ref[...] = jnp.zeros_like(ys_ref)


def _experts(xs, g_ffn, blk_lo, blk_hi, nvalid, w_g, w_u, w_d):
    n_rows = xs.shape[0]
    nb = n_rows // MOE_ROWS
    d, de = w_g.shape[1], w_g.shape[2]
    up = lambda sel: pl.BlockSpec((None, d, de), lambda i, lo, hi, nv: (sel(lo, hi)[i], 0, 0))
    down = lambda sel: pl.BlockSpec((None, de, d), lambda i, lo, hi, nv: (sel(lo, hi)[i], 0, 0))
    first = lambda lo, hi: lo
    second = lambda lo, hi: hi
    grid_spec = pltpu.PrefetchScalarGridSpec(
        num_scalar_prefetch=3,
        grid=(nb,),
        in_specs=[
            pl.BlockSpec((MOE_ROWS, ROW_W), lambda i, lo, hi, nv: (jnp.minimum(i, nv[0] - 1), 0)),
            pl.BlockSpec((1, d), lambda i, lo, hi, nv: (0, 0)),
            up(first), up(first), down(first), up(second), up(second), down(second),
        ],
        out_specs=pl.BlockSpec((MOE_ROWS, d), lambda i, lo, hi, nv: (i, 0)),
    )
    return pl.pallas_call(
        _expert_body,
        grid_spec=grid_spec,
        out_shape=jax.ShapeDtypeStruct((n_rows, d), F32),
        compiler_params=_params(("arbitrary",)),
        name="experts",
    )(blk_lo, blk_hi, nvalid, xs, g_ffn.reshape(1, d), w_g, w_u, w_d, w_g, w_u, w_d)


def _combine_body(cur_ref, nxt_ref, x2_ref, ys_ref, g_ref, o_ref, ybuf, sem):
    n8 = ROW_TILE // SUBLANES
    i = pl.program_id(0)
    last = pl.num_programs(0) - 1
    slot = i % 2

    def issue_tile(idx_ref, s):
        def issue(r8, carry):
            for u in range(SUBLANES):
                d = idx_ref[0, 0, r8 * SUBLANES + u]
                src = ys_ref.at[lax.shift_right_logical(d, 3), pl.ds(d & (SUBLANES - 1), 1)]
                pltpu.make_async_copy(src, ybuf.at[s, r8, pl.ds(u, 1)], sem.at[s]).start(priority=u % 2)
            return carry

        lax.fori_loop(0, n8, issue, 0)

    @pl.when(i == 0)
    def _():
        issue_tile(cur_ref, 0)

    @pl.when(i < last)
    def _():
        issue_tile(nxt_ref, 1 - slot)

    pltpu.make_async_copy(ys_ref.at[pl.ds(0, n8)], ybuf.at[slot], sem.at[slot]).wait()
    y = ybuf[slot].reshape(ROW_TILE, D_MODEL)
    o_ref[...] = _rms(x2_ref[...] + y, g_ref[...])


def _combine(x2e, ys, dest, g_final):
    t = x2e.shape[0]
    d = D_MODEL
    tm = ROW_TILE
    nt = t // tm
    idx = dest.reshape(nt, 1, tm)
    return pl.pallas_call(
        _combine_body,
        grid=(nt,),
        in_specs=[
            pl.BlockSpec((1, 1, tm), lambda i: (i, 0, 0), memory_space=pltpu.SMEM),
            pl.BlockSpec((1, 1, tm), lambda i: (jnp.minimum(i + 1, nt - 1), 0, 0), memory_space=pltpu.SMEM),
            pl.BlockSpec((tm, d), lambda i: (i, 0)),
            pl.BlockSpec(memory_space=pl.ANY),
            pl.BlockSpec((1, d), lambda i: (0, 0)),
        ],
        out_specs=pl.BlockSpec((tm, d), lambda i: (i, 0)),
        out_shape=jax.ShapeDtypeStruct((t, d), F32),
        scratch_shapes=[pltpu.VMEM((2, tm // SUBLANES, SUBLANES, d), F32), pltpu.SemaphoreType.DMA((2,))],
        compiler_params=_params(("arbitrary",)),
        name="combine",
    )(idx, idx, x2e, ys.reshape(ys.shape[0] // SUBLANES, SUBLANES, d), g_final.reshape(1, d))


def _dispatch_plan(cls, t):
    onehot = (cls[:, None] == jnp.arange(N_CLASSES, dtype=jnp.int32)[None, :]).astype(jnp.int32)
    ranks = jnp.cumsum(onehot, axis=0) - onehot
    counts = jnp.sum(onehot, axis=0)
    padded = ((counts + MOE_ROWS - 1) // MOE_ROWS) * MOE_ROWS
    pad_end = jnp.cumsum(padded)
    pad_start = pad_end - padded
    dest = jnp.sum(onehot * (ranks + pad_start[None, :]), axis=1).astype(jnp.int32)
    fill_lo = pad_start + counts
    fill_mid = jnp.minimum(((fill_lo + SUBLANES - 1) // SUBLANES) * SUBLANES, pad_end)
    nvalid = (pad_end[-1] // MOE_ROWS).astype(jnp.int32).reshape(1)
    fill = jnp.concatenate([jnp.stack([fill_lo, fill_mid, pad_end], axis=1).reshape(-1), nvalid]).astype(jnp.int32)
    n_blocks = t // MOE_ROWS + N_CLASSES
    blk_start = jnp.arange(n_blocks, dtype=jnp.int32) * MOE_ROWS
    blk_cls = jnp.minimum(jnp.sum((blk_start[:, None] >= pad_end[None, :]).astype(jnp.int32), axis=1), N_CLASSES - 1)
    grp = blk_cls // N_PAIRS
    pair_onehot = ((blk_cls % N_PAIRS)[:, None] == jnp.arange(N_PAIRS, dtype=jnp.int32)[None, :]).astype(jnp.int32)
    blk_lo = grp * EXPERTS_PER_GROUP + jnp.sum(pair_onehot * jnp.asarray(PAIR_LO, jnp.int32)[None, :], axis=1)
    blk_hi = grp * EXPERTS_PER_GROUP + jnp.sum(pair_onehot * jnp.asarray(PAIR_HI, jnp.int32)[None, :], axis=1)
    return dest, fill, blk_lo.astype(jnp.int32), blk_hi.astype(jnp.int32), nvalid, n_blocks * MOE_ROWS


def _regroup_w_in(w_in):
    splits = np.cumsum([GLA_QK_W, GLA_QK_W, GLA_V_W, GLA_RANK, GLA_V_W, DIFF_QK_W, DIFF_QK_W, DIFF_V_W])[:-1]
    q_g, k_g, v_g, a_lr, r_g, q_d, k_d, v_d = jnp.split(w_in, [int(i) for i in splits], axis=1)
    a_pad = jnp.pad(a_lr, ((0, 0), (0, A_PAD - GLA_RANK)))
    q_d = q_d * (DIFF_DQK ** -0.5 * math.log2(math.e))
    return jnp.concatenate([q_g, k_g, v_g, r_g, a_pad, q_d, k_d], axis=1).astype(BF16), v_d.T.astype(BF16)


def kernel(x, mem, norm_mix_g, w_in, w_gla_a2, b_gla_a, gla_norm_g, diff_norm_g, lambda_q1, lambda_k1, lambda_q2,
           lambda_k2, w_out, norm_cross_g, norm_mem_g, w_cq, w_ckv, w_co, norm_ffn_g, w_router_grp, b_router_grp,
           w_router_exp, b_router_exp, w_e_gate, w_e_up, w_e_down, norm_final_g):
    b_, s_, d_ = x.shape
    m_ = mem.shape[1]
    t = b_ * s_
    xf = x.reshape(t, d_)

    kmem, vmem = _norm_matmul(mem.reshape(b_ * m_, d_), norm_mem_g[0], w_ckv[0].astype(BF16), (d_, d_), "mem_kv")

    w_cols, w_vt = _regroup_w_in(w_in[0])
    gla_in, dqk, v_t = _in_proj(xf, norm_mix_g[0], w_cols, w_vt)

    w_a2p = jnp.pad(w_gla_a2[0], ((0, A_PAD - GLA_RANK), (0, 0))).astype(BF16)
    o_g = _gla(gla_in, w_a2p, b_gla_a[0], gla_norm_g[0], b_, s_)

    lam_params = jnp.stack([lambda_q1[0], lambda_k1[0], lambda_q2[0], lambda_k2[0]]).astype(F32)
    o_d = _diff_attn(dqk, v_t, lam_params, diff_norm_g[0], b_, s_)

    pad_r = LANES - N_EXPERTS - N_GROUPS
    w_r = jnp.pad(jnp.concatenate([w_router_exp[0], w_router_grp[0]], axis=1), ((0, 0), (0, pad_r))).astype(BF16)
    b_r = jnp.pad(jnp.concatenate([b_router_exp[0], b_router_grp[0]]), (0, pad_r)).reshape(1, LANES)
    x2e = _post_mix(xf, o_g, o_d, w_out[0].astype(BF16), norm_cross_g[0], w_cq[0].astype(BF16), kmem, vmem,
                    w_co[0].astype(BF16), norm_ffn_g[0], w_r, b_r, s_, m_)

    cls = x2e[:, D_MODEL + 2].astype(jnp.int32)
    dest, fill, blk_lo, blk_hi, nvalid, n_rows = _dispatch_plan(cls, t)
    xs = _dispatch(x2e, dest, fill, n_rows)
    ys = _experts(xs, norm_ffn_g[0], blk_lo, blk_hi, nvalid, w_e_gate[0].astype(BF16), w_e_up[0].astype(BF16),
                  w_e_down[0].astype(BF16))

    out = _combine(x2e, ys, dest, norm_final_g)
    return out.reshape(b_, s_, d_)
```

```python
import functools
import math

import jax
import jax.numpy as jnp
import numpy as np
from jax import lax
from jax.experimental import pallas as pl
from jax.experimental.pallas import tpu as pltpu

F32 = jnp.float32
BF16 = jnp.bfloat16
U32 = jnp.uint32

EPS = 1e-6
D_MODEL = 1024
GLA_HEADS = 4
GLA_DK = 64
GLA_DV = 128
GLA_RANK = 16
GLA_TAU = 16.0
GLA_CHUNK = 64
GLA_QK_W = GLA_HEADS * GLA_DK
GLA_V_W = GLA_HEADS * GLA_DV
DIFF_HEADS = 4
DIFF_DQK = 64
DIFF_DV = 128
DIFF_QK_W = DIFF_HEADS * 2 * DIFF_DQK
DIFF_V_W = DIFF_HEADS * DIFF_DV
CROSS_HEADS = 4
CROSS_DH = D_MODEL // CROSS_HEADS
N_GROUPS = 4
EXPERTS_PER_GROUP = 4
N_EXPERTS = N_GROUPS * EXPERTS_PER_GROUP
TOP_K = 2
D_EXPERT = 512
LAMBDA_INIT = 0.8 - 0.6 * math.exp(-0.3 * 0)

LANES = 128
SUBLANES = 8
A_PAD = LANES
GLA_IN_W = 2 * GLA_QK_W + 2 * GLA_V_W + A_PAD
ROW_W = D_MODEL + LANES
PAIR_LO = (0, 0, 0, 1, 1, 2)
PAIR_HI = (1, 2, 3, 2, 3, 3)
N_PAIRS = len(PAIR_LO)
N_CLASSES = N_GROUPS * N_PAIRS

ROW_TILE = 512
MIX_TILE = 1024
SUB_TILE = 512
GLA_ROWS = 512
ATT_TILE = 512
ATT_ONES = 16
MOE_ROWS = 512
GATHER_ROWS = 1024
VMEM_LIMIT = 56 * 1024 * 1024


def _params(sem, vmem=VMEM_LIMIT, flags=None):
    return pltpu.CompilerParams(dimension_semantics=sem, vmem_limit_bytes=vmem, flags=flags)


def _rms(x, g):
    ms = jnp.mean(x * x, axis=-1, keepdims=True)
    return x * lax.rsqrt(ms + EPS) * g


def _dot(a, b):
    return jnp.dot(a, b, preferred_element_type=F32)


def _dot_nt(a, b):
    return lax.dot_general(a, b, (((1,), (1,)), ((), ())), preferred_element_type=F32)


def _norm_matmul_body(x_ref, g_ref, w_ref, *o_refs):
    h = _rms(x_ref[...], g_ref[...]).astype(BF16)
    off = 0
    for o_ref in o_refs:
        n = o_ref.shape[-1]
        o_ref[...] = _dot(h, w_ref[:, off:off + n]).astype(o_ref.dtype)
        off += n


def _norm_matmul(x, g, w, splits, name):
    t, d = x.shape
    n = w.shape[1]
    assert sum(splits) == n and t % ROW_TILE == 0
    return pl.pallas_call(
        _norm_matmul_body,
        grid=(t // ROW_TILE,),
        in_specs=[
            pl.BlockSpec((ROW_TILE, d), lambda i: (i, 0)),
            pl.BlockSpec((1, d), lambda i: (0, 0)),
            pl.BlockSpec((d, n), lambda i: (0, 0)),
        ],
        out_specs=[pl.BlockSpec((ROW_TILE, s), lambda i: (i, 0)) for s in splits],
        out_shape=[jax.ShapeDtypeStruct((t, s), BF16) for s in splits],
        compiler_params=_params(("parallel",)),
        name=name,
    )(x, g.reshape(1, d), w)


def _in_proj_body(x_ref, g_ref, w_ref, wvt_ref, gla_ref, dqk_ref, vt_ref):
    h = _rms(x_ref[...], g_ref[...]).astype(BF16)
    gla_ref[...] = _dot(h, w_ref[:, 0:GLA_IN_W]).astype(gla_ref.dtype)
    dqk_ref[...] = _dot(h, w_ref[:, GLA_IN_W:]).astype(dqk_ref.dtype)
    vt_ref[...] = _dot_nt(wvt_ref[...], h).astype(vt_ref.dtype)


def _in_proj(x, g, w, w_vt):
    t, d = x.shape
    n = w.shape[1]
    tm = ROW_TILE
    assert n == GLA_IN_W + 2 * DIFF_QK_W and t % tm == 0
    return pl.pallas_call(
        _in_proj_body,
        grid=(t // tm,),
        in_specs=[
            pl.BlockSpec((tm, d), lambda i: (i, 0)),
            pl.BlockSpec((1, d), lambda i: (0, 0)),
            pl.BlockSpec((d, n), lambda i: (0, 0)),
            pl.BlockSpec((DIFF_V_W, d), lambda i: (0, 0)),
        ],
        out_specs=[
            pl.BlockSpec((tm, GLA_IN_W), lambda i: (i, 0)),
            pl.BlockSpec((tm, 2 * DIFF_QK_W), lambda i: (i, 0)),
            pl.BlockSpec((DIFF_V_W, tm), lambda i: (0, i)),
        ],
        out_shape=[
            jax.ShapeDtypeStruct((t, GLA_IN_W), BF16),
            jax.ShapeDtypeStruct((t, 2 * DIFF_QK_W), BF16),
            jax.ShapeDtypeStruct((DIFF_V_W, t), BF16),
        ],
        compiler_params=_params(("parallel",)),
        name="in_proj",
    )(x, g.reshape(1, d), w, w_vt)


def _chunk_prefix_operator():
    i = np.arange(GLA_ROWS)
    same_chunk = (i[:, None] // GLA_CHUNK) == (i[None, :] // GLA_CHUNK)
    return jnp.asarray(same_chunk & (i[None, :] <= i[:, None]), dtype=BF16)


def _gla_body(in_ref, tri_ref, wa_ref, ba_ref, gn_ref, o_ref, s_ref):
    c_ = GLA_CHUNK
    n_c = GLA_ROWS // c_

    @pl.when(pl.program_id(1) == 0)
    def _():
        s_ref[...] = jnp.zeros_like(s_ref)

    q0, k0, v0, r0, a0 = 0, GLA_QK_W, 2 * GLA_QK_W, 2 * GLA_QK_W + GLA_V_W, 2 * GLA_QK_W + 2 * GLA_V_W
    z = _dot(in_ref[:, a0:a0 + A_PAD], wa_ref[...]) + ba_ref[...]
    log_a = jax.nn.log_sigmoid(z) / GLA_TAU
    la_hi = log_a.astype(BF16)
    la_lo = (log_a - la_hi.astype(F32)).astype(BF16)
    cum = _dot(tri_ref[...], la_hi) + _dot(tri_ref[...], la_lo)
    q = in_ref[:, q0:q0 + GLA_QK_W].astype(F32)
    k = in_ref[:, k0:k0 + GLA_QK_W].astype(F32)
    q_e = (q * (GLA_DK ** -0.5) * jnp.exp(cum)).astype(BF16)
    k_e = (k * jnp.exp(-cum)).astype(BF16)
    tots = [cum[(c + 1) * c_ - 1:(c + 1) * c_, :] for c in range(n_c)]
    k_s = jnp.concatenate([(k[c * c_:(c + 1) * c_] * jnp.exp(tots[c] - cum[c * c_:(c + 1) * c_])).astype(BF16)
                           for c in range(n_c)], axis=0)

    causal = (lax.broadcasted_iota(jnp.int32, (c_, c_), 0) >= lax.broadcasted_iota(jnp.int32, (c_, c_), 1))
    first_half = {n: lax.broadcasted_iota(jnp.int32, (n, LANES), 1) < GLA_DK for n in (GLA_ROWS, GLA_DV)}
    gn = gn_ref[...]

    for h in range(GLA_HEADS):
        tile = slice((h // 2) * LANES, (h // 2 + 1) * LANES)
        vs_ = slice(h * GLA_DV, (h + 1) * GLA_DV)
        head_lanes = first_half[GLA_ROWS] if h % 2 == 0 else ~first_half[GLA_ROWS]
        qe_m = jnp.where(head_lanes, q_e[:, tile], jnp.zeros((), BF16))
        ke_t = k_e[:, tile]
        ks_t = k_s[:, tile]
        v_h = in_ref[:, v0 + h * GLA_DV:v0 + (h + 1) * GLA_DV]
        vt_h = v_h.astype(F32).T.astype(BF16)
        o_intra, kv_t = [], []
        for c in range(n_c):
            rows = slice(c * c_, (c + 1) * c_)
            att = jnp.where(causal, _dot_nt(qe_m[rows], ke_t[rows]), 0.0).astype(BF16)
            o_intra.append(_dot(att, v_h[rows]))
            pair = slice((c // 2) * LANES, (c // 2 + 1) * LANES)
            chunk_lanes = first_half[GLA_DV] if c % 2 == 0 else ~first_half[GLA_DV]
            kv_t.append(_dot(jnp.where(chunk_lanes, vt_h[:, pair], jnp.zeros((), BF16)), ks_t[pair]))
        st = s_ref[h]
        s_prev = []
        for c in range(n_c):
            s_prev.append(st.astype(BF16))
            st = st * jnp.exp(tots[c][:, tile]) + kv_t[c]
        s_ref[h] = st
        r_h = in_ref[:, r0 + h * GLA_DV:r0 + (h + 1) * GLA_DV].astype(F32)
        for c in range(n_c):
            rows = slice(c * c_, (c + 1) * c_)
            o_h = o_intra[c] + _dot_nt(qe_m[rows], s_prev[c])
            o_ref[rows, vs_] = (_rms(o_h, gn) * jax.nn.silu(r_h[rows])).astype(o_ref.dtype)


def _gla(gla_in, w_a2p, b_a, gn, batch, seq):
    t = gla_in.shape[0]
    assert seq % GLA_ROWS == 0 and GLA_ROWS % (2 * GLA_CHUNK) == 0 and 2 * GLA_DK == LANES and 2 * GLA_CHUNK == LANES
    nsb = seq // GLA_ROWS
    return pl.pallas_call(
        _gla_body,
        grid=(batch, nsb),
        in_specs=[
            pl.BlockSpec((GLA_ROWS, GLA_IN_W), lambda b, j: (b * nsb + j, 0)),
            pl.BlockSpec((GLA_ROWS, GLA_ROWS), lambda b, j: (0, 0)),
            pl.BlockSpec((A_PAD, GLA_QK_W), lambda b, j: (0, 0)),
            pl.BlockSpec((1, GLA_QK_W), lambda b, j: (0, 0)),
            pl.BlockSpec((1, GLA_DV), lambda b, j: (0, 0)),
        ],
        out_specs=pl.BlockSpec((GLA_ROWS, GLA_V_W), lambda b, j: (b * nsb + j, 0)),
        out_shape=jax.ShapeDtypeStruct((t, GLA_V_W), BF16),
        scratch_shapes=[pltpu.VMEM((GLA_HEADS, GLA_DV, LANES), F32)],
        compiler_params=_params(("parallel", "arbitrary")),
        name="gla",
    )(gla_in, _chunk_prefix_operator(), w_a2p, b_a.reshape(1, -1), gn.reshape(1, -1))


def _alibi_features(seq):
    j = np.arange(seq) % ATT_TILE
    j_lo = j % 256
    j_hi = j - j_lo
    kf = np.zeros((seq, 2 * DIFF_DQK), np.float32)
    for base in (0, DIFF_DQK):
        for f in range(3):
            kf[:, base + f] = j_lo
            kf[:, base + 3 + f] = j_hi
    slopes = 2.0 ** (-8.0 * np.arange(1, DIFF_HEADS + 1) / DIFF_HEADS)
    c = jnp.asarray(slopes * math.log2(math.e), dtype=F32)
    c1 = c.astype(BF16)
    c2 = (c - c1.astype(F32)).astype(BF16)
    c3 = (c - c1.astype(F32) - c2.astype(F32)).astype(BF16)
    terms = jnp.stack([c1, c2, c3, c1, c2, c3], axis=1)
    qf = jnp.zeros((DIFF_HEADS, 2 * DIFF_DQK), BF16)
    qf = qf.at[:, 0:6].set(terms).at[:, DIFF_DQK:DIFF_DQK + 6].set(terms)
    qf = jnp.repeat(qf, 8, axis=0)
    return jnp.asarray(kf, dtype=BF16), qf, c


def _diff_body(c_ref, q_ref, k_ref, vt_ref, qf_ref, kf_ref, lam_ref, gn_ref, o_ref, kaug_ref, vta_ref):
    tq = ATT_TILE
    h = pl.program_id(1)
    seq = k_ref.shape[0]
    nq = seq // tq
    c = c_ref[h]

    k = k_ref[...]
    kf = kf_ref[...]
    lane_k = lax.broadcasted_iota(jnp.int32, k.shape, 1)
    kaug_ref[0] = jnp.where(lane_k < DIFF_DQK, k, kf)
    kaug_ref[1] = jnp.where(lane_k >= DIFF_DQK, k, kf)
    for jj in range(nq):
        vta_ref[jj, 0:DIFF_DV, :] = vt_ref[:, jj * tq:(jj + 1) * tq]
        vta_ref[jj, DIFF_DV:, :] = jnp.ones((ATT_ONES, tq), BF16)

    lq1, lk1, lq2, lk2 = (lam_ref[i:i + 1, :] for i in range(4))
    lam = (jnp.exp(jnp.sum(lq1 * lk1, axis=-1, keepdims=True))
           - jnp.exp(jnp.sum(lq2 * lk2, axis=-1, keepdims=True)) + LAMBDA_INIT)
    row = lax.broadcasted_iota(jnp.int32, (tq, tq), 0)
    col = lax.broadcasted_iota(jnp.int32, (tq, tq), 1)
    visible = row <= col
    lane = lax.broadcasted_iota(jnp.int32, (tq, 2 * DIFF_DQK), 1)
    qf = jnp.broadcast_to(qf_ref[0:1, :], (tq, 2 * DIFF_DQK))

    steps = [(qi, j) for qi in range(nq) for j in range(qi + 1)]
    q_maps = {}

    def scores(qi, j):
        if qi not in q_maps:
            q = q_ref[qi * tq:(qi + 1) * tq, :]
            q_maps[qi] = (jnp.where(lane < DIFF_DQK, q, qf), jnp.where(lane >= DIFF_DQK, q, qf))
        return [_dot_nt(kaug_ref[m, j * tq:(j + 1) * tq, :], q_maps[qi][m]) for m in range(2)]

    m_run = [None, None]
    acc = [None, None]
    s_next = scores(*steps[0])
    for t, (qi, j) in enumerate(steps):
        s_cur = s_next
        if t + 1 < len(steps):
            s_next = scores(*steps[t + 1])
        vblk = vta_ref[j]
        base = c * float(j * tq)
        for m in range(2):
            s_t = s_cur[m]
            if j == qi:
                s_t = jnp.where(visible, s_t, -jnp.inf)
            blk_max = jnp.max(s_t, axis=0, keepdims=True) + base
            if j == 0:
                m_new = blk_max
                acc[m] = _dot(vblk, jnp.exp2(s_t - (m_new - base)).astype(BF16))
            else:
                m_new = jnp.maximum(m_run[m], blk_max)
                alpha = jnp.exp2(m_run[m] - m_new)
                acc[m] = alpha * acc[m] + _dot(vblk, jnp.exp2(s_t - (m_new - base)).astype(BF16))
            m_run[m] = m_new
        if j == qi:
            o_t = (acc[0][0:DIFF_DV] / acc[0][DIFF_DV:DIFF_DV + 1]
                   - lam * (acc[1][0:DIFF_DV] / acc[1][DIFF_DV:DIFF_DV + 1]))
            o = o_t.T
            o_ref[qi * tq:(qi + 1) * tq, :] = (_rms(o, gn_ref[...]) * (1.0 - LAMBDA_INIT)).astype(o_ref.dtype)


def _diff_attn(dqk, v_t, lam_params, gn, batch, seq):
    t = dqk.shape[0]
    tq = ATT_TILE
    assert seq % tq == 0
    nq = seq // tq
    kfeat, qfeat, c = _alibi_features(seq)
    rows = DIFF_DV + ATT_ONES
    grid_spec = pltpu.PrefetchScalarGridSpec(
        num_scalar_prefetch=1,
        grid=(batch, DIFF_HEADS),
        in_specs=[
            pl.BlockSpec((seq, 2 * DIFF_DQK), lambda b, h, s: (b, h)),
            pl.BlockSpec((seq, 2 * DIFF_DQK), lambda b, h, s: (b, DIFF_HEADS + h)),
            pl.BlockSpec((DIFF_DV, seq), lambda b, h, s: (h, b)),
            pl.BlockSpec((8, 2 * DIFF_DQK), lambda b, h, s: (h, 0)),
            pl.BlockSpec((seq, 2 * DIFF_DQK), lambda b, h, s: (0, 0)),
            pl.BlockSpec((4, DIFF_DQK), lambda b, h, s: (0, 0)),
            pl.BlockSpec((1, DIFF_DV), lambda b, h, s: (0, 0)),
        ],
        out_specs=pl.BlockSpec((seq, DIFF_DV), lambda b, h, s: (b, h)),
        scratch_shapes=[
            pltpu.VMEM((2, seq, 2 * DIFF_DQK), BF16),
            pltpu.VMEM((nq, rows, tq), BF16),
        ],
    )
    return pl.pallas_call(
        _diff_body,
        grid_spec=grid_spec,
        out_shape=jax.ShapeDtypeStruct((t, DIFF_V_W), BF16),
        compiler_params=_params(("parallel", "parallel")),
        name="diff_attn",
    )(c, dqk, dqk, v_t, qfeat, kfeat, lam_params, gn.reshape(1, -1))


def _route(logits):
    lane = lax.broadcasted_iota(jnp.int32, logits.shape, 1)
    big = jnp.int32(LANES)
    neg = -jnp.inf

    def first_argmax(vals, vmax):
        return jnp.min(jnp.where(vals == vmax, lane, big), axis=-1, keepdims=True)

    is_grp = (lane >= N_EXPERTS) & (lane < N_EXPERTS + N_GROUPS)
    lg = jnp.where(is_grp, logits, neg)
    mg = jnp.max(lg, axis=-1, keepdims=True)
    p_g = 1.0 / jnp.sum(jnp.exp(lg - mg), axis=-1, keepdims=True)
    g_sel = first_argmax(lg, mg) - N_EXPERTS
    in_grp = (lane >= g_sel * EXPERTS_PER_GROUP) & (lane < (g_sel + 1) * EXPERTS_PER_GROUP)
    le = jnp.where(in_grp, logits, neg)
    m1 = jnp.max(le, axis=-1, keepdims=True)
    i1 = first_argmax(le, m1)
    le2 = jnp.where(lane == i1, neg, le)
    m2 = jnp.max(le2, axis=-1, keepdims=True)
    i2 = first_argmax(le2, m2)
    den = jnp.sum(jnp.exp(le - m1), axis=-1, keepdims=True)
    w1 = 1.0 / den
    w2 = jnp.exp(m2 - m1) / den
    wsum = w1 + w2
    g1 = p_g * w1 / wsum
    g2 = p_g * w2 / wsum
    first_is_lo = i1 < i2
    g_lo = jnp.where(first_is_lo, g1, g2)
    g_hi = jnp.where(first_is_lo, g2, g1)
    a = jnp.minimum(i1, i2) - g_sel * EXPERTS_PER_GROUP
    b = jnp.maximum(i1, i2) - g_sel * EXPERTS_PER_GROUP
    pair = 3 * a - jnp.where(a == 2, 1, 0) + (b - a - 1)
    cls = (g_sel * N_PAIRS + pair).astype(F32)
    out = jnp.where(lane == 0, g_lo, jnp.where(lane == 1, g_hi, cls))
    return jnp.where(lane < 3, out, 0.0)


def _post_mix_body(x_ref, og_ref, od_ref, wo_ref, gc_ref, wq_ref, km_ref, vm_ref, wco_ref, gf_ref, wr_ref, br_ref,
                   x2e_ref):
    tiles = [slice(j * SUB_TILE, (j + 1) * SUB_TILE) for j in range(MIX_TILE // SUB_TILE)]

    def out_proj(rows):
        return x_ref[rows] + _dot(og_ref[rows], wo_ref[0:GLA_V_W, :]) + _dot(od_ref[rows], wo_ref[GLA_V_W:, :])

    def cross_query(x1):
        return _dot(_rms(x1, gc_ref[...]).astype(BF16), wq_ref[...]).astype(BF16)

    def cross_attend(qc):
        heads = []
        for h in range(CROSS_HEADS):
            sl = slice(h * CROSS_DH, (h + 1) * CROSS_DH)
            s = _dot_nt(qc[:, sl], km_ref[:, sl]) * (CROSS_DH ** -0.5)
            s = s - jnp.max(s, axis=-1, keepdims=True)
            p = jnp.exp(s)
            p = p / jnp.sum(p, axis=-1, keepdims=True)
            heads.append(_dot(p.astype(BF16), vm_ref[:, sl]).astype(BF16))
        return jnp.concatenate(heads, axis=-1)

    def finish(rows, x2):
        h3 = _rms(x2, gf_ref[...]).astype(BF16)
        x2e_ref[rows, 0:D_MODEL] = x2
        x2e_ref[rows, D_MODEL:] = _route(_dot(h3, wr_ref[...]) + br_ref[...])

    x1 = [out_proj(rows) for rows in tiles]
    qc = [cross_query(v) for v in x1]
    oc = [cross_attend(v) for v in qc]
    x2 = [a + _dot(b, wco_ref[...]) for a, b in zip(x1, oc)]
    for rows, v in zip(tiles, x2):
        finish(rows, v)


def _post_mix(x, og, od, w_out, g_cross, w_cq, kmem, vmem, w_co, g_ffn, w_r, b_r, seq, mem_len):
    t, d = x.shape
    tm = MIX_TILE
    assert seq % tm == 0
    per_b = seq // tm
    full = lambda shape: pl.BlockSpec(shape, lambda i: (0, 0))
    rows = lambda w: pl.BlockSpec((tm, w), lambda i: (i, 0))
    return pl.pallas_call(
        _post_mix_body,
        grid=(t // tm,),
        in_specs=[
            rows(d), rows(GLA_V_W), rows(DIFF_V_W),
            full((d, d)), full((1, d)), full((d, d)),
            pl.BlockSpec((mem_len, d), lambda i: (i // per_b, 0)),
            pl.BlockSpec((mem_len, d), lambda i: (i // per_b, 0)),
            full((d, d)), full((1, d)), full((d, LANES)), full((1, LANES)),
        ],
        out_specs=rows(ROW_W),
        out_shape=jax.ShapeDtypeStruct((t, ROW_W), F32),
        compiler_params=_params(("parallel",)),
        name="post_mix",
    )(x, og, od, w_out, g_cross.reshape(1, d), w_cq, kmem, vmem, w_co, g_ffn.reshape(1, d), w_r, b_r)


def _dispatch_body(fill_ref, dest_ref, src_ref, dst_ref, sem):
    n8 = GATHER_ROWS // SUBLANES
    i = pl.program_id(0)

    def dst_row(d):
        return dst_ref.at[lax.shift_right_logical(d, 3), pl.ds(d & (SUBLANES - 1), 1)]

    def issue(r8, carry):
        for u in range(SUBLANES):
            d = dest_ref[0, 0, r8 * SUBLANES + u]
            pltpu.make_async_copy(src_ref.at[r8, pl.ds(u, 1)], dst_row(d), sem.at[0]).start(priority=u % 2)
        return carry

    lax.fori_loop(0, n8, issue, 0)

    def fill_padding(start):
        def go(cp):
            cp.start() if start else cp.wait()

        def fill_tiles(t0, nt):
            go(pltpu.make_async_copy(src_ref.at[pl.ds(0, nt)], dst_ref.at[pl.ds(t0, nt)], sem.at[1]))

        def per_class(c, carry):
            lo = fill_ref[3 * c]
            mid = fill_ref[3 * c + 1]
            hi = fill_ref[3 * c + 2]
            lax.fori_loop(lo, mid, lambda r, cc: (go(pltpu.make_async_copy(src_ref.at[0, pl.ds(0, 1)], dst_row(r),
                                                                             sem.at[1])), cc)[1], 0)
            tiles = lax.shift_right_logical(hi - mid, 3)
            off = lax.shift_right_logical(mid, 3)
            for bit in reversed(range((MOE_ROWS // SUBLANES).bit_length() - 1)):
                has = (tiles & (1 << bit)) != 0

                @pl.when(has)
                def _(off=off, bit=bit):
                    fill_tiles(off, 1 << bit)

                off = off + jnp.where(has, 1 << bit, 0)
            return carry

        lax.fori_loop(0, N_CLASSES, per_class, 0)
        blk8 = MOE_ROWS // SUBLANES
        n_blocks = dst_ref.shape[0] // blk8
        lax.fori_loop(fill_ref[3 * N_CLASSES], n_blocks, lambda b, cc: (fill_tiles(b * blk8, blk8), cc)[1], 0)

    @pl.when(i == 0)
    def _():
        fill_padding(True)
        fill_padding(False)

    pltpu.make_async_copy(src_ref, dst_ref.at[pl.ds(0, n8)], sem.at[0]).wait()


def _dispatch(src, dest, fill, n_rows):
    t, w = src.shape
    assert t % GATHER_ROWS == 0 and GATHER_ROWS >= MOE_ROWS and SUBLANES == 8
    nb = t // GATHER_ROWS
    n8 = GATHER_ROWS // SUBLANES
    grid_spec = pltpu.PrefetchScalarGridSpec(
        num_scalar_prefetch=1,
        grid=(nb,),
        in_specs=[
            pl.BlockSpec((1, 1, GATHER_ROWS), lambda i, f: (i, 0, 0), memory_space=pltpu.SMEM),
            pl.BlockSpec((n8, SUBLANES, w), lambda i, f: (i, 0, 0)),
        ],
        out_specs=pl.BlockSpec(memory_space=pl.ANY),
        scratch_shapes=[pltpu.SemaphoreType.DMA((2,))],
    )
    out = pl.pallas_call(
        _dispatch_body,
        grid_spec=grid_spec,
        out_shape=jax.ShapeDtypeStruct((n_rows // SUBLANES, SUBLANES, w), src.dtype),
        compiler_params=_params(("arbitrary",)),
        name="dispatch",
    )(fill, dest.reshape(nb, 1, GATHER_ROWS), src.reshape(t // SUBLANES, SUBLANES, w))
    return out.reshape(n_rows, w)


def _expert_body(lo_ref, hi_ref, nvalid_ref, xs_ref, gf_ref, wg_a, wu_a, wd_a, wg_b, wu_b, wd_b, ys_ref):
    i = pl.program_id(0)

    @pl.when(i < nvalid_ref[0])
    def _():
        xb = xs_ref[...]
        h = _rms(xb[:, 0:D_MODEL], gf_ref[...]).astype(BF16)

        gate_a, up_a = _dot(h, wg_a[...]), _dot(h, wu_a[...])
        gate_b, up_b = _dot(h, wg_b[...]), _dot(h, wu_b[...])
        y_a = _dot((jax.nn.silu(gate_a) * up_a).astype(BF16), wd_a[...])
        y_b = _dot((jax.nn.silu(gate_b) * up_b).astype(BF16), wd_b[...])
        g_lo = xb[:, D_MODEL:D_MODEL + 1]
        g_hi = xb[:, D_MODEL + 1:D_MODEL + 2]
        ys_ref[...] = g_lo * y_a + g_hi * y_b

    @pl.when(i >= nvalid_ref[0])
    def _():
        ys_ref[...] = jnp.zeros_like(ys_ref)


def _experts(xs, g_ffn, blk_lo, blk_hi, nvalid, w_g, w_u, w_d):
    n_rows = xs.shape[0]
    nb = n_rows // MOE_ROWS
    d, de = w_g.shape[1], w_g.shape[2]
    up = lambda sel: pl.BlockSpec((None, d, de), lambda i, lo, hi, nv: (sel(lo, hi)[i], 0, 0))
    down = lambda sel: pl.BlockSpec((None, de, d), lambda i, lo, hi, nv: (sel(lo, hi)[i], 0, 0))
    first = lambda lo, hi: lo
    second = lambda lo, hi: hi
    grid_spec = pltpu.PrefetchScalarGridSpec(
        num_scalar_prefetch=3,
        grid=(nb,),
        in_specs=[
            pl.BlockSpec((MOE_ROWS, ROW_W), lambda i, lo, hi, nv: (jnp.minimum(i, nv[0] - 1), 0)),
            pl.BlockSpec((1, d), lambda i, lo, hi, nv: (0, 0)),
            up(first), up(first), down(first), up(second), up(second), down(second),
        ],
        out_specs=pl.BlockSpec((MOE_ROWS, d), lambda i, lo, hi, nv: (i, 0)),
    )
    return pl.pallas_call(
        _expert_body,
        grid_spec=grid_spec,
        out_shape=jax.ShapeDtypeStruct((n_rows, d), F32),
        compiler_params=_params(("arbitrary",)),
        name="experts",
    )(blk_lo, blk_hi, nvalid, xs, g_ffn.reshape(1, d), w_g, w_u, w_d, w_g, w_u, w_d)


def _combine_body(cur_ref, nxt_ref, x2_ref, ys_ref, g_ref, o_ref, ybuf, sem):
    n8 = ROW_TILE // SUBLANES
    i = pl.program_id(0)
    last = pl.num_programs(0) - 1
    slot = i % 2

    def issue_tile(idx_ref, s):
        def issue(r8, carry):
            for u in range(SUBLANES):
                d = idx_ref[0, 0, r8 * SUBLANES + u]
                src = ys_ref.at[lax.shift_right_logical(d, 3), pl.ds(d & (SUBLANES - 1), 1)]
                pltpu.make_async_copy(src, ybuf.at[s, r8, pl.ds(u, 1)], sem.at[s]).start(priority=u % 2)
            return carry

        lax.fori_loop(0, n8, issue, 0)

    @pl.when(i == 0)
    def _():
        issue_tile(cur_ref, 0)

    @pl.when(i < last)
    def _():
        issue_tile(nxt_ref, 1 - slot)

    pltpu.make_async_copy(ys_ref.at[pl.ds(0, n8)], ybuf.at[slot], sem.at[slot]).wait()
    y = ybuf[slot].reshape(ROW_TILE, D_MODEL)
    o_ref[...] = _rms(x2_ref[...] + y, g_ref[...])


def _combine(x2e, ys, dest, g_final):
    t = x2e.shape[0]
    d = D_MODEL
    tm = ROW_TILE
    nt = t // tm
    idx = dest.reshape(nt, 1, tm)
    return pl.pallas_call(
        _combine_body,
        grid=(nt,),
        in_specs=[
            pl.BlockSpec((1, 1, tm), lambda i: (i, 0, 0), memory_space=pltpu.SMEM),
            pl.BlockSpec((1, 1, tm), lambda i: (jnp.minimum(i + 1, nt - 1), 0, 0), memory_space=pltpu.SMEM),
            pl.BlockSpec((tm, d), lambda i: (i, 0)),
            pl.BlockSpec(memory_space=pl.ANY),
            pl.BlockSpec((1, d), lambda i: (0, 0)),
        ],
        out_specs=pl.BlockSpec((tm, d), lambda i: (i, 0)),
        out_shape=jax.ShapeDtypeStruct((t, d), F32),
        scratch_shapes=[pltpu.VMEM((2, tm // SUBLANES, SUBLANES, d), F32), pltpu.SemaphoreType.DMA((2,))],
        compiler_params=_params(("arbitrary",)),
        name="combine",
    )(idx, idx, x2e, ys.reshape(ys.shape[0] // SUBLANES, SUBLANES, d), g_final.reshape(1, d))


def _dispatch_plan(cls, t):
    onehot = (cls[:, None] == jnp.arange(N_CLASSES, dtype=jnp.int32)[None, :]).astype(jnp.int32)
    ranks = jnp.cumsum(onehot, axis=0) - onehot
    counts = jnp.sum(onehot, axis=0)
    padded = ((counts + MOE_ROWS - 1) // MOE_ROWS) * MOE_ROWS
    pad_end = jnp.cumsum(padded)
    pad_start = pad_end - padded
    dest = jnp.sum(onehot * (ranks + pad_start[None, :]), axis=1).astype(jnp.int32)
    fill_lo = pad_start + counts
    fill_mid = jnp.minimum(((fill_lo + SUBLANES - 1) // SUBLANES) * SUBLANES, pad_end)
    nvalid = (pad_end[-1] // MOE_ROWS).astype(jnp.int32).reshape(1)
    fill = jnp.concatenate([jnp.stack([fill_lo, fill_mid, pad_end], axis=1).reshape(-1), nvalid]).astype(jnp.int32)
    n_blocks = t // MOE_ROWS + N_CLASSES
    blk_start = jnp.arange(n_blocks, dtype=jnp.int32) * MOE_ROWS
    blk_cls = jnp.minimum(jnp.sum((blk_start[:, None] >= pad_end[None, :]).astype(jnp.int32), axis=1), N_CLASSES - 1)
    grp = blk_cls // N_PAIRS
    pair_onehot = ((blk_cls % N_PAIRS)[:, None] == jnp.arange(N_PAIRS, dtype=jnp.int32)[None, :]).astype(jnp.int32)
    blk_lo = grp * EXPERTS_PER_GROUP + jnp.sum(pair_onehot * jnp.asarray(PAIR_LO, jnp.int32)[None, :], axis=1)
    blk_hi = grp * EXPERTS_PER_GROUP + jnp.sum(pair_onehot * jnp.asarray(PAIR_HI, jnp.int32)[None, :], axis=1)
    return dest, fill, blk_lo.astype(jnp.int32), blk_hi.astype(jnp.int32), nvalid, n_blocks * MOE_ROWS


def _regroup_w_in(w_in):
    splits = np.cumsum([GLA_QK_W, GLA_QK_W, GLA_V_W, GLA_RANK, GLA_V_W, DIFF_QK_W, DIFF_QK_W, DIFF_V_W])[:-1]
    q_g, k_g, v_g, a_lr, r_g, q_d, k_d, v_d = jnp.split(w_in, [int(i) for i in splits], axis=1)
    a_pad = jnp.pad(a_lr, ((0, 0), (0, A_PAD - GLA_RANK)))
    q_d = q_d * (DIFF_DQK ** -0.5 * math.log2(math.e))
    return jnp.concatenate([q_g, k_g, v_g, r_g, a_pad, q_d, k_d], axis=1).astype(BF16), v_d.T.astype(BF16)


def kernel(x, mem, norm_mix_g, w_in, w_gla_a2, b_gla_a, gla_norm_g, diff_norm_g, lambda_q1, lambda_k1, lambda_q2,
           lambda_k2, w_out, norm_cross_g, norm_mem_g, w_cq, w_ckv, w_co, norm_ffn_g, w_router_grp, b_router_grp,
           w_router_exp, b_router_exp, w_e_gate, w_e_up, w_e_down, norm_final_g):
    b_, s_, d_ = x.shape
    m_ = mem.shape[1]
    t = b_ * s_
    xf = x.reshape(t, d_)

    kmem, vmem = _norm_matmul(mem.reshape(b_ * m_, d_), norm_mem_g[0], w_ckv[0].astype(BF16), (d_, d_), "mem_kv")

    w_cols, w_vt = _regroup_w_in(w_in[0])
    gla_in, dqk, v_t = _in_proj(xf, norm_mix_g[0], w_cols, w_vt)

    w_a2p = jnp.pad(w_gla_a2[0], ((0, A_PAD - GLA_RANK), (0, 0))).astype(BF16)
    o_g = _gla(gla_in, w_a2p, b_gla_a[0], gla_norm_g[0], b_, s_)

    lam_params = jnp.stack([lambda_q1[0], lambda_k1[0], lambda_q2[0], lambda_k2[0]]).astype(F32)
    o_d = _diff_attn(dqk, v_t, lam_params, diff_norm_g[0], b_, s_)

    pad_r = LANES - N_EXPERTS - N_GROUPS
    w_r = jnp.pad(jnp.concatenate([w_router_exp[0], w_router_grp[0]], axis=1), ((0, 0), (0, pad_r))).astype(BF16)
    b_r = jnp.pad(jnp.concatenate([b_router_exp[0], b_router_grp[0]]), (0, pad_r)).reshape(1, LANES)
    x2e = _post_mix(xf, o_g, o_d, w_out[0].astype(BF16), norm_cross_g[0], w_cq[0].astype(BF16), kmem, vmem,
                    w_co[0].astype(BF16), norm_ffn_g[0], w_r, b_r, s_, m_)

    cls = x2e[:, D_MODEL + 2].astype(jnp.int32)
    dest, fill, blk_lo, blk_hi, nvalid, n_rows = _dispatch_plan(cls, t)
    xs = _dispatch(x2e, dest, fill, n_rows)
    ys = _experts(xs, norm_ffn_g[0], blk_lo, blk_hi, nvalid, w_e_gate[0].astype(BF16), w_e_up[0].astype(BF16),
                  w_e_down[0].astype(BF16))

    out = _combine(x2e, ys, dest, norm_final_g)
    return out.reshape(b_, s_, d_)
```

```python
import functools
import math

import jax
import jax.numpy as jnp
import numpy as np
from jax import lax
from jax.experimental import pallas as pl
from jax.experimental.pallas import tpu as pltpu

F32 = jnp.float32
BF16 = jnp.bfloat16
U32 = jnp.uint32

EPS = 1e-6
D_MODEL = 1024
GLA_HEADS = 4
GLA_DK = 64
GLA_DV = 128
GLA_RANK = 16
GLA_TAU = 16.0
GLA_CHUNK = 64
GLA_QK_W = GLA_HEADS * GLA_DK
GLA_V_W = GLA_HEADS * GLA_DV
DIFF_HEADS = 4
DIFF_DQK = 64
DIFF_DV = 128
DIFF_QK_W = DIFF_HEADS * 2 * DIFF_DQK
DIFF_V_W = DIFF_HEADS * DIFF_DV
CROSS_HEADS = 4
CROSS_DH = D_MODEL // CROSS_HEADS
N_GROUPS = 4
EXPERTS_PER_GROUP = 4
N_EXPERTS = N_GROUPS * EXPERTS_PER_GROUP
TOP_K = 2
D_EXPERT = 512
LAMBDA_INIT = 0.8 - 0.6 * math.exp(-0.3 * 0)

LANES = 128
SUBLANES = 8
A_PAD = LANES
GLA_IN_W = 2 * GLA_QK_W + 2 * GLA_V_W + A_PAD
ROW_W = D_MODEL + LANES
PAIR_LO = (0, 0, 0, 1, 1, 2)
PAIR_HI = (1, 2, 3, 2, 3, 3)
N_PAIRS = len(PAIR_LO)
N_CLASSES = N_GROUPS * N_PAIRS

ROW_TILE = 512
MIX_TILE = 1024
SUB_TILE = 512
GLA_ROWS = 512
ATT_TILE = 512
ATT_ONES = 16
MOE_ROWS = 512
GATHER_ROWS = 2048
VMEM_LIMIT = 56 * 1024 * 1024


def _params(sem, vmem=VMEM_LIMIT, flags=None):
    return pltpu.CompilerParams(dimension_semantics=sem, vmem_limit_bytes=vmem, flags=flags)


def _rms(x, g):
    ms = jnp.mean(x * x, axis=-1, keepdims=True)
    return x * lax.rsqrt(ms + EPS) * g


def _dot(a, b):
    return jnp.dot(a, b, preferred_element_type=F32)


def _dot_nt(a, b):
    return lax.dot_general(a, b, (((1,), (1,)), ((), ())), preferred_element_type=F32)


def _norm_matmul_body(x_ref, g_ref, w_ref, *o_refs):
    h = _rms(x_ref[...], g_ref[...]).astype(BF16)
    off = 0
    for o_ref in o_refs:
        n = o_ref.shape[-1]
        o_ref[...] = _dot(h, w_ref[:, off:off + n]).astype(o_ref.dtype)
        off += n


def _norm_matmul(x, g, w, splits, name):
    t, d = x.shape
    n = w.shape[1]
    assert sum(splits) == n and t % ROW_TILE == 0
    return pl.pallas_call(
        _norm_matmul_body,
        grid=(t // ROW_TILE,),
        in_specs=[
            pl.BlockSpec((ROW_TILE, d), lambda i: (i, 0)),
            pl.BlockSpec((1, d), lambda i: (0, 0)),
            pl.BlockSpec((d, n), lambda i: (0, 0)),
        ],
        out_specs=[pl.BlockSpec((ROW_TILE, s), lambda i: (i, 0)) for s in splits],
        out_shape=[jax.ShapeDtypeStruct((t, s), BF16) for s in splits],
        compiler_params=_params(("parallel",)),
        name=name,
    )(x, g.reshape(1, d), w)


def _in_proj_body(x_ref, g_ref, w_ref, wvt_ref, gla_ref, dqk_ref, vt_ref):
    h = _rms(x_ref[...], g_ref[...]).astype(BF16)
    gla_ref[...] = _dot(h, w_ref[:, 0:GLA_IN_W]).astype(gla_ref.dtype)
    dqk_ref[...] = _dot(h, w_ref[:, GLA_IN_W:]).astype(dqk_ref.dtype)
    vt_ref[...] = _dot_nt(wvt_ref[...], h).astype(vt_ref.dtype)


def _in_proj(x, g, w, w_vt):
    t, d = x.shape
    n = w.shape[1]
    tm = ROW_TILE
    assert n == GLA_IN_W + 2 * DIFF_QK_W and t % tm == 0
    return pl.pallas_call(
        _in_proj_body,
        grid=(t // tm,),
        in_specs=[
            pl.BlockSpec((tm, d), lambda i: (i, 0)),
            pl.BlockSpec((1, d), lambda i: (0, 0)),
            pl.BlockSpec((d, n), lambda i: (0, 0)),
            pl.BlockSpec((DIFF_V_W, d), lambda i: (0, 0)),
        ],
        out_specs=[
            pl.BlockSpec((tm, GLA_IN_W), lambda i: (i, 0)),
            pl.BlockSpec((tm, 2 * DIFF_QK_W), lambda i: (i, 0)),
            pl.BlockSpec((DIFF_V_W, tm), lambda i: (0, i)),
        ],
        out_shape=[
            jax.ShapeDtypeStruct((t, GLA_IN_W), BF16),
            jax.ShapeDtypeStruct((t, 2 * DIFF_QK_W), BF16),
            jax.ShapeDtypeStruct((DIFF_V_W, t), BF16),
        ],
        compiler_params=_params(("parallel",)),
        name="in_proj",
    )(x, g.reshape(1, d), w, w_vt)


def _chunk_prefix_operator():
    i = np.arange(2 * GLA_CHUNK)
    same_chunk = (i[:, None] // GLA_CHUNK) == (i[None, :] // GLA_CHUNK)
    return jnp.asarray(same_chunk & (i[None, :] <= i[:, None]), dtype=BF16)


def _gla_body(in_ref, tri_ref, wa_ref, ba_ref, gn_ref, o_ref, s_ref):
    c_ = GLA_CHUNK

    @pl.when(pl.program_id(1) == 0)
    def _():
        s_ref[...] = jnp.zeros_like(s_ref)

    q0, k0, v0, r0, a0 = 0, GLA_QK_W, 2 * GLA_QK_W, 2 * GLA_QK_W + GLA_V_W, 2 * GLA_QK_W + 2 * GLA_V_W
    g_ = 2 * c_
    n_g = GLA_ROWS // g_
    tri = tri_ref[...]
    causal = (lax.broadcasted_iota(jnp.int32, (c_, c_), 0) >= lax.broadcasted_iota(jnp.int32, (c_, c_), 1))
    first_half = lax.broadcasted_iota(jnp.int32, (g_, LANES), 1) < GLA_DK
    gn = gn_ref[...]

    def decays(g):
        rows = slice(g * g_, (g + 1) * g_)
        z = _dot(in_ref[rows, a0:a0 + A_PAD], wa_ref[...]) + ba_ref[...]
        log_a = jax.nn.log_sigmoid(z) / GLA_TAU
        la_hi = log_a.astype(BF16)
        la_lo = (log_a - la_hi.astype(F32)).astype(BF16)
        cum = _dot(tri, la_hi) + _dot(tri, la_lo)
        q = in_ref[rows, q0:q0 + GLA_QK_W].astype(F32)
        k = in_ref[rows, k0:k0 + GLA_QK_W].astype(F32)
        q_e = (q * (GLA_DK ** -0.5) * jnp.exp(cum)).astype(BF16)
        k_e = (k * jnp.exp(-cum)).astype(BF16)
        tots = [cum[(c + 1) * c_ - 1:(c + 1) * c_, :] for c in range(2)]
        k_s = jnp.concatenate([(k[c * c_:(c + 1) * c_] * jnp.exp(tots[c] - cum[c * c_:(c + 1) * c_])).astype(BF16)
                               for c in range(2)], axis=0)
        return q_e, k_e, k_s, tots

    def local_products(g, gates):
        q_e, k_e, k_s, _ = gates
        rows_g = slice(g * g_, (g + 1) * g_)
        out = []
        for h in range(GLA_HEADS):
            tile = slice((h // 2) * LANES, (h // 2 + 1) * LANES)
            head_lanes = first_half if h % 2 == 0 else ~first_half
            qe_m = jnp.where(head_lanes, q_e[:, tile], jnp.zeros((), BF16))
            ke_t = k_e[:, tile]
            ks_t = k_s[:, tile]
            v_h = in_ref[rows_g, v0 + h * GLA_DV:v0 + (h + 1) * GLA_DV]
            vt_h = v_h.astype(F32).T.astype(BF16)
            per_chunk = []
            for c in range(2):
                rows = slice(c * c_, (c + 1) * c_)
                att = jnp.where(causal, _dot_nt(qe_m[rows], ke_t[rows]), 0.0).astype(BF16)
                chunk_lanes = first_half if c == 0 else ~first_half
                kv_t = _dot(jnp.where(chunk_lanes, vt_h, jnp.zeros((), BF16)), ks_t)
                per_chunk.append((_dot(att, v_h[rows]), kv_t, qe_m[rows]))
            out.append(per_chunk)
        return out

    gates = [decays(0)]
    local = []
    for g in range(n_g):
        if g + 1 < n_g:
            gates.append(decays(g + 1))
        local.append(local_products(g, gates[g]))

    for h in range(GLA_HEADS):
        tile = slice((h // 2) * LANES, (h // 2 + 1) * LANES)
        st = s_ref[h]
        s_prev = []
        for g in range(n_g):
            for c in range(2):
                s_prev.append(st.astype(BF16))
                st = st * jnp.exp(gates[g][3][c][:, tile]) + local[g][h][c][1]
        s_ref[h] = st
        for g in range(n_g):
            r_h = in_ref[g * g_:(g + 1) * g_, r0 + h * GLA_DV:r0 + (h + 1) * GLA_DV].astype(F32)
            for c in range(2):
                o_intra, _, qe_c = local[g][h][c]
                o_h = o_intra + _dot_nt(qe_c, s_prev[2 * g + c])
                out_rows = slice(g * g_ + c * c_, g * g_ + (c + 1) * c_)
                o_ref[out_rows, h * GLA_DV:(h + 1) * GLA_DV] = (
                    _rms(o_h, gn) * jax.nn.silu(r_h[c * c_:(c + 1) * c_])).astype(o_ref.dtype)


def _gla(gla_in, w_a2p, b_a, gn, batch, seq):
    t = gla_in.shape[0]
    assert seq % GLA_ROWS == 0 and GLA_ROWS % (2 * GLA_CHUNK) == 0 and 2 * GLA_DK == LANES and 2 * GLA_CHUNK == LANES
    nsb = seq // GLA_ROWS
    return pl.pallas_call(
        _gla_body,
        grid=(batch, nsb),
        in_specs=[
            pl.BlockSpec((GLA_ROWS, GLA_IN_W), lambda b, j: (b * nsb + j, 0)),
            pl.BlockSpec((2 * GLA_CHUNK, 2 * GLA_CHUNK), lambda b, j: (0, 0)),
            pl.BlockSpec((A_PAD, GLA_QK_W), lambda b, j: (0, 0)),
            pl.BlockSpec((1, GLA_QK_W), lambda b, j: (0, 0)),
            pl.BlockSpec((1, GLA_DV), lambda b, j: (0, 0)),
        ],
        out_specs=pl.BlockSpec((GLA_ROWS, GLA_V_W), lambda b, j: (b * nsb + j, 0)),
        out_shape=jax.ShapeDtypeStruct((t, GLA_V_W), BF16),
        scratch_shapes=[pltpu.VMEM((GLA_HEADS, GLA_DV, LANES), F32)],
        compiler_params=_params(("parallel", "arbitrary")),
        name="gla",
    )(gla_in, _chunk_prefix_operator(), w_a2p, b_a.reshape(1, -1), gn.reshape(1, -1))


def _alibi_features(seq):
    j = np.arange(seq) % ATT_TILE
    j_lo = j % 256
    j_hi = j - j_lo
    kf = np.zeros((seq, 2 * DIFF_DQK), np.float32)
    for base in (0, DIFF_DQK):
        for f in range(3):
            kf[:, base + f] = j_lo
            kf[:, base + 3 + f] = j_hi
    slopes = 2.0 ** (-8.0 * np.arange(1, DIFF_HEADS + 1) / DIFF_HEADS)
    c = jnp.asarray(slopes * math.log2(math.e), dtype=F32)
    c1 = c.astype(BF16)
    c2 = (c - c1.astype(F32)).astype(BF16)
    c3 = (c - c1.astype(F32) - c2.astype(F32)).astype(BF16)
    terms = jnp.stack([c1, c2, c3, c1, c2, c3], axis=1)
    qf = jnp.zeros((DIFF_HEADS, 2 * DIFF_DQK), BF16)
    qf = qf.at[:, 0:6].set(terms).at[:, DIFF_DQK:DIFF_DQK + 6].set(terms)
    qf = jnp.repeat(qf, 8, axis=0)
    return jnp.asarray(kf, dtype=BF16), qf, c


def _diff_body(c_ref, q_ref, k_ref, vt_ref, qf_ref, kf_ref, lam_ref, gn_ref, o_ref, kaug_ref, vta_ref):
    tq = ATT_TILE
    h = pl.program_id(1)
    seq = k_ref.shape[0]
    nq = seq // tq
    c = c_ref[h]

    k = k_ref[...]
    kf = kf_ref[...]
    lane_k = lax.broadcasted_iota(jnp.int32, k.shape, 1)
    kaug_ref[0] = jnp.where(lane_k < DIFF_DQK, k, kf)
    kaug_ref[1] = jnp.where(lane_k >= DIFF_DQK, k, kf)
    for jj in range(nq):
        vta_ref[jj, 0:DIFF_DV, :] = vt_ref[:, jj * tq:(jj + 1) * tq]
        vta_ref[jj, DIFF_DV:, :] = jnp.ones((ATT_ONES, tq), BF16)

    lq1, lk1, lq2, lk2 = (lam_ref[i:i + 1, :] for i in range(4))
    lam = (jnp.exp(jnp.sum(lq1 * lk1, axis=-1, keepdims=True))
           - jnp.exp(jnp.sum(lq2 * lk2, axis=-1, keepdims=True)) + LAMBDA_INIT)
    row = lax.broadcasted_iota(jnp.int32, (tq, tq), 0)
    col = lax.broadcasted_iota(jnp.int32, (tq, tq), 1)
    visible = row <= col
    lane = lax.broadcasted_iota(jnp.int32, (tq, 2 * DIFF_DQK), 1)
    qf = jnp.broadcast_to(qf_ref[0:1, :], (tq, 2 * DIFF_DQK))

    steps = [(qi, j) for qi in range(nq) for j in range(qi + 1)]
    q_maps = {}

    def scores(qi, j):
        if qi not in q_maps:
            q = q_ref[qi * tq:(qi + 1) * tq, :]
            q_maps[qi] = (jnp.where(lane < DIFF_DQK, q, qf), jnp.where(lane >= DIFF_DQK, q, qf))
        return [_dot_nt(kaug_ref[m, j * tq:(j + 1) * tq, :], q_maps[qi][m]) for m in range(2)]

    m_run = [None, None]
    acc = [None, None]
    s_next = scores(*steps[0])
    for t, (qi, j) in enumerate(steps):
        s_cur = s_next
        if t + 1 < len(steps):
            s_next = scores(*steps[t + 1])
        vblk = vta_ref[j]
        base = c * float(j * tq)
        for m in range(2):
            s_t = s_cur[m]
            if j == qi:
                s_t = jnp.where(visible, s_t, -jnp.inf)
            blk_max = jnp.max(s_t, axis=0, keepdims=True) + base
            if j == 0:
                m_new = blk_max
                acc[m] = _dot(vblk, jnp.exp2(s_t - (m_new - base)).astype(BF16))
            else:
                m_new = jnp.maximum(m_run[m], blk_max)
                alpha = jnp.exp2(m_run[m] - m_new)
                acc[m] = alpha * acc[m] + _dot(vblk, jnp.exp2(s_t - (m_new - base)).astype(BF16))
            m_run[m] = m_new
        if j == qi:
            o_t = (acc[0][0:DIFF_DV] / acc[0][DIFF_DV:DIFF_DV + 1]
                   - lam * (acc[1][0:DIFF_DV] / acc[1][DIFF_DV:DIFF_DV + 1]))
            o = o_t.T
            o_ref[qi * tq:(qi + 1) * tq, :] = (_rms(o, gn_ref[...]) * (1.0 - LAMBDA_INIT)).astype(o_ref.dtype)


def _diff_attn(dqk, v_t, lam_params, gn, batch, seq):
    t = dqk.shape[0]
    tq = ATT_TILE
    assert seq % tq == 0
    nq = seq // tq
    kfeat, qfeat, c = _alibi_features(seq)
    rows = DIFF_DV + ATT_ONES
    grid_spec = pltpu.PrefetchScalarGridSpec(
        num_scalar_prefetch=1,
        grid=(batch, DIFF_HEADS),
        in_specs=[
            pl.BlockSpec((seq, 2 * DIFF_DQK), lambda b, h, s: (b, h)),
            pl.BlockSpec((seq, 2 * DIFF_DQK), lambda b, h, s: (b, DIFF_HEADS + h)),
            pl.BlockSpec((DIFF_DV, seq), lambda b, h, s: (h, b)),
            pl.BlockSpec((8, 2 * DIFF_DQK), lambda b, h, s: (h, 0)),
            pl.BlockSpec((seq, 2 * DIFF_DQK), lambda b, h, s: (0, 0)),
            pl.BlockSpec((4, DIFF_DQK), lambda b, h, s: (0, 0)),
            pl.BlockSpec((1, DIFF_DV), lambda b, h, s: (0, 0)),
        ],
        out_specs=pl.BlockSpec((seq, DIFF_DV), lambda b, h, s: (b, h)),
        scratch_shapes=[
            pltpu.VMEM((2, seq, 2 * DIFF_DQK), BF16),
            pltpu.VMEM((nq, rows, tq), BF16),
        ],
    )
    return pl.pallas_call(
        _diff_body,
        grid_spec=grid_spec,
        out_shape=jax.ShapeDtypeStruct((t, DIFF_V_W), BF16),
        compiler_params=_params(("parallel", "parallel")),
        name="diff_attn",
    )(c, dqk, dqk, v_t, qfeat, kfeat, lam_params, gn.reshape(1, -1))


def _route(logits):
    lane = lax.broadcasted_iota(jnp.int32, logits.shape, 1)
    big = jnp.int32(LANES)
    neg = -jnp.inf

    def first_argmax(vals, vmax):
        return jnp.min(jnp.where(vals == vmax, lane, big), axis=-1, keepdims=True)

    is_grp = (lane >= N_EXPERTS) & (lane < N_EXPERTS + N_GROUPS)
    lg = jnp.where(is_grp, logits, neg)
    mg = jnp.max(lg, axis=-1, keepdims=True)
    p_g = 1.0 / jnp.sum(jnp.exp(lg - mg), axis=-1, keepdims=True)
    g_sel = first_argmax(lg, mg) - N_EXPERTS
    in_grp = (lane >= g_sel * EXPERTS_PER_GROUP) & (lane < (g_sel + 1) * EXPERTS_PER_GROUP)
    le = jnp.where(in_grp, logits, neg)
    m1 = jnp.max(le, axis=-1, keepdims=True)
    i1 = first_argmax(le, m1)
    le2 = jnp.where(lane == i1, neg, le)
    m2 = jnp.max(le2, axis=-1, keepdims=True)
    i2 = first_argmax(le2, m2)
    den = jnp.sum(jnp.exp(le - m1), axis=-1, keepdims=True)
    w1 = 1.0 / den
    w2 = jnp.exp(m2 - m1) / den
    wsum = w1 + w2
    g1 = p_g * w1 / wsum
    g2 = p_g * w2 / wsum
    first_is_lo = i1 < i2
    g_lo = jnp.where(first_is_lo, g1, g2)
    g_hi = jnp.where(first_is_lo, g2, g1)
    a = jnp.minimum(i1, i2) - g_sel * EXPERTS_PER_GROUP
    b = jnp.maximum(i1, i2) - g_sel * EXPERTS_PER_GROUP
    pair = 3 * a - jnp.where(a == 2, 1, 0) + (b - a - 1)
    cls = (g_sel * N_PAIRS + pair).astype(F32)
    out = jnp.where(lane == 0, g_lo, jnp.where(lane == 1, g_hi, cls))
    return jnp.where(lane < 3, out, 0.0)


def _post_mix_body(x_ref, og_ref, od_ref, wo_ref, gc_ref, wq_ref, km_ref, vm_ref, wco_ref, gf_ref, wr_ref, br_ref,
                   x2e_ref):
    tiles = [slice(j * SUB_TILE, (j + 1) * SUB_TILE) for j in range(MIX_TILE // SUB_TILE)]

    def out_proj(rows):
        return x_ref[rows] + _dot(jnp.concatenate([og_ref[rows], od_ref[rows]], axis=1), wo_ref[...])

    def cross_query(x1):
        return _dot(_rms(x1, gc_ref[...]).astype(BF16), wq_ref[...]).astype(BF16)

    def cross_attend(qc):
        heads = []
        for h in range(CROSS_HEADS):
            sl = slice(h * CROSS_DH, (h + 1) * CROSS_DH)
            s = _dot_nt(qc[:, sl], km_ref[:, sl]) * (CROSS_DH ** -0.5)
            s = s - jnp.max(s, axis=-1, keepdims=True)
            p = jnp.exp(s)
            p = p / jnp.sum(p, axis=-1, keepdims=True)
            heads.append(_dot(p.astype(BF16), vm_ref[:, sl]).astype(BF16))
        return jnp.concatenate(heads, axis=-1)

    def finish(rows, x2):
        h3 = _rms(x2, gf_ref[...]).astype(BF16)
        x2e_ref[rows, 0:D_MODEL] = x2
        x2e_ref[rows, D_MODEL:] = _route(_dot(h3, wr_ref[...]) + br_ref[...])

    x1 = [out_proj(rows) for rows in tiles]
    qc = [cross_query(v) for v in x1]
    oc = [cross_attend(v) for v in qc]
    x2 = [a + _dot(b, wco_ref[...]) for a, b in zip(x1, oc)]
    for rows, v in zip(tiles, x2):
        finish(rows, v)


def _post_mix(x, og, od, w_out, g_cross, w_cq, kmem, vmem, w_co, g_ffn, w_r, b_r, seq, mem_len):
    t, d = x.shape
    tm = MIX_TILE
    assert seq % tm == 0
    per_b = seq // tm
    full = lambda shape: pl.BlockSpec(shape, lambda i: (0, 0))
    rows = lambda w: pl.BlockSpec((tm, w), lambda i: (i, 0))
    return pl.pallas_call(
        _post_mix_body,
        grid=(t // tm,),
        in_specs=[
            rows(d), rows(GLA_V_W), rows(DIFF_V_W),
            full((d, d)), full((1, d)), full((d, d)),
            pl.BlockSpec((mem_len, d), lambda i: (i // per_b, 0)),
            pl.BlockSpec((mem_len, d), lambda i: (i // per_b, 0)),
            full((d, d)), full((1, d)), full((d, LANES)), full((1, LANES)),
        ],
        out_specs=rows(ROW_W),
        out_shape=jax.ShapeDtypeStruct((t, ROW_W), F32),
        compiler_params=_params(("parallel",)),
        name="post_mix",
    )(x, og, od, w_out, g_cross.reshape(1, d), w_cq, kmem, vmem, w_co, g_ffn.reshape(1, d), w_r, b_r)


def _dispatch_body(fill_ref, dest_ref, src_ref, dst_ref, sem):
    n8 = GATHER_ROWS // SUBLANES
    i = pl.program_id(0)

    def dst_row(d):
        return dst_ref.at[lax.shift_right_logical(d, 3), pl.ds(d & (SUBLANES - 1), 1)]

    def issue(r8, carry):
        for u in range(SUBLANES):
            d = dest_ref[0, 0, r8 * SUBLANES + u]
            pltpu.make_async_copy(src_ref.at[r8, pl.ds(u, 1)], dst_row(d), sem.at[0]).start(priority=u % 2)
        return carry

    lax.fori_loop(0, n8, issue, 0)

    def fill_padding(start):
        def go(cp):
            cp.start() if start else cp.wait()

        def fill_tiles(t0, nt):
            go(pltpu.make_async_copy(src_ref.at[pl.ds(0, nt)], dst_ref.at[pl.ds(t0, nt)], sem.at[1]))

        def per_class(c, carry):
            lo = fill_ref[3 * c]
            mid = fill_ref[3 * c + 1]
            hi = fill_ref[3 * c + 2]
            lax.fori_loop(lo, mid, lambda r, cc: (go(pltpu.make_async_copy(src_ref.at[0, pl.ds(0, 1)], dst_row(r),
                                                                             sem.at[1])), cc)[1], 0)
            tiles = lax.shift_right_logical(hi - mid, 3)
            off = lax.shift_right_logical(mid, 3)
            for bit in reversed(range((MOE_ROWS // SUBLANES).bit_length() - 1)):
                has = (tiles & (1 << bit)) != 0

                @pl.when(has)
                def _(off=off, bit=bit):
                    fill_tiles(off, 1 << bit)

                off = off + jnp.where(has, 1 << bit, 0)
            return carry

        lax.fori_loop(0, N_CLASSES, per_class, 0)
        blk8 = MOE_ROWS // SUBLANES
        n_blocks = dst_ref.shape[0] // blk8
        lax.fori_loop(fill_ref[3 * N_CLASSES], n_blocks, lambda b, cc: (fill_tiles(b * blk8, blk8), cc)[1], 0)

    @pl.when(i == 0)
    def _():
        fill_padding(True)
        fill_padding(False)

    pltpu.make_async_copy(src_ref, dst_ref.at[pl.ds(0, n8)], sem.at[0]).wait()


def _dispatch(src, dest, fill, n_rows):
    t, w = src.shape
    assert t % GATHER_ROWS == 0 and GATHER_ROWS >= MOE_ROWS and SUBLANES == 8
    nb = t // GATHER_ROWS
    n8 = GATHER_ROWS // SUBLANES
    grid_spec = pltpu.PrefetchScalarGridSpec(
        num_scalar_prefetch=1,
        grid=(nb,),
        in_specs=[
            pl.BlockSpec((1, 1, GATHER_ROWS), lambda i, f: (i, 0, 0), memory_space=pltpu.SMEM),
            pl.BlockSpec((n8, SUBLANES, w), lambda i, f: (i, 0, 0)),
        ],
        out_specs=pl.BlockSpec(memory_space=pl.ANY),
        scratch_shapes=[pltpu.SemaphoreType.DMA((2,))],
    )
    out = pl.pallas_call(
        _dispatch_body,
        grid_spec=grid_spec,
        out_shape=jax.ShapeDtypeStruct((n_rows // SUBLANES, SUBLANES, w), src.dtype),
        compiler_params=_params(("arbitrary",)),
        name="dispatch",
    )(fill, dest.reshape(nb, 1, GATHER_ROWS), src.reshape(t // SUBLANES, SUBLANES, w))
    return out.reshape(n_rows, w)


def _expert_body(lo_ref, hi_ref, nvalid_ref, xs_ref, gf_ref, wg_a, wu_a, wd_a, wg_b, wu_b, wd_b, ys_ref):
    i = pl.program_id(0)

    @pl.when(i < nvalid_ref[0])
    def _():
        xb = xs_ref[...]
        h = _rms(xb[:, 0:D_MODEL], gf_ref[...]).astype(BF16)

        gate_a, up_a = _dot(h, wg_a[...]), _dot(h, wu_a[...])
        gate_b, up_b = _dot(h, wg_b[...]), _dot(h, wu_b[...])
        y_a = _dot((jax.nn.silu(gate_a) * up_a).astype(BF16), wd_a[...])
        y_b = _dot((jax.nn.silu(gate_b) * up_b).astype(BF16), wd_b[...])
        g_lo = xb[:, D_MODEL:D_MODEL + 1]
        g_hi = xb[:, D_MODEL + 1:D_MODEL + 2]
        ys_ref[...] = g_lo * y_a + g_hi * y_b

    @pl.when(i >= nvalid_ref[0])
    def _():
        ys_ref[...] = jnp.zeros_like(ys_ref)


def _experts(xs, g_ffn, blk_lo, blk_hi, nvalid, w_g, w_u, w_d):
    n_rows = xs.shape[0]
    nb = n_rows // MOE_ROWS
    d, de = w_g.shape[1], w_g.shape[2]
    up = lambda sel: pl.BlockSpec((None, d, de), lambda i, lo, hi, nv: (sel(lo, hi)[i], 0, 0))
    down = lambda sel: pl.BlockSpec((None, de, d), lambda i, lo, hi, nv: (sel(lo, hi)[i], 0, 0))
    first = lambda lo, hi: lo
    second = lambda lo, hi: hi
    grid_spec = pltpu.PrefetchScalarGridSpec(
        num_scalar_prefetch=3,
        grid=(nb,),
        in_specs=[
            pl.BlockSpec((MOE_ROWS, ROW_W), lambda i, lo, hi, nv: (jnp.minimum(i, nv[0] - 1), 0)),
            pl.BlockSpec((1, d), lambda i, lo, hi, nv: (0, 0)),
            up(first), up(first), down(first), up(second), up(second), down(second),
        ],
        out_specs=pl.BlockSpec((MOE_ROWS, d), lambda i, lo, hi, nv: (i, 0)),
    )
    return pl.pallas_call(
        _expert_body,
        grid_spec=grid_spec,
        out_shape=jax.ShapeDtypeStruct((n_rows, d), F32),
        compiler_params=_params(("arbitrary",)),
        name="experts",
    )(blk_lo, blk_hi, nvalid, xs, g_ffn.reshape(1, d), w_g, w_u, w_d, w_g, w_u, w_d)


def _combine_body(cur_ref, nxt_ref, x2_ref, ys_ref, g_ref, o_ref, ybuf, sem):
    n8 = ROW_TILE // SUBLANES
    i = pl.program_id(0)
    last = pl.num_programs(0) - 1
    slot = i % 2

    def issue_tile(idx_ref, s):
        def issue(r8, carry):
            for u in range(SUBLANES):
                d = idx_ref[0, 0, r8 * SUBLANES + u]
                src = ys_ref.at[lax.shift_right_logical(d, 3), pl.ds(d & (SUBLANES - 1), 1)]
                pltpu.make_async_copy(src, ybuf.at[s, r8, pl.ds(u, 1)], sem.at[s]).start(priority=u % 2)
            return carry

        lax.fori_loop(0, n8, issue, 0)

    @pl.when(i == 0)
    def _():
        issue_tile(cur_ref, 0)

    @pl.when(i < last)
    def _():
        issue_tile(nxt_ref, 1 - slot)

    pltpu.make_async_copy(ys_ref.at[pl.ds(0, n8)], ybuf.at[slot], sem.at[slot]).wait()
    y = ybuf[slot].reshape(ROW_TILE, D_MODEL)
    o_ref[...] = _rms(x2_ref[...] + y, g_ref[...])


def _combine(x2e, ys, dest, g_final):
    t = x2e.shape[0]
    d = D_MODEL
    tm = ROW_TILE
    nt = t // tm
    idx = dest.reshape(nt, 1, tm)
    return pl.pallas_call(
        _combine_body,
        grid=(nt,),
        in_specs=[
            pl.BlockSpec((1, 1, tm), lambda i: (i, 0, 0), memory_space=pltpu.SMEM),
            pl.BlockSpec((1, 1, tm), lambda i: (jnp.minimum(i + 1, nt - 1), 0, 0), memory_space=pltpu.SMEM),
            pl.BlockSpec((tm, d), lambda i: (i, 0)),
            pl.BlockSpec(memory_space=pl.ANY),
            pl.BlockSpec((1, d), lambda i: (0, 0)),
        ],
        out_specs=pl.BlockSpec((tm, d), lambda i: (i, 0)),
        out_shape=jax.ShapeDtypeStruct((t, d), F32),
        scratch_shapes=[pltpu.VMEM((2, tm // SUBLANES, SUBLANES, d), F32), pltpu.SemaphoreType.DMA((2,))],
        compiler_params=_params(("arbitrary",)),
        name="combine",
    )(idx, idx, x2e, ys.reshape(ys.shape[0] // SUBLANES, SUBLANES, d), g_final.reshape(1, d))


def _dispatch_plan(cls, t):
    onehot = (cls[:, None] == jnp.arange(N_CLASSES, dtype=jnp.int32)[None, :]).astype(jnp.int32)
    ranks = jnp.cumsum(onehot, axis=0) - onehot
    counts = jnp.sum(onehot, axis=0)
    padded = ((counts + MOE_ROWS - 1) // MOE_ROWS) * MOE_ROWS
    pad_end = jnp.cumsum(padded)
    pad_start = pad_end - padded
    dest = jnp.sum(onehot * (ranks + pad_start[None, :]), axis=1).astype(jnp.int32)
    fill_lo = pad_start + counts
    fill_mid = jnp.minimum(((fill_lo + SUBLANES - 1) // SUBLANES) * SUBLANES, pad_end)
    nvalid = (pad_end[-1] // MOE_ROWS).astype(jnp.int32).reshape(1)
    fill = jnp.concatenate([jnp.stack([fill_lo, fill_mid, pad_end], axis=1).reshape(-1), nvalid]).astype(jnp.int32)
    n_blocks = t // MOE_ROWS + N_CLASSES
    blk_start = jnp.arange(n_blocks, dtype=jnp.int32) * MOE_ROWS
    blk_cls = jnp.minimum(jnp.sum((blk_start[:, None] >= pad_end[None, :]).astype(jnp.int32), axis=1), N_CLASSES - 1)
    grp = blk_cls // N_PAIRS
    pair_onehot = ((blk_cls % N_PAIRS)[:, None] == jnp.arange(N_PAIRS, dtype=jnp.int32)[None, :]).astype(jnp.int32)
    blk_lo = grp * EXPERTS_PER_GROUP + jnp.sum(pair_onehot * jnp.asarray(PAIR_LO, jnp.int32)[None, :], axis=1)
    blk_hi = grp * EXPERTS_PER_GROUP + jnp.sum(pair_onehot * jnp.asarray(PAIR_HI, jnp.int32)[None, :], axis=1)
    return dest, fill, blk_lo.astype(jnp.int32), blk_hi.astype(jnp.int32), nvalid, n_blocks * MOE_ROWS


def _regroup_w_in(w_in):
    splits = np.cumsum([GLA_QK_W, GLA_QK_W, GLA_V_W, GLA_RANK, GLA_V_W, DIFF_QK_W, DIFF_QK_W, DIFF_V_W])[:-1]
    q_g, k_g, v_g, a_lr, r_g, q_d, k_d, v_d = jnp.split(w_in, [int(i) for i in splits], axis=1)
    a_pad = jnp.pad(a_lr, ((0, 0), (0, A_PAD - GLA_RANK)))
    q_d = q_d * (DIFF_DQK ** -0.5 * math.log2(math.e))
    return jnp.concatenate([q_g, k_g, v_g, r_g, a_pad, q_d, k_d], axis=1).astype(BF16), v_d.T.astype(BF16)


def kernel(x, mem, norm_mix_g, w_in, w_gla_a2, b_gla_a, gla_norm_g, diff_norm_g, lambda_q1, lambda_k1, lambda_q2,
           lambda_k2, w_out, norm_cross_g, norm_mem_g, w_cq, w_ckv, w_co, norm_ffn_g, w_router_grp, b_router_grp,
           w_router_exp, b_router_exp, w_e_gate, w_e_up, w_e_down, norm_final_g):
    b_, s_, d_ = x.shape
    m_ = mem.shape[1]
    t = b_ * s_
    xf = x.reshape(t, d_)

    kmem, vmem = _norm_matmul(mem.reshape(b_ * m_, d_), norm_mem_g[0], w_ckv[0].astype(BF16), (d_, d_), "mem_kv")

    w_cols, w_vt = _regroup_w_in(w_in[0])
    gla_in, dqk, v_t = _in_proj(xf, norm_mix_g[0], w_cols, w_vt)

    w_a2p = jnp.pad(w_gla_a2[0], ((0, A_PAD - GLA_RANK), (0, 0))).astype(BF16)
    o_g = _gla(gla_in, w_a2p, b_gla_a[0], gla_norm_g[0], b_, s_)

    lam_params = jnp.stack([lambda_q1[0], lambda_k1[0], lambda_q2[0], lambda_k2[0]]).astype(F32)
    o_d = _diff_attn(dqk, v_t, lam_params, diff_norm_g[0], b_, s_)

    pad_r = LANES - N_EXPERTS - N_GROUPS
    w_r = jnp.pad(jnp.concatenate([w_router_exp[0], w_router_grp[0]], axis=1), ((0, 0), (0, pad_r))).astype(BF16)
    b_r = jnp.pad(jnp.concatenate([b_router_exp[0], b_router_grp[0]]), (0, pad_r)).reshape(1, LANES)
    x2e = _post_mix(xf, o_g, o_d, w_out[0].astype(BF16), norm_cross_g[0], w_cq[0].astype(BF16), kmem, vmem,
                    w_co[0].astype(BF16), norm_ffn_g[0], w_r, b_r, s_, m_)

    cls = x2e[:, D_MODEL + 2].astype(jnp.int32)
    dest, fill, blk_lo, blk_hi, nvalid, n_rows = _dispatch_plan(cls, t)
    xs = _dispatch(x2e, dest, fill, n_rows)
    ys = _experts(xs, norm_ffn_g[0], blk_lo, blk_hi, nvalid, w_e_gate[0].astype(BF16), w_e_up[0].astype(BF16),
                  w_e_down[0].astype(BF16))

    out = _combine(x2e, ys, dest, norm_final_g)
    return out.reshape(b_, s_, d_)
```

```python
import functools
import math

import jax
import jax.numpy as jnp
import numpy as np
from jax import lax
from jax.experimental import pallas as pl
from jax.experimental.pallas import tpu as pltpu

F32 = jnp.float32
BF16 = jnp.bfloat16
U32 = jnp.uint32

EPS = 1e-6
D_MODEL = 1024
GLA_HEADS = 4
GLA_DK = 64
GLA_DV = 128
GLA_RANK = 16
GLA_TAU = 16.0
GLA_CHUNK = 64
GLA_QK_W = GLA_HEADS * GLA_DK
GLA_V_W = GLA_HEADS * GLA_DV
DIFF_HEADS = 4
DIFF_DQK = 64
DIFF_DV = 128
DIFF_QK_W = DIFF_HEADS * 2 * DIFF_DQK
DIFF_V_W = DIFF_HEADS * DIFF_DV
CROSS_HEADS = 4
CROSS_DH = D_MODEL // CROSS_HEADS
N_GROUPS = 4
EXPERTS_PER_GROUP = 4
N_EXPERTS = N_GROUPS * EXPERTS_PER_GROUP
TOP_K = 2
D_EXPERT = 512
LAMBDA_INIT = 0.8 - 0.6 * math.exp(-0.3 * 0)

LANES = 128
SUBLANES = 8
A_PAD = LANES
GLA_IN_W = 2 * GLA_QK_W + 2 * GLA_V_W + A_PAD
ROW_W = D_MODEL + LANES
PAIR_LO = (0, 0, 0, 1, 1, 2)
PAIR_HI = (1, 2, 3, 2, 3, 3)
N_PAIRS = len(PAIR_LO)
N_CLASSES = N_GROUPS * N_PAIRS

ROW_TILE = 512
MIX_TILE = 1024
SUB_TILE = 512
GLA_ROWS = 512
ATT_TILE = 512
ATT_ONES = 16
MOE_ROWS = 512
GATHER_ROWS = 2048
VMEM_LIMIT = 56 * 1024 * 1024


def _params(sem, vmem=VMEM_LIMIT, flags=None):
    return pltpu.CompilerParams(dimension_semantics=sem, vmem_limit_bytes=vmem, flags=flags)


def _rms(x, g):
    ms = jnp.mean(x * x, axis=-1, keepdims=True)
    return x * lax.rsqrt(ms + EPS) * g


def _dot(a, b):
    return jnp.dot(a, b, preferred_element_type=F32)


def _dot_nt(a, b):
    return lax.dot_general(a, b, (((1,), (1,)), ((), ())), preferred_element_type=F32)


def _norm_matmul_body(x_ref, g_ref, w_ref, *o_refs):
    h = _rms(x_ref[...], g_ref[...]).astype(BF16)
    off = 0
    for o_ref in o_refs:
        n = o_ref.shape[-1]
        o_ref[...] = _dot(h, w_ref[:, off:off + n]).astype(o_ref.dtype)
        off += n


def _norm_matmul(x, g, w, splits, name):
    t, d = x.shape
    n = w.shape[1]
    assert sum(splits) == n and t % ROW_TILE == 0
    return pl.pallas_call(
        _norm_matmul_body,
        grid=(t // ROW_TILE,),
        in_specs=[
            pl.BlockSpec((ROW_TILE, d), lambda i: (i, 0)),
            pl.BlockSpec((1, d), lambda i: (0, 0)),
            pl.BlockSpec((d, n), lambda i: (0, 0)),
        ],
        out_specs=[pl.BlockSpec((ROW_TILE, s), lambda i: (i, 0)) for s in splits],
        out_shape=[jax.ShapeDtypeStruct((t, s), BF16) for s in splits],
        compiler_params=_params(("parallel",)),
        name=name,
    )(x, g.reshape(1, d), w)


def _in_proj_body(x_ref, g_ref, w_ref, wvt_ref, gla_ref, dqk_ref, vt_ref):
    h = _rms(x_ref[...], g_ref[...]).astype(BF16)
    gla_ref[...] = _dot(h, w_ref[:, 0:GLA_IN_W]).astype(gla_ref.dtype)
    dqk_ref[...] = _dot(h, w_ref[:, GLA_IN_W:]).astype(dqk_ref.dtype)
    vt_ref[...] = _dot_nt(wvt_ref[...], h).astype(vt_ref.dtype)


def _in_proj(x, g, w, w_vt):
    t, d = x.shape
    n = w.shape[1]
    tm = ROW_TILE
    assert n == GLA_IN_W + 2 * DIFF_QK_W and t % tm == 0
    return pl.pallas_call(
        _in_proj_body,
        grid=(t // tm,),
        in_specs=[
            pl.BlockSpec((tm, d), lambda i: (i, 0)),
            pl.BlockSpec((1, d), lambda i: (0, 0)),
            pl.BlockSpec((d, n), lambda i: (0, 0)),
            pl.BlockSpec((DIFF_V_W, d), lambda i: (0, 0)),
        ],
        out_specs=[
            pl.BlockSpec((tm, GLA_IN_W), lambda i: (i, 0)),
            pl.BlockSpec((tm, 2 * DIFF_QK_W), lambda i: (i, 0)),
            pl.BlockSpec((DIFF_V_W, tm), lambda i: (0, i)),
        ],
        out_shape=[
            jax.ShapeDtypeStruct((t, GLA_IN_W), BF16),
            jax.ShapeDtypeStruct((t, 2 * DIFF_QK_W), BF16),
            jax.ShapeDtypeStruct((DIFF_V_W, t), BF16),
        ],
        compiler_params=_params(("parallel",)),
        name="in_proj",
    )(x, g.reshape(1, d), w, w_vt)


def _chunk_prefix_operator():
    i = np.arange(2 * GLA_CHUNK)
    same_chunk = (i[:, None] // GLA_CHUNK) == (i[None, :] // GLA_CHUNK)
    return jnp.asarray(same_chunk & (i[None, :] <= i[:, None]), dtype=BF16)


def _gla_body(in_ref, tri_ref, wa_ref, ba_ref, gn_ref, o_ref, s_ref):
    c_ = GLA_CHUNK

    @pl.when(pl.program_id(1) == 0)
    def _():
        s_ref[...] = jnp.zeros_like(s_ref)

    q0, k0, v0, r0, a0 = 0, GLA_QK_W, 2 * GLA_QK_W, 2 * GLA_QK_W + GLA_V_W, 2 * GLA_QK_W + 2 * GLA_V_W
    g_ = 2 * c_
    n_g = GLA_ROWS // g_
    tri = tri_ref[...]
    causal = (lax.broadcasted_iota(jnp.int32, (c_, c_), 0) >= lax.broadcasted_iota(jnp.int32, (c_, c_), 1))
    first_half = lax.broadcasted_iota(jnp.int32, (g_, LANES), 1) < GLA_DK
    gn = gn_ref[...]

    def decays(g):
        rows = slice(g * g_, (g + 1) * g_)
        z = _dot(in_ref[rows, a0:a0 + A_PAD], wa_ref[...]) + ba_ref[...]
        log_a = jax.nn.log_sigmoid(z) / GLA_TAU
        la_hi = log_a.astype(BF16)
        la_lo = (log_a - la_hi.astype(F32)).astype(BF16)
        cum = _dot(tri, la_hi) + _dot(tri, la_lo)
        q = in_ref[rows, q0:q0 + GLA_QK_W].astype(F32)
        k = in_ref[rows, k0:k0 + GLA_QK_W].astype(F32)
        q_e = (q * (GLA_DK ** -0.5) * jnp.exp(cum)).astype(BF16)
        k_e = (k * jnp.exp(-cum)).astype(BF16)
        tots = [cum[(c + 1) * c_ - 1:(c + 1) * c_, :] for c in range(2)]
        k_s = jnp.concatenate([(k[c * c_:(c + 1) * c_] * jnp.exp(tots[c] - cum[c * c_:(c + 1) * c_])).astype(BF16)
                               for c in range(2)], axis=0)
        return q_e, k_e, k_s, tots

    def local_products(g, gates):
        q_e, k_e, k_s, _ = gates
        rows_g = slice(g * g_, (g + 1) * g_)
        out = []
        for h in range(GLA_HEADS):
            tile = slice((h // 2) * LANES, (h // 2 + 1) * LANES)
            head_lanes = first_half if h % 2 == 0 else ~first_half
            qe_m = jnp.where(head_lanes, q_e[:, tile], jnp.zeros((), BF16))
            ke_t = k_e[:, tile]
            ks_t = k_s[:, tile]
            v_h = in_ref[rows_g, v0 + h * GLA_DV:v0 + (h + 1) * GLA_DV]
            vt_h = v_h.astype(F32).T.astype(BF16)
            per_chunk = []
            for c in range(2):
                rows = slice(c * c_, (c + 1) * c_)
                att = jnp.where(causal, _dot_nt(qe_m[rows], ke_t[rows]), 0.0).astype(BF16)
                chunk_lanes = first_half if c == 0 else ~first_half
                kv_t = _dot(jnp.where(chunk_lanes, vt_h, jnp.zeros((), BF16)), ks_t)
                per_chunk.append((_dot(att, v_h[rows]), kv_t, qe_m[rows]))
            out.append(per_chunk)
        return out

    gates = [decays(0)]
    local = []
    for g in range(n_g):
        if g + 1 < n_g:
            gates.append(decays(g + 1))
        local.append(local_products(g, gates[g]))

    for h in range(GLA_HEADS):
        tile = slice((h // 2) * LANES, (h // 2 + 1) * LANES)
        st = s_ref[h]
        s_prev = []
        for g in range(n_g):
            for c in range(2):
                s_prev.append(st.astype(BF16))
                st = st * jnp.exp(gates[g][3][c][:, tile]) + local[g][h][c][1]
        s_ref[h] = st
        for g in range(n_g):
            r_h = in_ref[g * g_:(g + 1) * g_, r0 + h * GLA_DV:r0 + (h + 1) * GLA_DV].astype(F32)
            for c in range(2):
                o_intra, _, qe_c = local[g][h][c]
                o_h = o_intra + _dot_nt(qe_c, s_prev[2 * g + c])
                out_rows = slice(g * g_ + c * c_, g * g_ + (c + 1) * c_)
                o_ref[out_rows, h * GLA_DV:(h + 1) * GLA_DV] = (
                    _rms(o_h, gn) * jax.nn.silu(r_h[c * c_:(c + 1) * c_])).astype(o_ref.dtype)


def _gla(gla_in, w_a2p, b_a, gn, batch, seq):
    t = gla_in.shape[0]
    assert seq % GLA_ROWS == 0 and GLA_ROWS % (2 * GLA_CHUNK) == 0 and 2 * GLA_DK == LANES and 2 * GLA_CHUNK == LANES
    nsb = seq // GLA_ROWS
    return pl.pallas_call(
        _gla_body,
        grid=(batch, nsb),
        in_specs=[
            pl.BlockSpec((GLA_ROWS, GLA_IN_W), lambda b, j: (b * nsb + j, 0)),
            pl.BlockSpec((2 * GLA_CHUNK, 2 * GLA_CHUNK), lambda b, j: (0, 0)),
            pl.BlockSpec((A_PAD, GLA_QK_W), lambda b, j: (0, 0)),
            pl.BlockSpec((1, GLA_QK_W), lambda b, j: (0, 0)),
            pl.BlockSpec((1, GLA_DV), lambda b, j: (0, 0)),
        ],
        out_specs=pl.BlockSpec((GLA_ROWS, GLA_V_W), lambda b, j: (b * nsb + j, 0)),
        out_shape=jax.ShapeDtypeStruct((t, GLA_V_W), BF16),
        scratch_shapes=[pltpu.VMEM((GLA_HEADS, GLA_DV, LANES), F32)],
        compiler_params=_params(("parallel", "arbitrary")),
        name="gla",
    )(gla_in, _chunk_prefix_operator(), w_a2p, b_a.reshape(1, -1), gn.reshape(1, -1))


def _alibi_features(seq):
    j = np.arange(seq) % ATT_TILE
    j_lo = j % 256
    j_hi = j - j_lo
    kf = np.zeros((seq, 2 * DIFF_DQK), np.float32)
    for base in (0, DIFF_DQK):
        for f in range(3):
            kf[:, base + f] = j_lo
            kf[:, base + 3 + f] = j_hi
    slopes = 2.0 ** (-8.0 * np.arange(1, DIFF_HEADS + 1) / DIFF_HEADS)
    c = jnp.asarray(slopes * math.log2(math.e), dtype=F32)
    c1 = c.astype(BF16)
    c2 = (c - c1.astype(F32)).astype(BF16)
    c3 = (c - c1.astype(F32) - c2.astype(F32)).astype(BF16)
    terms = jnp.stack([c1, c2, c3, c1, c2, c3], axis=1)
    qf = jnp.zeros((DIFF_HEADS, 2 * DIFF_DQK), BF16)
    qf = qf.at[:, 0:6].set(terms).at[:, DIFF_DQK:DIFF_DQK + 6].set(terms)
    qf = jnp.repeat(qf, 8, axis=0)
    return jnp.asarray(kf, dtype=BF16), qf, c


def _diff_body(c_ref, q_ref, k_ref, vt_ref, qf_ref, kf_ref, lam_ref, gn_ref, o_ref, kaug_ref, vta_ref):
    tq = ATT_TILE
    h = pl.program_id(1)
    seq = k_ref.shape[0]
    nq = seq // tq
    c = c_ref[h]

    k = k_ref[...]
    kf = kf_ref[...]
    lane_k = lax.broadcasted_iota(jnp.int32, k.shape, 1)
    kaug_ref[0] = jnp.where(lane_k < DIFF_DQK, k, kf)
    kaug_ref[1] = jnp.where(lane_k >= DIFF_DQK, k, kf)
    for jj in range(nq):
        vta_ref[jj, 0:DIFF_DV, :] = vt_ref[:, jj * tq:(jj + 1) * tq]
        vta_ref[jj, DIFF_DV:, :] = jnp.ones((ATT_ONES, tq), BF16)

    lq1, lk1, lq2, lk2 = (lam_ref[i:i + 1, :] for i in range(4))
    lam = (jnp.exp(jnp.sum(lq1 * lk1, axis=-1, keepdims=True))
           - jnp.exp(jnp.sum(lq2 * lk2, axis=-1, keepdims=True)) + LAMBDA_INIT)
    row = lax.broadcasted_iota(jnp.int32, (tq, tq), 0)
    col = lax.broadcasted_iota(jnp.int32, (tq, tq), 1)
    visible = row <= col
    lane = lax.broadcasted_iota(jnp.int32, (tq, 2 * DIFF_DQK), 1)
    qf = jnp.broadcast_to(qf_ref[0:1, :], (tq, 2 * DIFF_DQK))

    steps = [(qi, j) for qi in range(nq) for j in range(qi + 1)]
    q_maps = {}

    def scores(qi, j, m):
        if qi not in q_maps:
            q = q_ref[qi * tq:(qi + 1) * tq, :]
            q_maps[qi] = (jnp.where(lane < DIFF_DQK, q, qf), jnp.where(lane >= DIFF_DQK, q, qf))
        return _dot_nt(kaug_ref[m, j * tq:(j + 1) * tq, :], q_maps[qi][m])

    m_run = [None, None]
    acc = [None, None]
    s_next = [scores(*steps[0], m) for m in range(2)]
    for t, (qi, j) in enumerate(steps):
        s_cur = list(s_next)
        vblk = vta_ref[j]
        base = c * float(j * tq)
        for m in range(2):
            if t + 1 < len(steps):
                s_next[m] = scores(*steps[t + 1], m)
            s_t = s_cur[m]
            if j == qi:
                s_t = jnp.where(visible, s_t, -jnp.inf)
            blk_max = jnp.max(s_t, axis=0, keepdims=True) + base
            if j == 0:
                m_new = blk_max
                acc[m] = _dot(vblk, jnp.exp2(s_t - (m_new - base)).astype(BF16))
            else:
                m_new = jnp.maximum(m_run[m], blk_max)
                alpha = jnp.exp2(m_run[m] - m_new)
                acc[m] = alpha * acc[m] + _dot(vblk, jnp.exp2(s_t - (m_new - base)).astype(BF16))
            m_run[m] = m_new
        if j == qi:
            o_t = (acc[0][0:DIFF_DV] / acc[0][DIFF_DV:DIFF_DV + 1]
                   - lam * (acc[1][0:DIFF_DV] / acc[1][DIFF_DV:DIFF_DV + 1]))
            o = o_t.T
            o_ref[qi * tq:(qi + 1) * tq, :] = (_rms(o, gn_ref[...]) * (1.0 - LAMBDA_INIT)).astype(o_ref.dtype)


def _diff_attn(dqk, v_t, lam_params, gn, batch, seq):
    t = dqk.shape[0]
    tq = ATT_TILE
    assert seq % tq == 0
    nq = seq // tq
    kfeat, qfeat, c = _alibi_features(seq)
    rows = DIFF_DV + ATT_ONES
    grid_spec = pltpu.PrefetchScalarGridSpec(
        num_scalar_prefetch=1,
        grid=(batch, DIFF_HEADS),
        in_specs=[
            pl.BlockSpec((seq, 2 * DIFF_DQK), lambda b, h, s: (b, h)),
            pl.BlockSpec((seq, 2 * DIFF_DQK), lambda b, h, s: (b, DIFF_HEADS + h)),
            pl.BlockSpec((DIFF_DV, seq), lambda b, h, s: (h, b)),
            pl.BlockSpec((8, 2 * DIFF_DQK), lambda b, h, s: (h, 0)),
            pl.BlockSpec((seq, 2 * DIFF_DQK), lambda b, h, s: (0, 0)),
            pl.BlockSpec((4, DIFF_DQK), lambda b, h, s: (0, 0)),
            pl.BlockSpec((1, DIFF_DV), lambda b, h, s: (0, 0)),
        ],
        out_specs=pl.BlockSpec((seq, DIFF_DV), lambda b, h, s: (b, h)),
        scratch_shapes=[
            pltpu.VMEM((2, seq, 2 * DIFF_DQK), BF16),
            pltpu.VMEM((nq, rows, tq), BF16),
        ],
    )
    return pl.pallas_call(
        _diff_body,
        grid_spec=grid_spec,
        out_shape=jax.ShapeDtypeStruct((t, DIFF_V_W), BF16),
        compiler_params=_params(("parallel", "parallel")),
        name="diff_attn",
    )(c, dqk, dqk, v_t, qfeat, kfeat, lam_params, gn.reshape(1, -1))


def _route(logits):
    lane = lax.broadcasted_iota(jnp.int32, logits.shape, 0)
    big = jnp.int32(LANES)
    neg = -jnp.inf

    def first_argmax(vals, vmax):
        return jnp.min(jnp.where(vals == vmax, lane, big), axis=0, keepdims=True)

    is_grp = (lane >= N_EXPERTS) & (lane < N_EXPERTS + N_GROUPS)
    lg = jnp.where(is_grp, logits, neg)
    mg = jnp.max(lg, axis=0, keepdims=True)
    p_g = 1.0 / jnp.sum(jnp.exp(lg - mg), axis=0, keepdims=True)
    g_sel = first_argmax(lg, mg) - N_EXPERTS
    in_grp = (lane >= g_sel * EXPERTS_PER_GROUP) & (lane < (g_sel + 1) * EXPERTS_PER_GROUP)
    le = jnp.where(in_grp, logits, neg)
    m1 = jnp.max(le, axis=0, keepdims=True)
    i1 = first_argmax(le, m1)
    le2 = jnp.where(lane == i1, neg, le)
    m2 = jnp.max(le2, axis=0, keepdims=True)
    i2 = first_argmax(le2, m2)
    den = jnp.sum(jnp.exp(le - m1), axis=0, keepdims=True)
    w1 = 1.0 / den
    w2 = jnp.exp(m2 - m1) / den
    wsum = w1 + w2
    g1 = p_g * w1 / wsum
    g2 = p_g * w2 / wsum
    first_is_lo = i1 < i2
    g_lo = jnp.where(first_is_lo, g1, g2)
    g_hi = jnp.where(first_is_lo, g2, g1)
    a = jnp.minimum(i1, i2) - g_sel * EXPERTS_PER_GROUP
    b = jnp.maximum(i1, i2) - g_sel * EXPERTS_PER_GROUP
    pair = 3 * a - jnp.where(a == 2, 1, 0) + (b - a - 1)
    cls = (g_sel * N_PAIRS + pair).astype(F32)
    out = jnp.where(lane == 0, g_lo, jnp.where(lane == 1, g_hi, cls))
    return jnp.where(lane < 3, out, 0.0).T


def _post_mix_body(x_ref, og_ref, od_ref, wo_ref, gc_ref, wq_ref, km_ref, vm_ref, wco_ref, gf_ref, wr_ref, br_ref,
                   x2e_ref):
    tiles = [slice(j * SUB_TILE, (j + 1) * SUB_TILE) for j in range(MIX_TILE // SUB_TILE)]

    def out_proj(rows):
        return x_ref[rows] + _dot(jnp.concatenate([og_ref[rows], od_ref[rows]], axis=1), wo_ref[...])

    def cross_query(x1):
        return _dot(_rms(x1, gc_ref[...]).astype(BF16), wq_ref[...]).astype(BF16)

    def cross_attend(qc):
        heads = []
        for h in range(CROSS_HEADS):
            sl = slice(h * CROSS_DH, (h + 1) * CROSS_DH)
            s = _dot_nt(qc[:, sl], km_ref[:, sl]) * (CROSS_DH ** -0.5)
            s = s - jnp.max(s, axis=-1, keepdims=True)
            p = jnp.exp(s)
            p = p / jnp.sum(p, axis=-1, keepdims=True)
            heads.append(_dot(p.astype(BF16), vm_ref[:, sl]).astype(BF16))
        return jnp.concatenate(heads, axis=-1)

    def finish(rows, x2):
        h3 = _rms(x2, gf_ref[...]).astype(BF16)
        x2e_ref[rows, 0:D_MODEL] = x2
        x2e_ref[rows, D_MODEL:] = _route(_dot_nt(wr_ref[...], h3) + br_ref[...])

    x1 = [out_proj(rows) for rows in tiles]
    qc = [cross_query(v) for v in x1]
    oc = [cross_attend(v) for v in qc]
    x2 = [a + _dot(b, wco_ref[...]) for a, b in zip(x1, oc)]
    for rows, v in zip(tiles, x2):
        finish(rows, v)


def _post_mix(x, og, od, w_out, g_cross, w_cq, kmem, vmem, w_co, g_ffn, w_r, b_r, seq, mem_len):
    t, d = x.shape
    tm = MIX_TILE
    assert seq % tm == 0
    per_b = seq // tm
    full = lambda shape: pl.BlockSpec(shape, lambda i: (0, 0))
    rows = lambda w: pl.BlockSpec((tm, w), lambda i: (i, 0))
    return pl.pallas_call(
        _post_mix_body,
        grid=(t // tm,),
        in_specs=[
            rows(d), rows(GLA_V_W), rows(DIFF_V_W),
            full((d, d)), full((1, d)), full((d, d)),
            pl.BlockSpec((mem_len, d), lambda i: (i // per_b, 0)),
            pl.BlockSpec((mem_len, d), lambda i: (i // per_b, 0)),
            full((d, d)), full((1, d)), full((LANES, d)), full((LANES, 1)),
        ],
        out_specs=rows(ROW_W),
        out_shape=jax.ShapeDtypeStruct((t, ROW_W), F32),
        compiler_params=_params(("parallel",)),
        name="post_mix",
    )(x, og, od, w_out, g_cross.reshape(1, d), w_cq, kmem, vmem, w_co, g_ffn.reshape(1, d), w_r, b_r)


def _dispatch_body(fill_ref, dest_ref, src_ref, dst_ref, sem):
    n8 = GATHER_ROWS // SUBLANES
    i = pl.program_id(0)

    def dst_row(d):
        return dst_ref.at[lax.shift_right_logical(d, 3), pl.ds(d & (SUBLANES - 1), 1)]

    def issue(r8, carry):
        for u in range(SUBLANES):
            d = dest_ref[0, 0, r8 * SUBLANES + u]
            pltpu.make_async_copy(src_ref.at[r8, pl.ds(u, 1)], dst_row(d), sem.at[0]).start(priority=u % 2)
        return carry

    lax.fori_loop(0, n8, issue, 0)

    def fill_padding(start):
        def go(cp):
            cp.start() if start else cp.wait()

        def fill_tiles(t0, nt):
            go(pltpu.make_async_copy(src_ref.at[pl.ds(0, nt)], dst_ref.at[pl.ds(t0, nt)], sem.at[1]))

        def per_class(c, carry):
            lo = fill_ref[3 * c]
            mid = fill_ref[3 * c + 1]
            hi = fill_ref[3 * c + 2]
            lax.fori_loop(lo, mid, lambda r, cc: (go(pltpu.make_async_copy(src_ref.at[0, pl.ds(0, 1)], dst_row(r),
                                                                             sem.at[1])), cc)[1], 0)
            tiles = lax.shift_right_logical(hi - mid, 3)
            off = lax.shift_right_logical(mid, 3)
            for bit in reversed(range((MOE_ROWS // SUBLANES).bit_length() - 1)):
                has = (tiles & (1 << bit)) != 0

                @pl.when(has)
                def _(off=off, bit=bit):
                    fill_tiles(off, 1 << bit)

                off = off + jnp.where(has, 1 << bit, 0)
            return carry

        lax.fori_loop(0, N_CLASSES, per_class, 0)
        blk8 = MOE_ROWS // SUBLANES
        n_blocks = dst_ref.shape[0] // blk8
        lax.fori_loop(fill_ref[3 * N_CLASSES], n_blocks, lambda b, cc: (fill_tiles(b * blk8, blk8), cc)[1], 0)

    @pl.when(i == 0)
    def _():
        fill_padding(True)
        fill_padding(False)

    pltpu.make_async_copy(src_ref, dst_ref.at[pl.ds(0, n8)], sem.at[0]).wait()


def _dispatch(src, dest, fill, n_rows):
    t, w = src.shape
    assert t % GATHER_ROWS == 0 and GATHER_ROWS >= MOE_ROWS and SUBLANES == 8
    nb = t // GATHER_ROWS
    n8 = GATHER_ROWS // SUBLANES
    grid_spec = pltpu.PrefetchScalarGridSpec(
        num_scalar_prefetch=1,
        grid=(nb,),
        in_specs=[
            pl.BlockSpec((1, 1, GATHER_ROWS), lambda i, f: (i, 0, 0), memory_space=pltpu.SMEM),
            pl.BlockSpec((n8, SUBLANES, w), lambda i, f: (i, 0, 0)),
        ],
        out_specs=pl.BlockSpec(memory_space=pl.ANY),
        scratch_shapes=[pltpu.SemaphoreType.DMA((2,))],
    )
    out = pl.pallas_call(
        _dispatch_body,
        grid_spec=grid_spec,
        out_shape=jax.ShapeDtypeStruct((n_rows // SUBLANES, SUBLANES, w), src.dtype),
        compiler_params=_params(("arbitrary",)),
        name="dispatch",
    )(fill, dest.reshape(nb, 1, GATHER_ROWS), src.reshape(t // SUBLANES, SUBLANES, w))
    return out.reshape(n_rows, w)


def _expert_body(lo_ref, hi_ref, nvalid_ref, xs_ref, gf_ref, wg_a, wu_a, wd_a, wg_b, wu_b, wd_b, ys_ref):
    i = pl.program_id(0)

    @pl.when(i < nvalid_ref[0])
    def _():
        xb = xs_ref[...]
        h = _rms(xb[:, 0:D_MODEL], gf_ref[...]).astype(BF16)

        gate_a, up_a = _dot(h, wg_a[...]), _dot(h, wu_a[...])
        gate_b, up_b = _dot(h, wg_b[...]), _dot(h, wu_b[...])
        y_a = _dot((jax.nn.silu(gate_a) * up_a).astype(BF16), wd_a[...])
        y_b = _dot((jax.nn.silu(gate_b) * up_b).astype(BF16), wd_b[...])
        g_lo = xb[:, D_MODEL:D_MODEL + 1]
        g_hi = xb[:, D_MODEL + 1:D_MODEL + 2]
        ys_ref[...] = g_lo * y_a + g_hi * y_b

    @pl.when(i >= nvalid_ref[0])
    def _():
        ys_ref[...] = jnp.zeros_like(ys_ref)


def _experts(xs, g_ffn, blk_lo, blk_hi, nvalid, w_g, w_u, w_d):
    n_rows = xs.shape[0]
    nb = n_rows // MOE_ROWS
    d, de = w_g.shape[1], w_g.shape[2]
    up = lambda sel: pl.BlockSpec((None, d, de), lambda i, lo, hi, nv: (sel(lo, hi)[i], 0, 0))
    down = lambda sel: pl.BlockSpec((None, de, d), lambda i, lo, hi, nv: (sel(lo, hi)[i], 0, 0))
    first = lambda lo, hi: lo
    second = lambda lo, hi: hi
    grid_spec = pltpu.PrefetchScalarGridSpec(
        num_scalar_prefetch=3,
        grid=(nb,),
        in_specs=[
            pl.BlockSpec((MOE_ROWS, ROW_W), lambda i, lo, hi, nv: (jnp.minimum(i, nv[0] - 1), 0)),
            pl.BlockSpec((1, d), lambda i, lo, hi, nv: (0, 0)),
            up(first), up(first), down(first), up(second), up(second), down(second),
        ],
        out_specs=pl.BlockSpec((MOE_ROWS, d), lambda i, lo, hi, nv: (i, 0)),
    )
    return pl.pallas_call(
        _expert_body,
        grid_spec=grid_spec,
        out_shape=jax.ShapeDtypeStruct((n_rows, d), F32),
        compiler_params=_params(("arbitrary",)),
        name="experts",
    )(blk_lo, blk_hi, nvalid, xs, g_ffn.reshape(1, d), w_g, w_u, w_d, w_g, w_u, w_d)


def _combine_body(cur_ref, nxt_ref, x2_ref, ys_ref, g_ref, o_ref, ybuf, sem):
    n8 = ROW_TILE // SUBLANES
    i = pl.program_id(0)
    last = pl.num_programs(0) - 1
    slot = i % 2

    def issue_tile(idx_ref, s):
        def issue(r8, carry):
            for u in range(SUBLANES):
                d = idx_ref[0, 0, r8 * SUBLANES + u]
                src = ys_ref.at[lax.shift_right_logical(d, 3), pl.ds(d & (SUBLANES - 1), 1)]
                pltpu.make_async_copy(src, ybuf.at[s, r8, pl.ds(u, 1)], sem.at[s]).start(priority=u % 2)
            return carry

        lax.fori_loop(0, n8, issue, 0)

    @pl.when(i == 0)
    def _():
        issue_tile(cur_ref, 0)

    @pl.when(i < last)
    def _():
        issue_tile(nxt_ref, 1 - slot)

    pltpu.make_async_copy(ys_ref.at[pl.ds(0, n8)], ybuf.at[slot], sem.at[slot]).wait()
    y = ybuf[slot].reshape(ROW_TILE, D_MODEL)
    o_ref[...] = _rms(x2_ref[...] + y, g_ref[...])


def _combine(x2e, ys, dest, g_final):
    t = x2e.shape[0]
    d = D_MODEL
    tm = ROW_TILE
    nt = t // tm
    idx = dest.reshape(nt, 1, tm)
    return pl.pallas_call(
        _combine_body,
        grid=(nt,),
        in_specs=[
            pl.BlockSpec((1, 1, tm), lambda i: (i, 0, 0), memory_space=pltpu.SMEM),
            pl.BlockSpec((1, 1, tm), lambda i: (jnp.minimum(i + 1, nt - 1), 0, 0), memory_space=pltpu.SMEM),
            pl.BlockSpec((tm, d), lambda i: (i, 0)),
            pl.BlockSpec(memory_space=pl.ANY),
            pl.BlockSpec((1, d), lambda i: (0, 0)),
        ],
        out_specs=pl.BlockSpec((tm, d), lambda i: (i, 0)),
        out_shape=jax.ShapeDtypeStruct((t, d), F32),
        scratch_shapes=[pltpu.VMEM((2, tm // SUBLANES, SUBLANES, d), F32), pltpu.SemaphoreType.DMA((2,))],
        compiler_params=_params(("arbitrary",)),
        name="combine",
    )(idx, idx, x2e, ys.reshape(ys.shape[0] // SUBLANES, SUBLANES, d), g_final.reshape(1, d))


def _dispatch_plan(cls, t):
    onehot = (cls[:, None] == jnp.arange(N_CLASSES, dtype=jnp.int32)[None, :]).astype(jnp.int32)
    ranks = jnp.cumsum(onehot, axis=0) - onehot
    counts = jnp.sum(onehot, axis=0)
    padded = ((counts + MOE_ROWS - 1) // MOE_ROWS) * MOE_ROWS
    pad_end = jnp.cumsum(padded)
    pad_start = pad_end - padded
    dest = jnp.sum(onehot * (ranks + pad_start[None, :]), axis=1).astype(jnp.int32)
    fill_lo = pad_start + counts
    fill_mid = jnp.minimum(((fill_lo + SUBLANES - 1) // SUBLANES) * SUBLANES, pad_end)
    nvalid = (pad_end[-1] // MOE_ROWS).astype(jnp.int32).reshape(1)
    fill = jnp.concatenate([jnp.stack([fill_lo, fill_mid, pad_end], axis=1).reshape(-1), nvalid]).astype(jnp.int32)
    n_blocks = t // MOE_ROWS + N_CLASSES
    blk_start = jnp.arange(n_blocks, dtype=jnp.int32) * MOE_ROWS
    blk_cls = jnp.minimum(jnp.sum((blk_start[:, None] >= pad_end[None, :]).astype(jnp.int32), axis=1), N_CLASSES - 1)
    grp = blk_cls // N_PAIRS
    pair_onehot = ((blk_cls % N_PAIRS)[:, None] == jnp.arange(N_PAIRS, dtype=jnp.int32)[None, :]).astype(jnp.int32)
    blk_lo = grp * EXPERTS_PER_GROUP + jnp.sum(pair_onehot * jnp.asarray(PAIR_LO, jnp.int32)[None, :], axis=1)
    blk_hi = grp * EXPERTS_PER_GROUP + jnp.sum(pair_onehot * jnp.asarray(PAIR_HI, jnp.int32)[None, :], axis=1)
    return dest, fill, blk_lo.astype(jnp.int32), blk_hi.astype(jnp.int32), nvalid, n_blocks * MOE_ROWS


def _regroup_w_in(w_in):
    splits = np.cumsum([GLA_QK_W, GLA_QK_W, GLA_V_W, GLA_RANK, GLA_V_W, DIFF_QK_W, DIFF_QK_W, DIFF_V_W])[:-1]
    q_g, k_g, v_g, a_lr, r_g, q_d, k_d, v_d = jnp.split(w_in, [int(i) for i in splits], axis=1)
    a_pad = jnp.pad(a_lr, ((0, 0), (0, A_PAD - GLA_RANK)))
    q_d = q_d * (DIFF_DQK ** -0.5 * math.log2(math.e))
    return jnp.concatenate([q_g, k_g, v_g, r_g, a_pad, q_d, k_d], axis=1).astype(BF16), v_d.T.astype(BF16)


def kernel(x, mem, norm_mix_g, w_in, w_gla_a2, b_gla_a, gla_norm_g, diff_norm_g, lambda_q1, lambda_k1, lambda_q2,
           lambda_k2, w_out, norm_cross_g, norm_mem_g, w_cq, w_ckv, w_co, norm_ffn_g, w_router_grp, b_router_grp,
           w_router_exp, b_router_exp, w_e_gate, w_e_up, w_e_down, norm_final_g):
    b_, s_, d_ = x.shape
    m_ = mem.shape[1]
    t = b_ * s_
    xf = x.reshape(t, d_)

    kmem, vmem = _norm_matmul(mem.reshape(b_ * m_, d_), norm_mem_g[0], w_ckv[0].astype(BF16), (d_, d_), "mem_kv")

    w_cols, w_vt = _regroup_w_in(w_in[0])
    gla_in, dqk, v_t = _in_proj(xf, norm_mix_g[0], w_cols, w_vt)

    w_a2p = jnp.pad(w_gla_a2[0], ((0, A_PAD - GLA_RANK), (0, 0))).astype(BF16)
    o_g = _gla(gla_in, w_a2p, b_gla_a[0], gla_norm_g[0], b_, s_)

    lam_params = jnp.stack([lambda_q1[0], lambda_k1[0], lambda_q2[0], lambda_k2[0]]).astype(F32)
    o_d = _diff_attn(dqk, v_t, lam_params, diff_norm_g[0], b_, s_)

    pad_r = LANES - N_EXPERTS - N_GROUPS
    w_r = jnp.pad(jnp.concatenate([w_router_exp[0], w_router_grp[0]], axis=1).T, ((0, pad_r), (0, 0))).astype(BF16)
    b_r = jnp.pad(jnp.concatenate([b_router_exp[0], b_router_grp[0]]), (0, pad_r)).reshape(LANES, 1)
    x2e = _post_mix(xf, o_g, o_d, w_out[0].astype(BF16), norm_cross_g[0], w_cq[0].astype(BF16), kmem, vmem,
                    w_co[0].astype(BF16), norm_ffn_g[0], w_r, b_r, s_, m_)

    cls = x2e[:, D_MODEL + 2].astype(jnp.int32)
    dest, fill, blk_lo, blk_hi, nvalid, n_rows = _dispatch_plan(cls, t)
    xs = _dispatch(x2e, dest, fill, n_rows)
    ys = _experts(xs, norm_ffn_g[0], blk_lo, blk_hi, nvalid, w_e_gate[0].astype(BF16), w_e_up[0].astype(BF16),
                  w_e_down[0].astype(BF16))

    out = _combine(x2e, ys, dest, norm_final_g)
    return out.reshape(b_, s_, d_)
```

```python
import functools
import math

import jax
import jax.numpy as jnp
import numpy as np
from jax import lax
from jax.experimental import pallas as pl
from jax.experimental.pallas import tpu as pltpu

F32 = jnp.float32
BF16 = jnp.bfloat16
U32 = jnp.uint32

EPS = 1e-6
D_MODEL = 1024
GLA_HEADS = 4
GLA_DK = 64
GLA_DV = 128
GLA_RANK = 16
GLA_TAU = 16.0
GLA_CHUNK = 64
GLA_QK_W = GLA_HEADS * GLA_DK
GLA_V_W = GLA_HEADS * GLA_DV
DIFF_HEADS = 4
DIFF_DQK = 64
DIFF_DV = 128
DIFF_QK_W = DIFF_HEADS * 2 * DIFF_DQK
DIFF_V_W = DIFF_HEADS * DIFF_DV
CROSS_HEADS = 4
CROSS_DH = D_MODEL // CROSS_HEADS
N_GROUPS = 4
EXPERTS_PER_GROUP = 4
N_EXPERTS = N_GROUPS * EXPERTS_PER_GROUP
TOP_K = 2
D_EXPERT = 512
LAMBDA_INIT = 0.8 - 0.6 * math.exp(-0.3 * 0)

LANES = 128
SUBLANES = 8
A_PAD = LANES
GLA_IN_W = 2 * GLA_QK_W + 2 * GLA_V_W + A_PAD
ROW_W = D_MODEL + LANES
PAIR_LO = (0, 0, 0, 1, 1, 2)
PAIR_HI = (1, 2, 3, 2, 3, 3)
N_PAIRS = len(PAIR_LO)
N_CLASSES = N_GROUPS * N_PAIRS

ROW_TILE = 512
MIX_TILE = 1024
SUB_TILE = 512
GLA_ROWS = 2048
ATT_TILE = 512
ATT_ONES = 16
MOE_ROWS = 512
GATHER_ROWS = 2048
VMEM_LIMIT = 56 * 1024 * 1024


def _params(sem, vmem=VMEM_LIMIT, flags=None):
    return pltpu.CompilerParams(dimension_semantics=sem, vmem_limit_bytes=vmem, flags=flags)


def _rms(x, g):
    ms = jnp.mean(x * x, axis=-1, keepdims=True)
    return x * lax.rsqrt(ms + EPS) * g


def _dot(a, b):
    return jnp.dot(a, b, preferred_element_type=F32)


def _dot_nt(a, b):
    return lax.dot_general(a, b, (((1,), (1,)), ((), ())), preferred_element_type=F32)


def _norm_matmul_body(x_ref, g_ref, w_ref, *o_refs):
    h = _rms(x_ref[...], g_ref[...]).astype(BF16)
    off = 0
    for o_ref in o_refs:
        n = o_ref.shape[-1]
        o_ref[...] = _dot(h, w_ref[:, off:off + n]).astype(o_ref.dtype)
        off += n


def _norm_matmul(x, g, w, splits, name):
    t, d = x.shape
    n = w.shape[1]
    assert sum(splits) == n and t % ROW_TILE == 0
    return pl.pallas_call(
        _norm_matmul_body,
        grid=(t // ROW_TILE,),
        in_specs=[
            pl.BlockSpec((ROW_TILE, d), lambda i: (i, 0)),
            pl.BlockSpec((1, d), lambda i: (0, 0)),
            pl.BlockSpec((d, n), lambda i: (0, 0)),
        ],
        out_specs=[pl.BlockSpec((ROW_TILE, s), lambda i: (i, 0)) for s in splits],
        out_shape=[jax.ShapeDtypeStruct((t, s), BF16) for s in splits],
        compiler_params=_params(("parallel",)),
        name=name,
    )(x, g.reshape(1, d), w)


def _in_proj_body(x_ref, g_ref, w_ref, wvt_ref, gla_ref, dqk_ref, vt_ref):
    h = _rms(x_ref[...], g_ref[...]).astype(BF16)
    gla_ref[...] = _dot(h, w_ref[:, 0:GLA_IN_W]).astype(gla_ref.dtype)
    dqk_ref[...] = _dot(h, w_ref[:, GLA_IN_W:]).astype(dqk_ref.dtype)
    vt_ref[...] = _dot_nt(wvt_ref[...], h).astype(vt_ref.dtype)


def _in_proj(x, g, w, w_vt):
    t, d = x.shape
    n = w.shape[1]
    tm = ROW_TILE
    assert n == GLA_IN_W + 2 * DIFF_QK_W and t % tm == 0
    return pl.pallas_call(
        _in_proj_body,
        grid=(t // tm,),
        in_specs=[
            pl.BlockSpec((tm, d), lambda i: (i, 0)),
            pl.BlockSpec((1, d), lambda i: (0, 0)),
            pl.BlockSpec((d, n), lambda i: (0, 0)),
            pl.BlockSpec((DIFF_V_W, d), lambda i: (0, 0)),
        ],
        out_specs=[
            pl.BlockSpec((tm, GLA_IN_W), lambda i: (i, 0)),
            pl.BlockSpec((tm, 2 * DIFF_QK_W), lambda i: (i, 0)),
            pl.BlockSpec((DIFF_V_W, tm), lambda i: (0, i)),
        ],
        out_shape=[
            jax.ShapeDtypeStruct((t, GLA_IN_W), BF16),
            jax.ShapeDtypeStruct((t, 2 * DIFF_QK_W), BF16),
            jax.ShapeDtypeStruct((DIFF_V_W, t), BF16),
        ],
        compiler_params=_params(("parallel",)),
        name="in_proj",
    )(x, g.reshape(1, d), w, w_vt)


def _chunk_prefix_operator():
    i = np.arange(2 * GLA_CHUNK)
    same_chunk = (i[:, None] // GLA_CHUNK) == (i[None, :] // GLA_CHUNK)
    return jnp.asarray(same_chunk & (i[None, :] <= i[:, None]), dtype=BF16)


def _gla_body(in_ref, tri_ref, wa_ref, ba_ref, gn_ref, o_ref, s_ref):
    c_ = GLA_CHUNK

    @pl.when(pl.program_id(1) == 0)
    def _():
        s_ref[...] = jnp.zeros_like(s_ref)

    q0, k0, v0, r0, a0 = 0, GLA_QK_W, 2 * GLA_QK_W, 2 * GLA_QK_W + GLA_V_W, 2 * GLA_QK_W + 2 * GLA_V_W
    g_ = 2 * c_
    n_g = GLA_ROWS // g_
    tri = tri_ref[...]
    causal = (lax.broadcasted_iota(jnp.int32, (c_, c_), 0) >= lax.broadcasted_iota(jnp.int32, (c_, c_), 1))
    first_half = lax.broadcasted_iota(jnp.int32, (g_, LANES), 1) < GLA_DK
    gn = gn_ref[...]

    def decays(g):
        rows = slice(g * g_, (g + 1) * g_)
        z = _dot(in_ref[rows, a0:a0 + A_PAD], wa_ref[...]) + ba_ref[...]
        log_a = jax.nn.log_sigmoid(z) / GLA_TAU
        la_hi = log_a.astype(BF16)
        la_lo = (log_a - la_hi.astype(F32)).astype(BF16)
        cum = _dot(tri, la_hi) + _dot(tri, la_lo)
        q = in_ref[rows, q0:q0 + GLA_QK_W].astype(F32)
        k = in_ref[rows, k0:k0 + GLA_QK_W].astype(F32)
        q_e = (q * (GLA_DK ** -0.5) * jnp.exp(cum)).astype(BF16)
        k_e = (k * jnp.exp(-cum)).astype(BF16)
        tots = [cum[(c + 1) * c_ - 1:(c + 1) * c_, :] for c in range(2)]
        k_s = jnp.concatenate([(k[c * c_:(c + 1) * c_] * jnp.exp(tots[c] - cum[c * c_:(c + 1) * c_])).astype(BF16)
                               for c in range(2)], axis=0)
        return q_e, k_e, k_s, tots

    def local_products(g, gates):
        q_e, k_e, k_s, _ = gates
        rows_g = slice(g * g_, (g + 1) * g_)
        out = []
        for h in range(GLA_HEADS):
            tile = slice((h // 2) * LANES, (h // 2 + 1) * LANES)
            head_lanes = first_half if h % 2 == 0 else ~first_half
            qe_m = jnp.where(head_lanes, q_e[:, tile], jnp.zeros((), BF16))
            ke_t = k_e[:, tile]
            ks_t = k_s[:, tile]
            v_h = in_ref[rows_g, v0 + h * GLA_DV:v0 + (h + 1) * GLA_DV]
            vt_h = v_h.astype(F32).T.astype(BF16)
            per_chunk = []
            for c in range(2):
                rows = slice(c * c_, (c + 1) * c_)
                att = jnp.where(causal, _dot_nt(qe_m[rows], ke_t[rows]), 0.0).astype(BF16)
                chunk_lanes = first_half if c == 0 else ~first_half
                kv_t = _dot(jnp.where(chunk_lanes, vt_h, jnp.zeros((), BF16)), ks_t)
                per_chunk.append((_dot(att, v_h[rows]), kv_t, qe_m[rows]))
            out.append(per_chunk)
        return out

    gates = [decays(0)]
    local = []
    for g in range(n_g):
        if g + 1 < n_g:
            gates.append(decays(g + 1))
        local.append(local_products(g, gates[g]))

    for h in range(GLA_HEADS):
        tile = slice((h // 2) * LANES, (h // 2 + 1) * LANES)
        st = s_ref[h]
        s_prev = []
        for g in range(n_g):
            for c in range(2):
                s_prev.append(st.astype(BF16))
                st = st * jnp.exp(gates[g][3][c][:, tile]) + local[g][h][c][1]
        s_ref[h] = st
        for g in range(n_g):
            r_h = in_ref[g * g_:(g + 1) * g_, r0 + h * GLA_DV:r0 + (h + 1) * GLA_DV].astype(F32)
            for c in range(2):
                o_intra, _, qe_c = local[g][h][c]
                o_h = o_intra + _dot_nt(qe_c, s_prev[2 * g + c])
                out_rows = slice(g * g_ + c * c_, g * g_ + (c + 1) * c_)
                o_ref[out_rows, h * GLA_DV:(h + 1) * GLA_DV] = (
                    _rms(o_h, gn) * jax.nn.silu(r_h[c * c_:(c + 1) * c_])).astype(o_ref.dtype)


def _gla(gla_in, w_a2p, b_a, gn, batch, seq):
    t = gla_in.shape[0]
    assert seq % GLA_ROWS == 0 and GLA_ROWS % (2 * GLA_CHUNK) == 0 and 2 * GLA_DK == LANES and 2 * GLA_CHUNK == LANES
    nsb = seq // GLA_ROWS
    return pl.pallas_call(
        _gla_body,
        grid=(batch, nsb),
        in_specs=[
            pl.BlockSpec((GLA_ROWS, GLA_IN_W), lambda b, j: (b * nsb + j, 0)),
            pl.BlockSpec((2 * GLA_CHUNK, 2 * GLA_CHUNK), lambda b, j: (0, 0)),
            pl.BlockSpec((A_PAD, GLA_QK_W), lambda b, j: (0, 0)),
            pl.BlockSpec((1, GLA_QK_W), lambda b, j: (0, 0)),
            pl.BlockSpec((1, GLA_DV), lambda b, j: (0, 0)),
        ],
        out_specs=pl.BlockSpec((GLA_ROWS, GLA_V_W), lambda b, j: (b * nsb + j, 0)),
        out_shape=jax.ShapeDtypeStruct((t, GLA_V_W), BF16),
        scratch_shapes=[pltpu.VMEM((GLA_HEADS, GLA_DV, LANES), F32)],
        compiler_params=_params(("parallel", "arbitrary")),
        name="gla",
    )(gla_in, _chunk_prefix_operator(), w_a2p, b_a.reshape(1, -1), gn.reshape(1, -1))


def _alibi_features(seq):
    j = np.arange(seq) % ATT_TILE
    j_lo = j % 256
    j_hi = j - j_lo
    kf = np.zeros((seq, 2 * DIFF_DQK), np.float32)
    for base in (0, DIFF_DQK):
        for f in range(3):
            kf[:, base + f] = j_lo
            kf[:, base + 3 + f] = j_hi
    slopes = 2.0 ** (-8.0 * np.arange(1, DIFF_HEADS + 1) / DIFF_HEADS)
    c = jnp.asarray(slopes * math.log2(math.e), dtype=F32)
    c1 = c.astype(BF16)
    c2 = (c - c1.astype(F32)).astype(BF16)
    c3 = (c - c1.astype(F32) - c2.astype(F32)).astype(BF16)
    terms = jnp.stack([c1, c2, c3, c1, c2, c3], axis=1)
    qf = jnp.zeros((DIFF_HEADS, 2 * DIFF_DQK), BF16)
    qf = qf.at[:, 0:6].set(terms).at[:, DIFF_DQK:DIFF_DQK + 6].set(terms)
    qf = jnp.repeat(qf, 8, axis=0)
    return jnp.asarray(kf, dtype=BF16), qf, c


def _diff_body(c_ref, q_ref, k_ref, vt_ref, qf_ref, kf_ref, lam_ref, gn_ref, o_ref, kaug_ref, vta_ref):
    tq = ATT_TILE
    h = pl.program_id(1)
    seq = k_ref.shape[0]
    nq = seq // tq
    c = c_ref[h]

    k = k_ref[...]
    kf = kf_ref[...]
    lane_k = lax.broadcasted_iota(jnp.int32, k.shape, 1)
    kaug_ref[0] = jnp.where(lane_k < DIFF_DQK, k, kf)
    kaug_ref[1] = jnp.where(lane_k >= DIFF_DQK, k, kf)
    for jj in range(nq):
        vta_ref[jj, 0:DIFF_DV, :] = vt_ref[:, jj * tq:(jj + 1) * tq]
        vta_ref[jj, DIFF_DV:, :] = jnp.ones((ATT_ONES, tq), BF16)

    lq1, lk1, lq2, lk2 = (lam_ref[i:i + 1, :] for i in range(4))
    lam = (jnp.exp(jnp.sum(lq1 * lk1, axis=-1, keepdims=True))
           - jnp.exp(jnp.sum(lq2 * lk2, axis=-1, keepdims=True)) + LAMBDA_INIT)
    row = lax.broadcasted_iota(jnp.int32, (tq, tq), 0)
    col = lax.broadcasted_iota(jnp.int32, (tq, tq), 1)
    visible = row <= col
    lane = lax.broadcasted_iota(jnp.int32, (tq, 2 * DIFF_DQK), 1)
    qf = jnp.broadcast_to(qf_ref[0:1, :], (tq, 2 * DIFF_DQK))

    steps = [(qi, j) for qi in range(nq) for j in range(qi + 1)]
    q_maps = {}

    def scores(qi, j):
        if qi not in q_maps:
            q = q_ref[qi * tq:(qi + 1) * tq, :]
            q_maps[qi] = (jnp.where(lane < DIFF_DQK, q, qf), jnp.where(lane >= DIFF_DQK, q, qf))
        return [_dot_nt(kaug_ref[m, j * tq:(j + 1) * tq, :], q_maps[qi][m]) for m in range(2)]

    m_run = [None, None]
    acc = [None, None]
    s_next = scores(*steps[0])
    for t, (qi, j) in enumerate(steps):
        s_cur = s_next
        if t + 1 < len(steps):
            s_next = scores(*steps[t + 1])
        vblk = vta_ref[j]
        base = c * float(j * tq)
        for m in range(2):
            s_t = s_cur[m]
            if j == qi:
                s_t = jnp.where(visible, s_t, -jnp.inf)
            blk_max = jnp.max(s_t, axis=0, keepdims=True) + base
            if j == 0:
                m_new = blk_max
                acc[m] = _dot(vblk, jnp.exp2(s_t - (m_new - base)).astype(BF16))
            else:
                m_new = jnp.maximum(m_run[m], blk_max)
                alpha = jnp.exp2(m_run[m] - m_new)
                acc[m] = alpha * acc[m] + _dot(vblk, jnp.exp2(s_t - (m_new - base)).astype(BF16))
            m_run[m] = m_new
        if j == qi:
            o_t = (acc[0][0:DIFF_DV] / acc[0][DIFF_DV:DIFF_DV + 1]
                   - lam * (acc[1][0:DIFF_DV] / acc[1][DIFF_DV:DIFF_DV + 1]))
            o = o_t.T
            o_ref[qi * tq:(qi + 1) * tq, :] = (_rms(o, gn_ref[...]) * (1.0 - LAMBDA_INIT)).astype(o_ref.dtype)


def _diff_attn(dqk, v_t, lam_params, gn, batch, seq):
    t = dqk.shape[0]
    tq = ATT_TILE
    assert seq % tq == 0
    nq = seq // tq
    kfeat, qfeat, c = _alibi_features(seq)
    rows = DIFF_DV + ATT_ONES
    grid_spec = pltpu.PrefetchScalarGridSpec(
        num_scalar_prefetch=1,
        grid=(batch, DIFF_HEADS),
        in_specs=[
            pl.BlockSpec((seq, 2 * DIFF_DQK), lambda b, h, s: (b, h)),
            pl.BlockSpec((seq, 2 * DIFF_DQK), lambda b, h, s: (b, DIFF_HEADS + h)),
            pl.BlockSpec((DIFF_DV, seq), lambda b, h, s: (h, b)),
            pl.BlockSpec((8, 2 * DIFF_DQK), lambda b, h, s: (h, 0)),
            pl.BlockSpec((seq, 2 * DIFF_DQK), lambda b, h, s: (0, 0)),
            pl.BlockSpec((4, DIFF_DQK), lambda b, h, s: (0, 0)),
            pl.BlockSpec((1, DIFF_DV), lambda b, h, s: (0, 0)),
        ],
        out_specs=pl.BlockSpec((seq, DIFF_DV), lambda b, h, s: (b, h)),
        scratch_shapes=[
            pltpu.VMEM((2, seq, 2 * DIFF_DQK), BF16),
            pltpu.VMEM((nq, rows, tq), BF16),
        ],
    )
    return pl.pallas_call(
        _diff_body,
        grid_spec=grid_spec,
        out_shape=jax.ShapeDtypeStruct((t, DIFF_V_W), BF16),
        compiler_params=_params(("parallel", "parallel")),
        name="diff_attn",
    )(c, dqk, dqk, v_t, qfeat, kfeat, lam_params, gn.reshape(1, -1))


def _route(logits):
    lane = lax.broadcasted_iota(jnp.int32, logits.shape, 0)
    big = jnp.int32(LANES)
    neg = -jnp.inf

    def first_argmax(vals, vmax):
        return jnp.min(jnp.where(vals == vmax, lane, big), axis=0, keepdims=True)

    is_grp = (lane >= N_EXPERTS) & (lane < N_EXPERTS + N_GROUPS)
    lg = jnp.where(is_grp, logits, neg)
    mg = jnp.max(lg, axis=0, keepdims=True)
    p_g = 1.0 / jnp.sum(jnp.exp(lg - mg), axis=0, keepdims=True)
    g_sel = first_argmax(lg, mg) - N_EXPERTS
    in_grp = (lane >= g_sel * EXPERTS_PER_GROUP) & (lane < (g_sel + 1) * EXPERTS_PER_GROUP)
    le = jnp.where(in_grp, logits, neg)
    m1 = jnp.max(le, axis=0, keepdims=True)
    i1 = first_argmax(le, m1)
    le2 = jnp.where(lane == i1, neg, le)
    m2 = jnp.max(le2, axis=0, keepdims=True)
    i2 = first_argmax(le2, m2)
    den = jnp.sum(jnp.exp(le - m1), axis=0, keepdims=True)
    w1 = 1.0 / den
    w2 = jnp.exp(m2 - m1) / den
    wsum = w1 + w2
    g1 = p_g * w1 / wsum
    g2 = p_g * w2 / wsum
    first_is_lo = i1 < i2
    g_lo = jnp.where(first_is_lo, g1, g2)
    g_hi = jnp.where(first_is_lo, g2, g1)
    a = jnp.minimum(i1, i2) - g_sel * EXPERTS_PER_GROUP
    b = jnp.maximum(i1, i2) - g_sel * EXPERTS_PER_GROUP
    pair = 3 * a - jnp.where(a == 2, 1, 0) + (b - a - 1)
    cls = (g_sel * N_PAIRS + pair).astype(F32)
    out = jnp.where(lane == 0, g_lo, jnp.where(lane == 1, g_hi, cls))
    return jnp.where(lane < 3, out, 0.0).T


def _post_mix_body(x_ref, og_ref, od_ref, wo_ref, gc_ref, wq_ref, km_ref, vm_ref, wco_ref, gf_ref, wr_ref, br_ref,
                   x2e_ref):
    tiles = [slice(j * SUB_TILE, (j + 1) * SUB_TILE) for j in range(MIX_TILE // SUB_TILE)]

    def out_proj(rows):
        return x_ref[rows] + _dot(jnp.concatenate([og_ref[rows], od_ref[rows]], axis=1), wo_ref[...])

    def cross_query(x1):
        return _dot(_rms(x1, gc_ref[...]).astype(BF16), wq_ref[...]).astype(BF16)

    def cross_attend(qc):
        heads = []
        for h in range(CROSS_HEADS):
            sl = slice(h * CROSS_DH, (h + 1) * CROSS_DH)
            s = _dot_nt(qc[:, sl], km_ref[:, sl]) * (CROSS_DH ** -0.5)
            s = s - jnp.max(s, axis=-1, keepdims=True)
            p = jnp.exp(s)
            p = p / jnp.sum(p, axis=-1, keepdims=True)
            heads.append(_dot(p.astype(BF16), vm_ref[:, sl]).astype(BF16))
        return jnp.concatenate(heads, axis=-1)

    def finish(rows, x2):
        h3 = _rms(x2, gf_ref[...]).astype(BF16)
        x2e_ref[rows, 0:D_MODEL] = x2
        x2e_ref[rows, D_MODEL:] = _route(_dot_nt(wr_ref[...], h3) + br_ref[...])

    x1 = [out_proj(rows) for rows in tiles]
    qc = [cross_query(v) for v in x1]
    oc = [cross_attend(v) for v in qc]
    x2 = [a + _dot(b, wco_ref[...]) for a, b in zip(x1, oc)]
    for rows, v in zip(tiles, x2):
        finish(rows, v)


def _post_mix(x, og, od, w_out, g_cross, w_cq, kmem, vmem, w_co, g_ffn, w_r, b_r, seq, mem_len):
    t, d = x.shape
    tm = MIX_TILE
    assert seq % tm == 0
    per_b = seq // tm
    full = lambda shape: pl.BlockSpec(shape, lambda i: (0, 0))
    rows = lambda w: pl.BlockSpec((tm, w), lambda i: (i, 0))
    return pl.pallas_call(
        _post_mix_body,
        grid=(t // tm,),
        in_specs=[
            rows(d), rows(GLA_V_W), rows(DIFF_V_W),
            full((d, d)), full((1, d)), full((d, d)),
            pl.BlockSpec((mem_len, d), lambda i: (i // per_b, 0)),
            pl.BlockSpec((mem_len, d), lambda i: (i // per_b, 0)),
            full((d, d)), full((1, d)), full((LANES, d)), full((LANES, 1)),
        ],
        out_specs=rows(ROW_W),
        out_shape=jax.ShapeDtypeStruct((t, ROW_W), F32),
        compiler_params=_params(("parallel",)),
        name="post_mix",
    )(x, og, od, w_out, g_cross.reshape(1, d), w_cq, kmem, vmem, w_co, g_ffn.reshape(1, d), w_r, b_r)


def _dispatch_body(fill_ref, dest_ref, src_ref, dst_ref, sem):
    n8 = GATHER_ROWS // SUBLANES
    i = pl.program_id(0)

    def dst_row(d):
        return dst_ref.at[lax.shift_right_logical(d, 3), pl.ds(d & (SUBLANES - 1), 1)]

    def issue(r8, carry):
        for u in range(SUBLANES):
            d = dest_ref[0, 0, r8 * SUBLANES + u]
            pltpu.make_async_copy(src_ref.at[r8, pl.ds(u, 1)], dst_row(d), sem.at[0]).start(priority=u % 2)
        return carry

    lax.fori_loop(0, n8, issue, 0)

    def fill_padding(start):
        def go(cp):
            cp.start() if start else cp.wait()

        def fill_tiles(t0, nt):
            go(pltpu.make_async_copy(src_ref.at[pl.ds(0, nt)], dst_ref.at[pl.ds(t0, nt)], sem.at[1]))

        def per_class(c, carry):
            lo = fill_ref[3 * c]
            mid = fill_ref[3 * c + 1]
            hi = fill_ref[3 * c + 2]
            lax.fori_loop(lo, mid, lambda r, cc: (go(pltpu.make_async_copy(src_ref.at[0, pl.ds(0, 1)], dst_row(r),
                                                                             sem.at[1])), cc)[1], 0)
            tiles = lax.shift_right_logical(hi - mid, 3)
            off = lax.shift_right_logical(mid, 3)
            for bit in reversed(range((MOE_ROWS // SUBLANES).bit_length() - 1)):
                has = (tiles & (1 << bit)) != 0

                @pl.when(has)
                def _(off=off, bit=bit):
                    fill_tiles(off, 1 << bit)

                off = off + jnp.where(has, 1 << bit, 0)
            return carry

        lax.fori_loop(0, N_CLASSES, per_class, 0)
        blk8 = MOE_ROWS // SUBLANES
        n_blocks = dst_ref.shape[0] // blk8
        lax.fori_loop(fill_ref[3 * N_CLASSES], n_blocks, lambda b, cc: (fill_tiles(b * blk8, blk8), cc)[1], 0)

    @pl.when(i == 0)
    def _():
        fill_padding(True)
        fill_padding(False)

    pltpu.make_async_copy(src_ref, dst_ref.at[pl.ds(0, n8)], sem.at[0]).wait()


def _dispatch(src, dest, fill, n_rows):
    t, w = src.shape
    assert t % GATHER_ROWS == 0 and GATHER_ROWS >= MOE_ROWS and SUBLANES == 8
    nb = t // GATHER_ROWS
    n8 = GATHER_ROWS // SUBLANES
    grid_spec = pltpu.PrefetchScalarGridSpec(
        num_scalar_prefetch=1,
        grid=(nb,),
        in_specs=[
            pl.BlockSpec((1, 1, GATHER_ROWS), lambda i, f: (i, 0, 0), memory_space=pltpu.SMEM),
            pl.BlockSpec((n8, SUBLANES, w), lambda i, f: (i, 0, 0)),
        ],
        out_specs=pl.BlockSpec(memory_space=pl.ANY),
        scratch_shapes=[pltpu.SemaphoreType.DMA((2,))],
    )
    out = pl.pallas_call(
        _dispatch_body,
        grid_spec=grid_spec,
        out_shape=jax.ShapeDtypeStruct((n_rows // SUBLANES, SUBLANES, w), src.dtype),
        compiler_params=_params(("arbitrary",)),
        name="dispatch",
    )(fill, dest.reshape(nb, 1, GATHER_ROWS), src.reshape(t // SUBLANES, SUBLANES, w))
    return out.reshape(n_rows, w)


def _expert_body(lo_ref, hi_ref, nvalid_ref, xs_ref, gf_ref, wg_a, wu_a, wd_a, wg_b, wu_b, wd_b, ys_ref):
    i = pl.program_id(0)

    @pl.when(i < nvalid_ref[0])
    def _():
        xb = xs_ref[...]
        h = _rms(xb[:, 0:D_MODEL], gf_ref[...]).astype(BF16)

        gate_a, up_a = _dot(h, wg_a[...]), _dot(h, wu_a[...])
        gate_b, up_b = _dot(h, wg_b[...]), _dot(h, wu_b[...])
        hid_a = (jax.nn.silu(gate_a) * up_a).astype(BF16)
        hid_b = (jax.nn.silu(gate_b) * up_b).astype(BF16)
        y_lo = xb[:, D_MODEL:D_MODEL + 1] * _dot(hid_a, wd_a[...])
        ys_ref[...] = y_lo + xb[:, D_MODEL + 1:D_MODEL + 2] * _dot(hid_b, wd_b[...])

    @pl.when(i >= nvalid_ref[0])
    def _():
        ys_ref[...] = jnp.zeros_like(ys_ref)


def _experts(xs, g_ffn, blk_lo, blk_hi, nvalid, w_g, w_u, w_d):
    n_rows = xs.shape[0]
    nb = n_rows // MOE_ROWS
    d, de = w_g.shape[1], w_g.shape[2]
    up = lambda sel: pl.BlockSpec((None, d, de), lambda i, lo, hi, nv: (sel(lo, hi)[i], 0, 0))
    down = lambda sel: pl.BlockSpec((None, de, d), lambda i, lo, hi, nv: (sel(lo, hi)[i], 0, 0))
    first = lambda lo, hi: lo
    second = lambda lo, hi: hi
    grid_spec = pltpu.PrefetchScalarGridSpec(
        num_scalar_prefetch=3,
        grid=(nb,),
        in_specs=[
            pl.BlockSpec((MOE_ROWS, ROW_W), lambda i, lo, hi, nv: (jnp.minimum(i, nv[0] - 1), 0)),
            pl.BlockSpec((1, d), lambda i, lo, hi, nv: (0, 0)),
            up(first), up(first), down(first), up(second), up(second), down(second),
        ],
        out_specs=pl.BlockSpec((MOE_ROWS, d), lambda i, lo, hi, nv: (i, 0)),
    )
    return pl.pallas_call(
        _expert_body,
        grid_spec=grid_spec,
        out_shape=jax.ShapeDtypeStruct((n_rows, d), F32),
        compiler_params=_params(("arbitrary",)),
        name="experts",
    )(blk_lo, blk_hi, nvalid, xs, g_ffn.reshape(1, d), w_g, w_u, w_d, w_g, w_u, w_d)


def _combine_body(cur_ref, nxt_ref, x2_ref, ys_ref, g_ref, o_ref, ybuf, sem):
    n8 = ROW_TILE // SUBLANES
    i = pl.program_id(0)
    last = pl.num_programs(0) - 1
    slot = i % 2

    def issue_tile(idx_ref, s):
        def issue(r8, carry):
            for u in range(SUBLANES):
                d = idx_ref[0, 0, r8 * SUBLANES + u]
                src = ys_ref.at[lax.shift_right_logical(d, 3), pl.ds(d & (SUBLANES - 1), 1)]
                pltpu.make_async_copy(src, ybuf.at[s, r8, pl.ds(u, 1)], sem.at[s]).start(priority=u % 2)
            return carry

        lax.fori_loop(0, n8, issue, 0)

    @pl.when(i == 0)
    def _():
        issue_tile(cur_ref, 0)

    @pl.when(i < last)
    def _():
        issue_tile(nxt_ref, 1 - slot)

    pltpu.make_async_copy(ys_ref.at[pl.ds(0, n8)], ybuf.at[slot], sem.at[slot]).wait()
    y = ybuf[slot].reshape(ROW_TILE, D_MODEL)
    o_ref[...] = _rms(x2_ref[...] + y, g_ref[...])


def _combine(x2e, ys, dest, g_final):
    t = x2e.shape[0]
    d = D_MODEL
    tm = ROW_TILE
    nt = t // tm
    idx = dest.reshape(nt, 1, tm)
    return pl.pallas_call(
        _combine_body,
        grid=(nt,),
        in_specs=[
            pl.BlockSpec((1, 1, tm), lambda i: (i, 0, 0), memory_space=pltpu.SMEM),
            pl.BlockSpec((1, 1, tm), lambda i: (jnp.minimum(i + 1, nt - 1), 0, 0), memory_space=pltpu.SMEM),
            pl.BlockSpec((tm, d), lambda i: (i, 0)),
            pl.BlockSpec(memory_space=pl.ANY),
            pl.BlockSpec((1, d), lambda i: (0, 0)),
        ],
        out_specs=pl.BlockSpec((tm, d), lambda i: (i, 0)),
        out_shape=jax.ShapeDtypeStruct((t, d), F32),
        scratch_shapes=[pltpu.VMEM((2, tm // SUBLANES, SUBLANES, d), F32), pltpu.SemaphoreType.DMA((2,))],
        compiler_params=_params(("arbitrary",)),
        name="combine",
    )(idx, idx, x2e, ys.reshape(ys.shape[0] // SUBLANES, SUBLANES, d), g_final.reshape(1, d))


def _dispatch_plan(cls, t):
    onehot = (cls[:, None] == jnp.arange(N_CLASSES, dtype=jnp.int32)[None, :]).astype(jnp.int32)
    ranks = jnp.cumsum(onehot, axis=0) - onehot
    counts = jnp.sum(onehot, axis=0)
    padded = ((counts + MOE_ROWS - 1) // MOE_ROWS) * MOE_ROWS
    pad_end = jnp.cumsum(padded)
    pad_start = pad_end - padded
    dest = jnp.sum(onehot * (ranks + pad_start[None, :]), axis=1).astype(jnp.int32)
    fill_lo = pad_start + counts
    fill_mid = jnp.minimum(((fill_lo + SUBLANES - 1) // SUBLANES) * SUBLANES, pad_end)
    nvalid = (pad_end[-1] // MOE_ROWS).astype(jnp.int32).reshape(1)
    fill = jnp.concatenate([jnp.stack([fill_lo, fill_mid, pad_end], axis=1).reshape(-1), nvalid]).astype(jnp.int32)
    n_blocks = t // MOE_ROWS + N_CLASSES
    blk_start = jnp.arange(n_blocks, dtype=jnp.int32) * MOE_ROWS
    blk_cls = jnp.minimum(jnp.sum((blk_start[:, None] >= pad_end[None, :]).astype(jnp.int32), axis=1), N_CLASSES - 1)
    grp = blk_cls // N_PAIRS
    pair_onehot = ((blk_cls % N_PAIRS)[:, None] == jnp.arange(N_PAIRS, dtype=jnp.int32)[None, :]).astype(jnp.int32)
    blk_lo = grp * EXPERTS_PER_GROUP + jnp.sum(pair_onehot * jnp.asarray(PAIR_LO, jnp.int32)[None, :], axis=1)
    blk_hi = grp * EXPERTS_PER_GROUP + jnp.sum(pair_onehot * jnp.asarray(PAIR_HI, jnp.int32)[None, :], axis=1)
    return dest, fill, blk_lo.astype(jnp.int32), blk_hi.astype(jnp.int32), nvalid, n_blocks * MOE_ROWS


def _regroup_w_in(w_in):
    splits = np.cumsum([GLA_QK_W, GLA_QK_W, GLA_V_W, GLA_RANK, GLA_V_W, DIFF_QK_W, DIFF_QK_W, DIFF_V_W])[:-1]
    q_g, k_g, v_g, a_lr, r_g, q_d, k_d, v_d = jnp.split(w_in, [int(i) for i in splits], axis=1)
    a_pad = jnp.pad(a_lr, ((0, 0), (0, A_PAD - GLA_RANK)))
    q_d = q_d * (DIFF_DQK ** -0.5 * math.log2(math.e))
    return jnp.concatenate([q_g, k_g, v_g, r_g, a_pad, q_d, k_d], axis=1).astype(BF16), v_d.T.astype(BF16)


def kernel(x, mem, norm_mix_g, w_in, w_gla_a2, b_gla_a, gla_norm_g, diff_norm_g, lambda_q1, lambda_k1, lambda_q2,
           lambda_k2, w_out, norm_cross_g, norm_mem_g, w_cq, w_ckv, w_co, norm_ffn_g, w_router_grp, b_router_grp,
           w_router_exp, b_router_exp, w_e_gate, w_e_up, w_e_down, norm_final_g):
    b_, s_, d_ = x.shape
    m_ = mem.shape[1]
    t = b_ * s_
    xf = x.reshape(t, d_)

    kmem, vmem = _norm_matmul(mem.reshape(b_ * m_, d_), norm_mem_g[0], w_ckv[0].astype(BF16), (d_, d_), "mem_kv")

    w_cols, w_vt = _regroup_w_in(w_in[0])
    gla_in, dqk, v_t = _in_proj(xf, norm_mix_g[0], w_cols, w_vt)

    w_a2p = jnp.pad(w_gla_a2[0], ((0, A_PAD - GLA_RANK), (0, 0))).astype(BF16)
    o_g = _gla(gla_in, w_a2p, b_gla_a[0], gla_norm_g[0], b_, s_)

    lam_params = jnp.stack([lambda_q1[0], lambda_k1[0], lambda_q2[0], lambda_k2[0]]).astype(F32)
    o_d = _diff_attn(dqk, v_t, lam_params, diff_norm_g[0], b_, s_)

    pad_r = LANES - N_EXPERTS - N_GROUPS
    w_r = jnp.pad(jnp.concatenate([w_router_exp[0], w_router_grp[0]], axis=1).T, ((0, pad_r), (0, 0))).astype(BF16)
    b_r = jnp.pad(jnp.concatenate([b_router_exp[0], b_router_grp[0]]), (0, pad_r)).reshape(LANES, 1)
    x2e = _post_mix(xf, o_g, o_d, w_out[0].astype(BF16), norm_cross_g[0], w_cq[0].astype(BF16), kmem, vmem,
                    w_co[0].astype(BF16), norm_ffn_g[0], w_r, b_r, s_, m_)

    cls = x2e[:, D_MODEL + 2].astype(jnp.int32)
    dest, fill, blk_lo, blk_hi, nvalid, n_rows = _dispatch_plan(cls, t)
    xs = _dispatch(x2e, dest, fill, n_rows)
    ys = _experts(xs, norm_ffn_g[0], blk_lo, blk_hi, nvalid, w_e_gate[0].astype(BF16), w_e_up[0].astype(BF16),
                  w_e_down[0].astype(BF16))

    out = _combine(x2e, ys, dest, norm_final_g)
    return out.reshape(b_, s_, d_)
```

```python
import math

import jax
import jax.numpy as jnp
import numpy as np
from jax import lax
from jax.experimental import pallas as pl
from jax.experimental.pallas import tpu as pltpu

F32 = jnp.float32
BF16 = jnp.bfloat16

EPS = 1e-6
D_MODEL = 1024
GLA_HEADS = 4
GLA_DK = 64
GLA_DV = 128
GLA_RANK = 16
GLA_TAU = 16.0
GLA_CHUNK = 64
GLA_QK_W = GLA_HEADS * GLA_DK
GLA_V_W = GLA_HEADS * GLA_DV
DIFF_HEADS = 4
DIFF_DQK = 64
DIFF_DV = 128
DIFF_QK_W = DIFF_HEADS * 2 * DIFF_DQK
DIFF_V_W = DIFF_HEADS * DIFF_DV
CROSS_HEADS = 4
CROSS_DH = D_MODEL // CROSS_HEADS
N_GROUPS = 4
EXPERTS_PER_GROUP = 4
N_EXPERTS = N_GROUPS * EXPERTS_PER_GROUP
LAMBDA_INIT = 0.8 - 0.6 * math.exp(-0.3 * 0)

LANES = 128
SUBLANES = 8
A_PAD = LANES
GLA_IN_W = 2 * GLA_QK_W + 2 * GLA_V_W + A_PAD
ROW_W = D_MODEL + LANES
PAIR_LO = (0, 0, 0, 1, 1, 2)
PAIR_HI = (1, 2, 3, 2, 3, 3)
N_PAIRS = len(PAIR_LO)
N_CLASSES = N_GROUPS * N_PAIRS

ROW_TILE = 512
MIX_TILE = 1024
SUB_TILE = 512
GLA_ROWS = 2048
ATT_TILE = 512
ATT_ONES = 16
MOE_ROWS = 512
DISPATCH_ROWS = 2048
VMEM_LIMIT = 56 * 1024 * 1024


def _params(sem):
    return pltpu.CompilerParams(dimension_semantics=sem, vmem_limit_bytes=VMEM_LIMIT)


def _rms(x, g):
    ms = jnp.mean(x * x, axis=-1, keepdims=True)
    return x * lax.rsqrt(ms + EPS) * g


def _dot(a, b):
    return jnp.dot(a, b, preferred_element_type=F32)


def _dot_nt(a, b):
    return lax.dot_general(a, b, (((1,), (1,)), ((), ())), preferred_element_type=F32)


def _norm_matmul_body(x_ref, g_ref, w_ref, *o_refs):
    h = _rms(x_ref[...], g_ref[...]).astype(BF16)
    off = 0
    for o_ref in o_refs:
        n = o_ref.shape[-1]
        o_ref[...] = _dot(h, w_ref[:, off:off + n]).astype(o_ref.dtype)
        off += n


def _norm_matmul(x, g, w, splits, name):
    t, d = x.shape
    n = w.shape[1]
    assert sum(splits) == n and t % ROW_TILE == 0
    return pl.pallas_call(
        _norm_matmul_body,
        grid=(t // ROW_TILE,),
        in_specs=[
            pl.BlockSpec((ROW_TILE, d), lambda i: (i, 0)),
            pl.BlockSpec((1, d), lambda i: (0, 0)),
            pl.BlockSpec((d, n), lambda i: (0, 0)),
        ],
        out_specs=[pl.BlockSpec((ROW_TILE, s), lambda i: (i, 0)) for s in splits],
        out_shape=[jax.ShapeDtypeStruct((t, s), BF16) for s in splits],
        compiler_params=_params(("parallel",)),
        name=name,
    )(x, g.reshape(1, d), w)


def _in_proj_body(x_ref, g_ref, w_ref, wvt_ref, gla_ref, dqk_ref, vt_ref):
    h = _rms(x_ref[...], g_ref[...]).astype(BF16)
    gla_ref[...] = _dot(h, w_ref[:, 0:GLA_IN_W]).astype(gla_ref.dtype)
    dqk_ref[...] = _dot(h, w_ref[:, GLA_IN_W:]).astype(dqk_ref.dtype)
    vt_ref[...] = _dot_nt(wvt_ref[...], h).astype(vt_ref.dtype)


def _in_proj(x, g, w, w_vt):
    t, d = x.shape
    n = w.shape[1]
    tm = ROW_TILE
    assert n == GLA_IN_W + 2 * DIFF_QK_W and t % tm == 0
    return pl.pallas_call(
        _in_proj_body,
        grid=(t // tm,),
        in_specs=[
            pl.BlockSpec((tm, d), lambda i: (i, 0)),
            pl.BlockSpec((1, d), lambda i: (0, 0)),
            pl.BlockSpec((d, n), lambda i: (0, 0)),
            pl.BlockSpec((DIFF_V_W, d), lambda i: (0, 0)),
        ],
        out_specs=[
            pl.BlockSpec((tm, GLA_IN_W), lambda i: (i, 0)),
            pl.BlockSpec((tm, 2 * DIFF_QK_W), lambda i: (i, 0)),
            pl.BlockSpec((DIFF_V_W, tm), lambda i: (0, i)),
        ],
        out_shape=[
            jax.ShapeDtypeStruct((t, GLA_IN_W), BF16),
            jax.ShapeDtypeStruct((t, 2 * DIFF_QK_W), BF16),
            jax.ShapeDtypeStruct((DIFF_V_W, t), BF16),
        ],
        compiler_params=_params(("parallel",)),
        name="in_proj",
    )(x, g.reshape(1, d), w, w_vt)


def _chunk_prefix_operator():
    i = np.arange(2 * GLA_CHUNK)
    same_chunk = (i[:, None] // GLA_CHUNK) == (i[None, :] // GLA_CHUNK)
    return jnp.asarray(same_chunk & (i[None, :] <= i[:, None]), dtype=BF16)


def _gla_body(in_ref, tri_ref, wa_ref, ba_ref, gn_ref, o_ref, s_ref):
    c_ = GLA_CHUNK

    @pl.when(pl.program_id(1) == 0)
    def _():
        s_ref[...] = jnp.zeros_like(s_ref)

    q0, k0, v0, r0, a0 = 0, GLA_QK_W, 2 * GLA_QK_W, 2 * GLA_QK_W + GLA_V_W, 2 * GLA_QK_W + 2 * GLA_V_W
    g_ = 2 * c_
    n_g = GLA_ROWS // g_
    tri = tri_ref[...]
    causal = (lax.broadcasted_iota(jnp.int32, (c_, c_), 0) >= lax.broadcasted_iota(jnp.int32, (c_, c_), 1))
    first_half = lax.broadcasted_iota(jnp.int32, (g_, LANES), 1) < GLA_DK
    gn = gn_ref[...]

    def decays(g):
        rows = slice(g * g_, (g + 1) * g_)
        z = _dot(in_ref[rows, a0:a0 + A_PAD], wa_ref[...]) + ba_ref[...]
        log_a = jax.nn.log_sigmoid(z) / GLA_TAU
        la_hi = log_a.astype(BF16)
        la_lo = (log_a - la_hi.astype(F32)).astype(BF16)
        cum = _dot(tri, la_hi) + _dot(tri, la_lo)
        q = in_ref[rows, q0:q0 + GLA_QK_W].astype(F32)
        k = in_ref[rows, k0:k0 + GLA_QK_W].astype(F32)
        q_e = (q * (GLA_DK ** -0.5) * jnp.exp(cum)).astype(BF16)
        k_e = (k * jnp.exp(-cum)).astype(BF16)
        tots = [cum[(c + 1) * c_ - 1:(c + 1) * c_, :] for c in range(2)]
        k_s = jnp.concatenate([(k[c * c_:(c + 1) * c_] * jnp.exp(tots[c] - cum[c * c_:(c + 1) * c_])).astype(BF16)
                               for c in range(2)], axis=0)
        return q_e, k_e, k_s, tots

    def local_products(g, gates):
        q_e, k_e, k_s, _ = gates
        rows_g = slice(g * g_, (g + 1) * g_)
        out = []
        for h in range(GLA_HEADS):
            tile = slice((h // 2) * LANES, (h // 2 + 1) * LANES)
            head_lanes = first_half if h % 2 == 0 else ~first_half
            qe_m = jnp.where(head_lanes, q_e[:, tile], jnp.zeros((), BF16))
            ke_t = k_e[:, tile]
            ks_t = k_s[:, tile]
            v_h = in_ref[rows_g, v0 + h * GLA_DV:v0 + (h + 1) * GLA_DV]
            vt_h = v_h.astype(F32).T.astype(BF16)
            per_chunk = []
            for c in range(2):
                rows = slice(c * c_, (c + 1) * c_)
                att = jnp.where(causal, _dot_nt(qe_m[rows], ke_t[rows]), 0.0).astype(BF16)
                chunk_lanes = first_half if c == 0 else ~first_half
                kv_t = _dot(jnp.where(chunk_lanes, vt_h, jnp.zeros((), BF16)), ks_t)
                per_chunk.append((_dot(att, v_h[rows]), kv_t, qe_m[rows]))
            out.append(per_chunk)
        return out

    gates = [decays(0)]
    local = []
    for g in range(n_g):
        if g + 1 < n_g:
            gates.append(decays(g + 1))
        local.append(local_products(g, gates[g]))

    for h in range(GLA_HEADS):
        tile = slice((h // 2) * LANES, (h // 2 + 1) * LANES)
        st = s_ref[h]
        s_prev = []
        for g in range(n_g):
            for c in range(2):
                s_prev.append(st.astype(BF16))
                st = st * jnp.exp(gates[g][3][c][:, tile]) + local[g][h][c][1]
        s_ref[h] = st
        for g in range(n_g):
            r_h = in_ref[g * g_:(g + 1) * g_, r0 + h * GLA_DV:r0 + (h + 1) * GLA_DV].astype(F32)
            for c in range(2):
                o_intra, _, qe_c = local[g][h][c]
                o_h = o_intra + _dot_nt(qe_c, s_prev[2 * g + c])
                out_rows = slice(g * g_ + c * c_, g * g_ + (c + 1) * c_)
                o_ref[out_rows, h * GLA_DV:(h + 1) * GLA_DV] = (
                    _rms(o_h, gn) * jax.nn.silu(r_h[c * c_:(c + 1) * c_])).astype(o_ref.dtype)


def _gla(gla_in, w_a2p, b_a, gn, batch, seq):
    t = gla_in.shape[0]
    assert seq % GLA_ROWS == 0 and GLA_ROWS % (2 * GLA_CHUNK) == 0 and 2 * GLA_DK == LANES and 2 * GLA_CHUNK == LANES
    nsb = seq // GLA_ROWS
    return pl.pallas_call(
        _gla_body,
        grid=(batch, nsb),
        in_specs=[
            pl.BlockSpec((GLA_ROWS, GLA_IN_W), lambda b, j: (b * nsb + j, 0)),
            pl.BlockSpec((2 * GLA_CHUNK, 2 * GLA_CHUNK), lambda b, j: (0, 0)),
            pl.BlockSpec((A_PAD, GLA_QK_W), lambda b, j: (0, 0)),
            pl.BlockSpec((1, GLA_QK_W), lambda b, j: (0, 0)),
            pl.BlockSpec((1, GLA_DV), lambda b, j: (0, 0)),
        ],
        out_specs=pl.BlockSpec((GLA_ROWS, GLA_V_W), lambda b, j: (b * nsb + j, 0)),
        out_shape=jax.ShapeDtypeStruct((t, GLA_V_W), BF16),
        scratch_shapes=[pltpu.VMEM((GLA_HEADS, GLA_DV, LANES), F32)],
        compiler_params=_params(("parallel", "arbitrary")),
        name="gla",
    )(gla_in, _chunk_prefix_operator(), w_a2p, b_a.reshape(1, -1), gn.reshape(1, -1))


def _alibi_features(seq):
    j = np.arange(seq) % ATT_TILE
    j_lo = j % 256
    j_hi = j - j_lo
    kf = np.zeros((seq, 2 * DIFF_DQK), np.float32)
    for base in (0, DIFF_DQK):
        for f in range(3):
            kf[:, base + f] = j_lo
            kf[:, base + 3 + f] = j_hi
    slopes = 2.0 ** (-8.0 * np.arange(1, DIFF_HEADS + 1) / DIFF_HEADS)
    c = jnp.asarray(slopes * math.log2(math.e), dtype=F32)
    c1 = c.astype(BF16)
    c2 = (c - c1.astype(F32)).astype(BF16)
    c3 = (c - c1.astype(F32) - c2.astype(F32)).astype(BF16)
    terms = jnp.stack([c1, c2, c3, c1, c2, c3], axis=1)
    qf = jnp.zeros((DIFF_HEADS, 2 * DIFF_DQK), BF16)
    qf = qf.at[:, 0:6].set(terms).at[:, DIFF_DQK:DIFF_DQK + 6].set(terms)
    qf = jnp.repeat(qf, 8, axis=0)
    return jnp.asarray(kf, dtype=BF16), qf, c


def _diff_body(c_ref, q_ref, k_ref, vt_ref, qf_ref, kf_ref, lam_ref, gn_ref, o_ref, kaug_ref, vta_ref):
    tq = ATT_TILE
    h = pl.program_id(1)
    seq = k_ref.shape[0]
    nq = seq // tq
    c = c_ref[h]

    k = k_ref[...]
    kf = kf_ref[...]
    lane_k = lax.broadcasted_iota(jnp.int32, k.shape, 1)
    kaug_ref[0] = jnp.where(lane_k < DIFF_DQK, k, kf)
    kaug_ref[1] = jnp.where(lane_k >= DIFF_DQK, k, kf)
    for jj in range(nq):
        vta_ref[jj, 0:DIFF_DV, :] = vt_ref[:, jj * tq:(jj + 1) * tq]
        vta_ref[jj, DIFF_DV:, :] = jnp.ones((ATT_ONES, tq), BF16)

    lq1, lk1, lq2, lk2 = (lam_ref[i:i + 1, :] for i in range(4))
    lam = (jnp.exp(jnp.sum(lq1 * lk1, axis=-1, keepdims=True))
           - jnp.exp(jnp.sum(lq2 * lk2, axis=-1, keepdims=True)) + LAMBDA_INIT)
    row = lax.broadcasted_iota(jnp.int32, (tq, tq), 0)
    col = lax.broadcasted_iota(jnp.int32, (tq, tq), 1)
    visible = row <= col
    lane = lax.broadcasted_iota(jnp.int32, (tq, 2 * DIFF_DQK), 1)
    qf = jnp.broadcast_to(qf_ref[0:1, :], (tq, 2 * DIFF_DQK))

    steps = [(qi, j) for qi in range(nq) for j in range(qi + 1)]
    q_maps = {}

    def scores(qi, j):
        if qi not in q_maps:
            q = q_ref[qi * tq:(qi + 1) * tq, :]
            q_maps[qi] = (jnp.where(lane < DIFF_DQK, q, qf), jnp.where(lane >= DIFF_DQK, q, qf))
        return [_dot_nt(kaug_ref[m, j * tq:(j + 1) * tq, :], q_maps[qi][m]) for m in range(2)]

    m_run = [None, None]
    acc = [None, None]
    s_next = scores(*steps[0])
    for t, (qi, j) in enumerate(steps):
        s_cur = s_next
        if t + 1 < len(steps):
            s_next = scores(*steps[t + 1])
        vblk = vta_ref[j]
        base = c * float(j * tq)
        for m in range(2):
            s_t = s_cur[m]
            if j == qi:
                s_t = jnp.where(visible, s_t, -jnp.inf)
            blk_max = jnp.max(s_t, axis=0, keepdims=True) + base
            if j == 0:
                m_new = blk_max
                acc[m] = _dot(vblk, jnp.exp2(s_t - (m_new - base)).astype(BF16))
            else:
                m_new = jnp.maximum(m_run[m], blk_max)
                alpha = jnp.exp2(m_run[m] - m_new)
                acc[m] = alpha * acc[m] + _dot(vblk, jnp.exp2(s_t - (m_new - base)).astype(BF16))
            m_run[m] = m_new
        if j == qi:
            o_t = (acc[0][0:DIFF_DV] / acc[0][DIFF_DV:DIFF_DV + 1]
                   - lam * (acc[1][0:DIFF_DV] / acc[1][DIFF_DV:DIFF_DV + 1]))
            o = o_t.T
            o_ref[qi * tq:(qi + 1) * tq, :] = (_rms(o, gn_ref[...]) * (1.0 - LAMBDA_INIT)).astype(o_ref.dtype)


def _diff_attn(dqk, v_t, lam_params, gn, batch, seq):
    t = dqk.shape[0]
    tq = ATT_TILE
    assert seq % tq == 0
    nq = seq // tq
    kfeat, qfeat, c = _alibi_features(seq)
    rows = DIFF_DV + ATT_ONES
    grid_spec = pltpu.PrefetchScalarGridSpec(
        num_scalar_prefetch=1,
        grid=(batch, DIFF_HEADS),
        in_specs=[
            pl.BlockSpec((seq, 2 * DIFF_DQK), lambda b, h, s: (b, h)),
            pl.BlockSpec((seq, 2 * DIFF_DQK), lambda b, h, s: (b, DIFF_HEADS + h)),
            pl.BlockSpec((DIFF_DV, seq), lambda b, h, s: (h, b)),
            pl.BlockSpec((8, 2 * DIFF_DQK), lambda b, h, s: (h, 0)),
            pl.BlockSpec((seq, 2 * DIFF_DQK), lambda b, h, s: (0, 0)),
            pl.BlockSpec((4, DIFF_DQK), lambda b, h, s: (0, 0)),
            pl.BlockSpec((1, DIFF_DV), lambda b, h, s: (0, 0)),
        ],
        out_specs=pl.BlockSpec((seq, DIFF_DV), lambda b, h, s: (b, h)),
        scratch_shapes=[
            pltpu.VMEM((2, seq, 2 * DIFF_DQK), BF16),
            pltpu.VMEM((nq, rows, tq), BF16),
        ],
    )
    return pl.pallas_call(
        _diff_body,
        grid_spec=grid_spec,
        out_shape=jax.ShapeDtypeStruct((t, DIFF_V_W), BF16),
        compiler_params=_params(("parallel", "parallel")),
        name="diff_attn",
    )(c, dqk, dqk, v_t, qfeat, kfeat, lam_params, gn.reshape(1, -1))


def _route(logits):
    lane = lax.broadcasted_iota(jnp.int32, logits.shape, 0)
    big = jnp.int32(LANES)
    neg = -jnp.inf

    def first_argmax(vals, vmax):
        return jnp.min(jnp.where(vals == vmax, lane, big), axis=0, keepdims=True)

    is_grp = (lane >= N_EXPERTS) & (lane < N_EXPERTS + N_GROUPS)
    lg = jnp.where(is_grp, logits, neg)
    mg = jnp.max(lg, axis=0, keepdims=True)
    p_g = 1.0 / jnp.sum(jnp.exp(lg - mg), axis=0, keepdims=True)
    g_sel = first_argmax(lg, mg) - N_EXPERTS
    in_grp = (lane >= g_sel * EXPERTS_PER_GROUP) & (lane < (g_sel + 1) * EXPERTS_PER_GROUP)
    le = jnp.where(in_grp, logits, neg)
    m1 = jnp.max(le, axis=0, keepdims=True)
    i1 = first_argmax(le, m1)
    le2 = jnp.where(lane == i1, neg, le)
    m2 = jnp.max(le2, axis=0, keepdims=True)
    i2 = first_argmax(le2, m2)
    den = jnp.sum(jnp.exp(le - m1), axis=0, keepdims=True)
    w1 = 1.0 / den
    w2 = jnp.exp(m2 - m1) / den
    wsum = w1 + w2
    g1 = p_g * w1 / wsum
    g2 = p_g * w2 / wsum
    first_is_lo = i1 < i2
    g_lo = jnp.where(first_is_lo, g1, g2)
    g_hi = jnp.where(first_is_lo, g2, g1)
    a = jnp.minimum(i1, i2) - g_sel * EXPERTS_PER_GROUP
    b = jnp.maximum(i1, i2) - g_sel * EXPERTS_PER_GROUP
    pair = 3 * a - jnp.where(a == 2, 1, 0) + (b - a - 1)
    cls = (g_sel * N_PAIRS + pair).astype(F32)
    out = jnp.where(lane == 0, g_lo, jnp.where(lane == 1, g_hi, cls))
    return jnp.where(lane < 3, out, 0.0).T


def _post_mix_body(x_ref, og_ref, od_ref, wo_ref, gc_ref, wq_ref, km_ref, vm_ref, wco_ref, gf_ref, wr_ref, br_ref,
                   x2e_ref):
    tiles = [slice(j * SUB_TILE, (j + 1) * SUB_TILE) for j in range(MIX_TILE // SUB_TILE)]

    def out_proj(rows):
        return x_ref[rows] + _dot(jnp.concatenate([og_ref[rows], od_ref[rows]], axis=1), wo_ref[...])

    def cross_query(x1):
        return _dot(_rms(x1, gc_ref[...]).astype(BF16), wq_ref[...]).astype(BF16)

    def cross_attend(qc):
        heads = []
        for h in range(CROSS_HEADS):
            sl = slice(h * CROSS_DH, (h + 1) * CROSS_DH)
            s = _dot_nt(qc[:, sl], km_ref[:, sl]) * (CROSS_DH ** -0.5)
            s = s - jnp.max(s, axis=-1, keepdims=True)
            p = jnp.exp(s)
            p = p / jnp.sum(p, axis=-1, keepdims=True)
            heads.append(_dot(p.astype(BF16), vm_ref[:, sl]).astype(BF16))
        return jnp.concatenate(heads, axis=-1)

    def finish(rows, x2):
        h3 = _rms(x2, gf_ref[...]).astype(BF16)
        x2e_ref[rows, 0:D_MODEL] = x2
        x2e_ref[rows, D_MODEL:] = _route(_dot_nt(wr_ref[...], h3) + br_ref[...])

    x1 = [out_proj(rows) for rows in tiles]
    qc = [cross_query(v) for v in x1]
    oc = [cross_attend(v) for v in qc]
    x2 = [a + _dot(b, wco_ref[...]) for a, b in zip(x1, oc)]
    for rows, v in zip(tiles, x2):
        finish(rows, v)


def _post_mix(x, og, od, w_out, g_cross, w_cq, kmem, vmem, w_co, g_ffn, w_r, b_r, seq, mem_len):
    t, d = x.shape
    tm = MIX_TILE
    assert seq % tm == 0
    per_b = seq // tm
    full = lambda shape: pl.BlockSpec(shape, lambda i: (0, 0))
    rows = lambda w: pl.BlockSpec((tm, w), lambda i: (i, 0))
    return pl.pallas_call(
        _post_mix_body,
        grid=(t // tm,),
        in_specs=[
            rows(d), rows(GLA_V_W), rows(DIFF_V_W),
            full((d, d)), full((1, d)), full((d, d)),
            pl.BlockSpec((mem_len, d), lambda i: (i // per_b, 0)),
            pl.BlockSpec((mem_len, d), lambda i: (i // per_b, 0)),
            full((d, d)), full((1, d)), full((LANES, d)), full((LANES, 1)),
        ],
        out_specs=rows(ROW_W),
        out_shape=jax.ShapeDtypeStruct((t, ROW_W), F32),
        compiler_params=_params(("parallel",)),
        name="post_mix",
    )(x, og, od, w_out, g_cross.reshape(1, d), w_cq, kmem, vmem, w_co, g_ffn.reshape(1, d), w_r, b_r)


def _dispatch_body(fill_ref, dest_ref, src_ref, dst_ref, sem):
    n8 = DISPATCH_ROWS // SUBLANES
    i = pl.program_id(0)

    def dst_row(d):
        return dst_ref.at[lax.shift_right_logical(d, 3), pl.ds(d & (SUBLANES - 1), 1)]

    def issue(r8, carry):
        for u in range(SUBLANES):
            d = dest_ref[0, 0, r8 * SUBLANES + u]
            pltpu.make_async_copy(src_ref.at[r8, pl.ds(u, 1)], dst_row(d), sem.at[0]).start(priority=u % 2)
        return carry

    lax.fori_loop(0, n8, issue, 0)

    def fill_padding(start):
        def go(cp):
            cp.start() if start else cp.wait()

        def fill_tiles(t0, nt):
            go(pltpu.make_async_copy(src_ref.at[pl.ds(0, nt)], dst_ref.at[pl.ds(t0, nt)], sem.at[1]))

        def per_class(c, carry):
            lo = fill_ref[3 * c]
            mid = fill_ref[3 * c + 1]
            hi = fill_ref[3 * c + 2]
            lax.fori_loop(lo, mid, lambda r, cc: (go(pltpu.make_async_copy(src_ref.at[0, pl.ds(0, 1)], dst_row(r),
                                                                             sem.at[1])), cc)[1], 0)
            tiles = lax.shift_right_logical(hi - mid, 3)
            off = lax.shift_right_logical(mid, 3)
            for bit in reversed(range((MOE_ROWS // SUBLANES).bit_length() - 1)):
                has = (tiles & (1 << bit)) != 0

                @pl.when(has)
                def _(off=off, bit=bit):
                    fill_tiles(off, 1 << bit)

                off = off + jnp.where(has, 1 << bit, 0)
            return carry

        lax.fori_loop(0, N_CLASSES, per_class, 0)
        blk8 = MOE_ROWS // SUBLANES
        n_blocks = dst_ref.shape[0] // blk8
        lax.fori_loop(fill_ref[3 * N_CLASSES], n_blocks, lambda b, cc: (fill_tiles(b * blk8, blk8), cc)[1], 0)

    @pl.when(i == 0)
    def _():
        fill_padding(True)
        fill_padding(False)

    pltpu.make_async_copy(src_ref, dst_ref.at[pl.ds(0, n8)], sem.at[0]).wait()


def _dispatch(src, dest, fill, n_rows):
    t, w = src.shape
    assert t % DISPATCH_ROWS == 0 and DISPATCH_ROWS >= MOE_ROWS and SUBLANES == 8
    nb = t // DISPATCH_ROWS
    n8 = DISPATCH_ROWS // SUBLANES
    grid_spec = pltpu.PrefetchScalarGridSpec(
        num_scalar_prefetch=1,
        grid=(nb,),
        in_specs=[
            pl.BlockSpec((1, 1, DISPATCH_ROWS), lambda i, f: (i, 0, 0), memory_space=pltpu.SMEM),
            pl.BlockSpec((n8, SUBLANES, w), lambda i, f: (i, 0, 0)),
        ],
        out_specs=pl.BlockSpec(memory_space=pl.ANY),
        scratch_shapes=[pltpu.SemaphoreType.DMA((2,))],
    )
    out = pl.pallas_call(
        _dispatch_body,
        grid_spec=grid_spec,
        out_shape=jax.ShapeDtypeStruct((n_rows // SUBLANES, SUBLANES, w), src.dtype),
        compiler_params=_params(("arbitrary",)),
        name="dispatch",
    )(fill, dest.reshape(nb, 1, DISPATCH_ROWS), src.reshape(t // SUBLANES, SUBLANES, w))
    return out.reshape(n_rows, w)


def _expert_body(lo_ref, hi_ref, nvalid_ref, xs_ref, gf_ref, wg_a, wu_a, wd_a, wg_b, wu_b, wd_b, ys_ref):
    i = pl.program_id(0)

    @pl.when(i < nvalid_ref[0])
    def _():
        xb = xs_ref[...]
        h = _rms(xb[:, 0:D_MODEL], gf_ref[...]).astype(BF16)

        gate_a, up_a = _dot(h, wg_a[...]), _dot(h, wu_a[...])
        gate_b, up_b = _dot(h, wg_b[...]), _dot(h, wu_b[...])
        hid_a = (jax.nn.silu(gate_a) * up_a).astype(BF16)
        hid_b = (jax.nn.silu(gate_b) * up_b).astype(BF16)
        y_lo = xb[:, D_MODEL:D_MODEL + 1] * _dot(hid_a, wd_a[...])
        ys_ref[...] = y_lo + xb[:, D_MODEL + 1:D_MODEL + 2] * _dot(hid_b, wd_b[...])

    @pl.when(i >= nvalid_ref[0])
    def _():
        ys_ref[...] = jnp.zeros_like(ys_ref)


def _experts(xs, g_ffn, blk_lo, blk_hi, nvalid, w_g, w_u, w_d):
    n_rows = xs.shape[0]
    nb = n_rows // MOE_ROWS
    d, de = w_g.shape[1], w_g.shape[2]
    up = lambda sel: pl.BlockSpec((None, d, de), lambda i, lo, hi, nv: (sel(lo, hi)[i], 0, 0))
    down = lambda sel: pl.BlockSpec((None, de, d), lambda i, lo, hi, nv: (sel(lo, hi)[i], 0, 0))
    first = lambda lo, hi: lo
    second = lambda lo, hi: hi
    grid_spec = pltpu.PrefetchScalarGridSpec(
        num_scalar_prefetch=3,
        grid=(nb,),
        in_specs=[
            pl.BlockSpec((MOE_ROWS, ROW_W), lambda i, lo, hi, nv: (jnp.minimum(i, nv[0] - 1), 0)),
            pl.BlockSpec((1, d), lambda i, lo, hi, nv: (0, 0)),
            up(first), up(first), down(first), up(second), up(second), down(second),
        ],
        out_specs=pl.BlockSpec((MOE_ROWS, d), lambda i, lo, hi, nv: (i, 0)),
    )
    return pl.pallas_call(
        _expert_body,
        grid_spec=grid_spec,
        out_shape=jax.ShapeDtypeStruct((n_rows, d), F32),
        compiler_params=_params(("arbitrary",)),
        name="experts",
    )(blk_lo, blk_hi, nvalid, xs, g_ffn.reshape(1, d), w_g, w_u, w_d, w_g, w_u, w_d)


def _combine_body(cur_ref, nxt_ref, x2_ref, ys_ref, g_ref, o_ref, ybuf, sem):
    n8 = ROW_TILE // SUBLANES
    i = pl.program_id(0)
    last = pl.num_programs(0) - 1
    slot = i % 2

    def issue_tile(idx_ref, s):
        def issue(r8, carry):
            for u in range(SUBLANES):
                d = idx_ref[0, 0, r8 * SUBLANES + u]
                src = ys_ref.at[lax.shift_right_logical(d, 3), pl.ds(d & (SUBLANES - 1), 1)]
                pltpu.make_async_copy(src, ybuf.at[s, r8, pl.ds(u, 1)], sem.at[s]).start(priority=u % 2)
            return carry

        lax.fori_loop(0, n8, issue, 0)

    @pl.when(i == 0)
    def _():
        issue_tile(cur_ref, 0)

    @pl.when(i < last)
    def _():
        issue_tile(nxt_ref, 1 - slot)

    pltpu.make_async_copy(ys_ref.at[pl.ds(0, n8)], ybuf.at[slot], sem.at[slot]).wait()
    y = ybuf[slot].reshape(ROW_TILE, D_MODEL)
    o_ref[...] = _rms(x2_ref[...] + y, g_ref[...])


def _combine(x2e, ys, dest, g_final):
    t = x2e.shape[0]
    d = D_MODEL
    tm = ROW_TILE
    nt = t // tm
    idx = dest.reshape(nt, 1, tm)
    return pl.pallas_call(
        _combine_body,
        grid=(nt,),
        in_specs=[
            pl.BlockSpec((1, 1, tm), lambda i: (i, 0, 0), memory_space=pltpu.SMEM),
            pl.BlockSpec((1, 1, tm), lambda i: (jnp.minimum(i + 1, nt - 1), 0, 0), memory_space=pltpu.SMEM),
            pl.BlockSpec((tm, d), lambda i: (i, 0)),
            pl.BlockSpec(memory_space=pl.ANY),
            pl.BlockSpec((1, d), lambda i: (0, 0)),
        ],
        out_specs=pl.BlockSpec((tm, d), lambda i: (i, 0)),
        out_shape=jax.ShapeDtypeStruct((t, d), F32),
        scratch_shapes=[pltpu.VMEM((2, tm // SUBLANES, SUBLANES, d), F32), pltpu.SemaphoreType.DMA((2,))],
        compiler_params=_params(("arbitrary",)),
        name="combine",
    )(idx, idx, x2e, ys.reshape(ys.shape[0] // SUBLANES, SUBLANES, d), g_final.reshape(1, d))


def _dispatch_plan(cls, t):
    onehot = (cls[:, None] == jnp.arange(N_CLASSES, dtype=jnp.int32)[None, :]).astype(jnp.int32)
    ranks = jnp.cumsum(onehot, axis=0) - onehot
    counts = jnp.sum(onehot, axis=0)
    padded = ((counts + MOE_ROWS - 1) // MOE_ROWS) * MOE_ROWS
    pad_end = jnp.cumsum(padded)
    pad_start = pad_end - padded
    dest = jnp.sum(onehot * (ranks + pad_start[None, :]), axis=1).astype(jnp.int32)
    fill_lo = pad_start + counts
    fill_mid = jnp.minimum(((fill_lo + SUBLANES - 1) // SUBLANES) * SUBLANES, pad_end)
    nvalid = (pad_end[-1] // MOE_ROWS).astype(jnp.int32).reshape(1)
    fill = jnp.concatenate([jnp.stack([fill_lo, fill_mid, pad_end], axis=1).reshape(-1), nvalid]).astype(jnp.int32)
    n_blocks = t // MOE_ROWS + N_CLASSES
    blk_start = jnp.arange(n_blocks, dtype=jnp.int32) * MOE_ROWS
    blk_cls = jnp.minimum(jnp.sum((blk_start[:, None] >= pad_end[None, :]).astype(jnp.int32), axis=1), N_CLASSES - 1)
    grp = blk_cls // N_PAIRS
    pair_onehot = ((blk_cls % N_PAIRS)[:, None] == jnp.arange(N_PAIRS, dtype=jnp.int32)[None, :]).astype(jnp.int32)
    blk_lo = grp * EXPERTS_PER_GROUP + jnp.sum(pair_onehot * jnp.asarray(PAIR_LO, jnp.int32)[None, :], axis=1)
    blk_hi = grp * EXPERTS_PER_GROUP + jnp.sum(pair_onehot * jnp.asarray(PAIR_HI, jnp.int32)[None, :], axis=1)
    return dest, fill, blk_lo.astype(jnp.int32), blk_hi.astype(jnp.int32), nvalid, n_blocks * MOE_ROWS


def _regroup_w_in(w_in):
    splits = np.cumsum([GLA_QK_W, GLA_QK_W, GLA_V_W, GLA_RANK, GLA_V_W, DIFF_QK_W, DIFF_QK_W, DIFF_V_W])[:-1]
    q_g, k_g, v_g, a_lr, r_g, q_d, k_d, v_d = jnp.split(w_in, [int(i) for i in splits], axis=1)
    a_pad = jnp.pad(a_lr, ((0, 0), (0, A_PAD - GLA_RANK)))
    q_d = q_d * (DIFF_DQK ** -0.5 * math.log2(math.e))
    return jnp.concatenate([q_g, k_g, v_g, r_g, a_pad, q_d, k_d], axis=1).astype(BF16), v_d.T.astype(BF16)


def kernel(x, mem, norm_mix_g, w_in, w_gla_a2, b_gla_a, gla_norm_g, diff_norm_g, lambda_q1, lambda_k1, lambda_q2,
           lambda_k2, w_out, norm_cross_g, norm_mem_g, w_cq, w_ckv, w_co, norm_ffn_g, w_router_grp, b_router_grp,
           w_router_exp, b_router_exp, w_e_gate, w_e_up, w_e_down, norm_final_g):
    b_, s_, d_ = x.shape
    m_ = mem.shape[1]
    t = b_ * s_
    xf = x.reshape(t, d_)

    kmem, vmem = _norm_matmul(mem.reshape(b_ * m_, d_), norm_mem_g[0], w_ckv[0].astype(BF16), (d_, d_), "mem_kv")

    w_cols, w_vt = _regroup_w_in(w_in[0])
    gla_in, dqk, v_t = _in_proj(xf, norm_mix_g[0], w_cols, w_vt)

    w_a2p = jnp.pad(w_gla_a2[0], ((0, A_PAD - GLA_RANK), (0, 0))).astype(BF16)
    o_g = _gla(gla_in, w_a2p, b_gla_a[0], gla_norm_g[0], b_, s_)

    lam_params = jnp.stack([lambda_q1[0], lambda_k1[0], lambda_q2[0], lambda_k2[0]]).astype(F32)
    o_d = _diff_attn(dqk, v_t, lam_params, diff_norm_g[0], b_, s_)

    pad_r = LANES - N_EXPERTS - N_GROUPS
    w_r = jnp.pad(jnp.concatenate([w_router_exp[0], w_router_grp[0]], axis=1).T, ((0, pad_r), (0, 0))).astype(BF16)
    b_r = jnp.pad(jnp.concatenate([b_router_exp[0], b_router_grp[0]]), (0, pad_r)).reshape(LANES, 1)
    x2e = _post_mix(xf, o_g, o_d, w_out[0].astype(BF16), norm_cross_g[0], w_cq[0].astype(BF16), kmem, vmem,
                    w_co[0].astype(BF16), norm_ffn_g[0], w_r, b_r, s_, m_)

    cls = x2e[:, D_MODEL + 2].astype(jnp.int32)
    dest, fill, blk_lo, blk_hi, nvalid, n_rows = _dispatch_plan(cls, t)
    xs = _dispatch(x2e, dest, fill, n_rows)
    ys = _experts(xs, norm_ffn_g[0], blk_lo, blk_hi, nvalid, w_e_gate[0].astype(BF16), w_e_up[0].astype(BF16),
                  w_e_down[0].astype(BF16))

    out = _combine(x2e, ys, dest, norm_final_g)
    return out.reshape(b_, s_, d_)
```

```python
import math

import jax
import jax.numpy as jnp
import numpy as np
from jax import lax
from jax.experimental import pallas as pl
from jax.experimental.pallas import tpu as pltpu

F32 = jnp.float32
BF16 = jnp.bfloat16

EPS = 1e-6
D_MODEL = 1024
GLA_HEADS = 4
GLA_DK = 64
GLA_DV = 128
GLA_RANK = 16
GLA_TAU = 16.0
GLA_CHUNK = 64
GLA_QK_W = GLA_HEADS * GLA_DK
GLA_V_W = GLA_HEADS * GLA_DV
DIFF_HEADS = 4
DIFF_DQK = 64
DIFF_DV = 128
DIFF_QK_W = DIFF_HEADS * 2 * DIFF_DQK
DIFF_V_W = DIFF_HEADS * DIFF_DV
CROSS_HEADS = 4
CROSS_DH = D_MODEL // CROSS_HEADS
N_GROUPS = 4
EXPERTS_PER_GROUP = 4
N_EXPERTS = N_GROUPS * EXPERTS_PER_GROUP
LAMBDA_INIT = 0.8 - 0.6 * math.exp(-0.3 * 0)

LANES = 128
SUBLANES = 8
A_PAD = LANES
GLA_IN_W = 2 * GLA_QK_W + 2 * GLA_V_W + A_PAD
ROW_W = D_MODEL + LANES
PAIR_LO = (0, 0, 0, 1, 1, 2)
PAIR_HI = (1, 2, 3, 2, 3, 3)
N_PAIRS = len(PAIR_LO)
N_CLASSES = N_GROUPS * N_PAIRS

ROW_TILE = 512
MIX_TILE = 1024
SUB_TILE = 512
GLA_ROWS = 2048
ATT_TILE = 512
ATT_ONES = 16
MOE_ROWS = 512
DISPATCH_ROWS = 2048
VMEM_LIMIT = 56 * 1024 * 1024


def _params(sem):
    return pltpu.CompilerParams(dimension_semantics=sem, vmem_limit_bytes=VMEM_LIMIT)


def _rms(x, g):
    ms = jnp.mean(x * x, axis=-1, keepdims=True)
    return x * lax.rsqrt(ms + EPS) * g


def _dot(a, b):
    return jnp.dot(a, b, preferred_element_type=F32)


def _dot_nt(a, b):
    return lax.dot_general(a, b, (((1,), (1,)), ((), ())), preferred_element_type=F32)


def _norm_matmul_body(x_ref, g_ref, w_ref, *o_refs):
    h = _rms(x_ref[...], g_ref[...]).astype(BF16)
    off = 0
    for o_ref in o_refs:
        n = o_ref.shape[-1]
        o_ref[...] = _dot(h, w_ref[:, off:off + n]).astype(o_ref.dtype)
        off += n


def _norm_matmul(x, g, w, splits, name):
    t, d = x.shape
    n = w.shape[1]
    assert sum(splits) == n and t % ROW_TILE == 0
    return pl.pallas_call(
        _norm_matmul_body,
        grid=(t // ROW_TILE,),
        in_specs=[
            pl.BlockSpec((ROW_TILE, d), lambda i: (i, 0)),
            pl.BlockSpec((1, d), lambda i: (0, 0)),
            pl.BlockSpec((d, n), lambda i: (0, 0)),
        ],
        out_specs=[pl.BlockSpec((ROW_TILE, s), lambda i: (i, 0)) for s in splits],
        out_shape=[jax.ShapeDtypeStruct((t, s), BF16) for s in splits],
        compiler_params=_params(("parallel",)),
        name=name,
    )(x, g.reshape(1, d), w)


def _in_proj_body(x_ref, g_ref, w_ref, wvt_ref, gla_ref, dqk_ref, vt_ref):
    h = _rms(x_ref[...], g_ref[...]).astype(BF16)
    gla_ref[...] = _dot(h, w_ref[:, 0:GLA_IN_W]).astype(gla_ref.dtype)
    dqk_ref[...] = _dot(h, w_ref[:, GLA_IN_W:]).astype(dqk_ref.dtype)
    vt_ref[...] = _dot_nt(wvt_ref[...], h).astype(vt_ref.dtype)


def _in_proj(x, g, w, w_vt):
    t, d = x.shape
    n = w.shape[1]
    tm = ROW_TILE
    assert n == GLA_IN_W + 2 * DIFF_QK_W and t % tm == 0
    return pl.pallas_call(
        _in_proj_body,
        grid=(t // tm,),
        in_specs=[
            pl.BlockSpec((tm, d), lambda i: (i, 0)),
            pl.BlockSpec((1, d), lambda i: (0, 0)),
            pl.BlockSpec((d, n), lambda i: (0, 0)),
            pl.BlockSpec((DIFF_V_W, d), lambda i: (0, 0)),
        ],
        out_specs=[
            pl.BlockSpec((tm, GLA_IN_W), lambda i: (i, 0)),
            pl.BlockSpec((tm, 2 * DIFF_QK_W), lambda i: (i, 0)),
            pl.BlockSpec((DIFF_V_W, tm), lambda i: (0, i)),
        ],
        out_shape=[
            jax.ShapeDtypeStruct((t, GLA_IN_W), BF16),
            jax.ShapeDtypeStruct((t, 2 * DIFF_QK_W), BF16),
            jax.ShapeDtypeStruct((DIFF_V_W, t), BF16),
        ],
        compiler_params=_params(("parallel",)),
        name="in_proj",
    )(x, g.reshape(1, d), w, w_vt)


def _chunk_prefix_operator():
    i = np.arange(2 * GLA_CHUNK)
    same_chunk = (i[:, None] // GLA_CHUNK) == (i[None, :] // GLA_CHUNK)
    return jnp.asarray(same_chunk & (i[None, :] <= i[:, None]), dtype=BF16)


def _gla_body(in_ref, tri_ref, wa_ref, ba_ref, gn_ref, o_ref, s_ref):
    c_ = GLA_CHUNK

    @pl.when(pl.program_id(1) == 0)
    def _():
        s_ref[...] = jnp.zeros_like(s_ref)

    q0, k0, v0, r0, a0 = 0, GLA_QK_W, 2 * GLA_QK_W, 2 * GLA_QK_W + GLA_V_W, 2 * GLA_QK_W + 2 * GLA_V_W
    g_ = 2 * c_
    n_g = GLA_ROWS // g_
    tri = tri_ref[...]
    causal = (lax.broadcasted_iota(jnp.int32, (c_, c_), 0) >= lax.broadcasted_iota(jnp.int32, (c_, c_), 1))
    first_half = lax.broadcasted_iota(jnp.int32, (g_, LANES), 1) < GLA_DK
    gn = gn_ref[...]

    def decays(g):
        rows = slice(g * g_, (g + 1) * g_)
        z = _dot(in_ref[rows, a0:a0 + A_PAD], wa_ref[...]) + ba_ref[...]
        log_a = jax.nn.log_sigmoid(z) / GLA_TAU
        la_hi = log_a.astype(BF16)
        la_lo = (log_a - la_hi.astype(F32)).astype(BF16)
        cum = _dot(tri, la_hi) + _dot(tri, la_lo)
        q = in_ref[rows, q0:q0 + GLA_QK_W].astype(F32)
        k = in_ref[rows, k0:k0 + GLA_QK_W].astype(F32)
        q_e = (q * (GLA_DK ** -0.5) * jnp.exp(cum)).astype(BF16)
        k_e = (k * jnp.exp(-cum)).astype(BF16)
        tots = [cum[(c + 1) * c_ - 1:(c + 1) * c_, :] for c in range(2)]
        k_s = jnp.concatenate([(k[c * c_:(c + 1) * c_] * jnp.exp(tots[c] - cum[c * c_:(c + 1) * c_])).astype(BF16)
                               for c in range(2)], axis=0)
        return q_e, k_e, k_s, tots

    def local_products(g, gates):
        q_e, k_e, k_s, _ = gates
        rows_g = slice(g * g_, (g + 1) * g_)
        out = []
        for h in range(GLA_HEADS):
            tile = slice((h // 2) * LANES, (h // 2 + 1) * LANES)
            head_lanes = first_half if h % 2 == 0 else ~first_half
            qe_m = jnp.where(head_lanes, q_e[:, tile], jnp.zeros((), BF16))
            ke_t = k_e[:, tile]
            ks_t = k_s[:, tile]
            v_h = in_ref[rows_g, v0 + h * GLA_DV:v0 + (h + 1) * GLA_DV]
            vt_h = v_h.astype(F32).T.astype(BF16)
            per_chunk = []
            for c in range(2):
                rows = slice(c * c_, (c + 1) * c_)
                att = jnp.where(causal, _dot_nt(qe_m[rows], ke_t[rows]), 0.0).astype(BF16)
                chunk_lanes = first_half if c == 0 else ~first_half
                kv_t = _dot(jnp.where(chunk_lanes, vt_h, jnp.zeros((), BF16)), ks_t)
                per_chunk.append((_dot(att, v_h[rows]), kv_t, qe_m[rows]))
            out.append(per_chunk)
        return out

    gates = [decays(0)]
    local = []
    for g in range(n_g):
        if g + 1 < n_g:
            gates.append(decays(g + 1))
        local.append(local_products(g, gates[g]))

    for h in range(GLA_HEADS):
        tile = slice((h // 2) * LANES, (h // 2 + 1) * LANES)
        st = s_ref[h]
        s_prev = []
        for g in range(n_g):
            for c in range(2):
                s_prev.append(st.astype(BF16))
                st = st * jnp.exp(gates[g][3][c][:, tile]) + local[g][h][c][1]
        s_ref[h] = st
        for g in range(n_g):
            r_h = in_ref[g * g_:(g + 1) * g_, r0 + h * GLA_DV:r0 + (h + 1) * GLA_DV].astype(F32)
            for c in range(2):
                o_intra, _, qe_c = local[g][h][c]
                o_h = o_intra + _dot_nt(qe_c, s_prev[2 * g + c])
                out_rows = slice(g * g_ + c * c_, g * g_ + (c + 1) * c_)
                o_ref[out_rows, h * GLA_DV:(h + 1) * GLA_DV] = (
                    _rms(o_h, gn) * jax.nn.silu(r_h[c * c_:(c + 1) * c_])).astype(o_ref.dtype)


def _gla(gla_in, w_a2p, b_a, gn, batch, seq):
    t = gla_in.shape[0]
    assert seq % GLA_ROWS == 0 and GLA_ROWS % (2 * GLA_CHUNK) == 0 and 2 * GLA_DK == LANES and 2 * GLA_CHUNK == LANES
    nsb = seq // GLA_ROWS
    return pl.pallas_call(
        _gla_body,
        grid=(batch, nsb),
        in_specs=[
            pl.BlockSpec((GLA_ROWS, GLA_IN_W), lambda b, j: (b * nsb + j, 0)),
            pl.BlockSpec((2 * GLA_CHUNK, 2 * GLA_CHUNK), lambda b, j: (0, 0)),
            pl.BlockSpec((A_PAD, GLA_QK_W), lambda b, j: (0, 0)),
            pl.BlockSpec((1, GLA_QK_W), lambda b, j: (0, 0)),
            pl.BlockSpec((1, GLA_DV), lambda b, j: (0, 0)),
        ],
        out_specs=pl.BlockSpec((GLA_ROWS, GLA_V_W), lambda b, j: (b * nsb + j, 0)),
        out_shape=jax.ShapeDtypeStruct((t, GLA_V_W), BF16),
        scratch_shapes=[pltpu.VMEM((GLA_HEADS, GLA_DV, LANES), F32)],
        compiler_params=_params(("parallel", "arbitrary")),
        name="gla",
    )(gla_in, _chunk_prefix_operator(), w_a2p, b_a.reshape(1, -1), gn.reshape(1, -1))


def _alibi_features(seq):
    j = np.arange(seq) % ATT_TILE
    j_lo = j % 256
    j_hi = j - j_lo
    kf = np.zeros((seq, 2 * DIFF_DQK), np.float32)
    for base in (0, DIFF_DQK):
        for f in range(3):
            kf[:, base + f] = j_lo
            kf[:, base + 3 + f] = j_hi
    slopes = 2.0 ** (-8.0 * np.arange(1, DIFF_HEADS + 1) / DIFF_HEADS)
    c = jnp.asarray(slopes * math.log2(math.e), dtype=F32)
    c1 = c.astype(BF16)
    c2 = (c - c1.astype(F32)).astype(BF16)
    c3 = (c - c1.astype(F32) - c2.astype(F32)).astype(BF16)
    terms = jnp.stack([c1, c2, c3, c1, c2, c3], axis=1)
    qf = jnp.zeros((DIFF_HEADS, 2 * DIFF_DQK), BF16)
    qf = qf.at[:, 0:6].set(terms).at[:, DIFF_DQK:DIFF_DQK + 6].set(terms)
    qf = jnp.repeat(qf, 8, axis=0)
    return jnp.asarray(kf, dtype=BF16), qf, c


def _diff_body(c_ref, q_ref, k_ref, vt_ref, qf_ref, kf_ref, lam_ref, gn_ref, o_ref, kaug_ref, vta_ref):
    tq = ATT_TILE
    h = pl.program_id(1)
    seq = k_ref.shape[0]
    nq = seq // tq
    c = c_ref[h]

    k = k_ref[...]
    kf = kf_ref[...]
    lane_k = lax.broadcasted_iota(jnp.int32, k.shape, 1)
    kaug_ref[0] = jnp.where(lane_k < DIFF_DQK, k, kf)
    kaug_ref[1] = jnp.where(lane_k >= DIFF_DQK, k, kf)
    for jj in range(nq):
        vta_ref[jj, 0:DIFF_DV, :] = vt_ref[:, jj * tq:(jj + 1) * tq]
        vta_ref[jj, DIFF_DV:, :] = jnp.ones((ATT_ONES, tq), BF16)

    lq1, lk1, lq2, lk2 = (lam_ref[i:i + 1, :] for i in range(4))
    lam = (jnp.exp(jnp.sum(lq1 * lk1, axis=-1, keepdims=True))
           - jnp.exp(jnp.sum(lq2 * lk2, axis=-1, keepdims=True)) + LAMBDA_INIT)
    row = lax.broadcasted_iota(jnp.int32, (tq, tq), 0)
    col = lax.broadcasted_iota(jnp.int32, (tq, tq), 1)
    visible = row <= col
    lane = lax.broadcasted_iota(jnp.int32, (tq, 2 * DIFF_DQK), 1)
    qf = jnp.broadcast_to(qf_ref[0:1, :], (tq, 2 * DIFF_DQK))

    steps = [(qi, j) for qi in range(nq) for j in range(qi + 1)]
    q_maps = {}

    def scores(qi, j):
        if qi not in q_maps:
            q = q_ref[qi * tq:(qi + 1) * tq, :]
            q_maps[qi] = (jnp.where(lane < DIFF_DQK, q, qf), jnp.where(lane >= DIFF_DQK, q, qf))
        return [_dot_nt(kaug_ref[m, j * tq:(j + 1) * tq, :], q_maps[qi][m]) for m in range(2)]

    m_run = [None, None]
    acc = [None, None]
    s_next = scores(*steps[0])
    for t, (qi, j) in enumerate(steps):
        s_cur = s_next
        if t + 1 < len(steps):
            s_next = scores(*steps[t + 1])
        vblk = vta_ref[j]
        base = c * float(j * tq)
        for m in range(2):
            s_t = s_cur[m]
            if j == qi:
                s_t = jnp.where(visible, s_t, -jnp.inf)
            blk_max = jnp.max(s_t, axis=0, keepdims=True) + base
            if j == 0:
                m_new = blk_max
                acc[m] = _dot(vblk, jnp.exp2(s_t - (m_new - base)).astype(BF16))
            else:
                m_new = jnp.maximum(m_run[m], blk_max)
                alpha = jnp.exp2(m_run[m] - m_new)
                acc[m] = alpha * acc[m] + _dot(vblk, jnp.exp2(s_t - (m_new - base)).astype(BF16))
            m_run[m] = m_new
        if j == qi:
            o_t = (acc[0][0:DIFF_DV] / acc[0][DIFF_DV:DIFF_DV + 1]
                   - lam * (acc[1][0:DIFF_DV] / acc[1][DIFF_DV:DIFF_DV + 1]))
            o = o_t.T
            o_ref[qi * tq:(qi + 1) * tq, :] = (_rms(o, gn_ref[...]) * (1.0 - LAMBDA_INIT)).astype(o_ref.dtype)


def _diff_attn(dqk, v_t, lam_params, gn, batch, seq):
    t = dqk.shape[0]
    tq = ATT_TILE
    assert seq % tq == 0
    nq = seq // tq
    kfeat, qfeat, c = _alibi_features(seq)
    rows = DIFF_DV + ATT_ONES
    grid_spec = pltpu.PrefetchScalarGridSpec(
        num_scalar_prefetch=1,
        grid=(batch, DIFF_HEADS),
        in_specs=[
            pl.BlockSpec((seq, 2 * DIFF_DQK), lambda b, h, s: (b, h)),
            pl.BlockSpec((seq, 2 * DIFF_DQK), lambda b, h, s: (b, DIFF_HEADS + h)),
            pl.BlockSpec((DIFF_DV, seq), lambda b, h, s: (h, b)),
            pl.BlockSpec((8, 2 * DIFF_DQK), lambda b, h, s: (h, 0)),
            pl.BlockSpec((seq, 2 * DIFF_DQK), lambda b, h, s: (0, 0)),
            pl.BlockSpec((4, DIFF_DQK), lambda b, h, s: (0, 0)),
            pl.BlockSpec((1, DIFF_DV), lambda b, h, s: (0, 0)),
        ],
        out_specs=pl.BlockSpec((seq, DIFF_DV), lambda b, h, s: (b, h)),
        scratch_shapes=[
            pltpu.VMEM((2, seq, 2 * DIFF_DQK), BF16),
            pltpu.VMEM((nq, rows, tq), BF16),
        ],
    )
    return pl.pallas_call(
        _diff_body,
        grid_spec=grid_spec,
        out_shape=jax.ShapeDtypeStruct((t, DIFF_V_W), BF16),
        compiler_params=_params(("parallel", "parallel")),
        name="diff_attn",
    )(c, dqk, dqk, v_t, qfeat, kfeat, lam_params, gn.reshape(1, -1))


def _route(logits):
    lane = lax.broadcasted_iota(jnp.int32, logits.shape, 0)
    big = jnp.int32(LANES)
    neg = -jnp.inf

    def first_argmax(vals, vmax):
        return jnp.min(jnp.where(vals == vmax, lane, big), axis=0, keepdims=True)

    is_grp = (lane >= N_EXPERTS) & (lane < N_EXPERTS + N_GROUPS)
    lg = jnp.where(is_grp, logits, neg)
    mg = jnp.max(lg, axis=0, keepdims=True)
    p_g = 1.0 / jnp.sum(jnp.exp(lg - mg), axis=0, keepdims=True)
    g_sel = first_argmax(lg, mg) - N_EXPERTS
    in_grp = (lane >= g_sel * EXPERTS_PER_GROUP) & (lane < (g_sel + 1) * EXPERTS_PER_GROUP)
    le = jnp.where(in_grp, logits, neg)
    m1 = jnp.max(le, axis=0, keepdims=True)
    i1 = first_argmax(le, m1)
    le2 = jnp.where(lane == i1, neg, le)
    m2 = jnp.max(le2, axis=0, keepdims=True)
    i2 = first_argmax(le2, m2)
    den = jnp.sum(jnp.exp(le - m1), axis=0, keepdims=True)
    w1 = 1.0 / den
    w2 = jnp.exp(m2 - m1) / den
    wsum = w1 + w2
    g1 = p_g * w1 / wsum
    g2 = p_g * w2 / wsum
    first_is_lo = i1 < i2
    g_lo = jnp.where(first_is_lo, g1, g2)
    g_hi = jnp.where(first_is_lo, g2, g1)
    a = jnp.minimum(i1, i2) - g_sel * EXPERTS_PER_GROUP
    b = jnp.maximum(i1, i2) - g_sel * EXPERTS_PER_GROUP
    pair = 3 * a - jnp.where(a == 2, 1, 0) + (b - a - 1)
    cls = (g_sel * N_PAIRS + pair).astype(F32)
    out = jnp.where(lane == 0, g_lo, jnp.where(lane == 1, g_hi, cls))
    return jnp.where(lane < 3, out, 0.0).T


def _post_mix_body(x_ref, og_ref, od_ref, wo_ref, gc_ref, wq_ref, km_ref, vm_ref, wco_ref, gf_ref, wr_ref, br_ref,
                   x2e_ref):
    tiles = [slice(j * SUB_TILE, (j + 1) * SUB_TILE) for j in range(MIX_TILE // SUB_TILE)]

    def out_proj(rows):
        return x_ref[rows] + _dot(jnp.concatenate([og_ref[rows], od_ref[rows]], axis=1), wo_ref[...])

    def cross_query(x1):
        return _dot(_rms(x1, gc_ref[...]).astype(BF16), wq_ref[...]).astype(BF16)

    def cross_attend(qc):
        heads = []
        for h in range(CROSS_HEADS):
            sl = slice(h * CROSS_DH, (h + 1) * CROSS_DH)
            s = _dot_nt(qc[:, sl], km_ref[:, sl]) * (CROSS_DH ** -0.5)
            s = s - jnp.max(s, axis=-1, keepdims=True)
            p = jnp.exp(s)
            p = p / jnp.sum(p, axis=-1, keepdims=True)
            heads.append(_dot(p.astype(BF16), vm_ref[:, sl]).astype(BF16))
        return jnp.concatenate(heads, axis=-1)

    def finish(rows, x2):
        h3 = _rms(x2, gf_ref[...]).astype(BF16)
        x2e_ref[rows, 0:D_MODEL] = x2
        x2e_ref[rows, D_MODEL:] = _route(_dot_nt(wr_ref[...], h3) + br_ref[...])

    x1 = [out_proj(rows) for rows in tiles]
    qc = [cross_query(v) for v in x1]
    oc = [cross_attend(v) for v in qc]
    x2 = [a + _dot(b, wco_ref[...]) for a, b in zip(x1, oc)]
    for rows, v in zip(tiles, x2):
        finish(rows, v)


def _post_mix(x, og, od, w_out, g_cross, w_cq, kmem, vmem, w_co, g_ffn, w_r, b_r, seq, mem_len):
    t, d = x.shape
    tm = MIX_TILE
    assert seq % tm == 0
    per_b = seq // tm
    full = lambda shape: pl.BlockSpec(shape, lambda i: (0, 0))
    rows = lambda w: pl.BlockSpec((tm, w), lambda i: (i, 0))
    return pl.pallas_call(
        _post_mix_body,
        grid=(t // tm,),
        in_specs=[
            rows(d), rows(GLA_V_W), rows(DIFF_V_W),
            full((d, d)), full((1, d)), full((d, d)),
            pl.BlockSpec((mem_len, d), lambda i: (i // per_b, 0)),
            pl.BlockSpec((mem_len, d), lambda i: (i // per_b, 0)),
            full((d, d)), full((1, d)), full((LANES, d)), full((LANES, 1)),
        ],
        out_specs=rows(ROW_W),
        out_shape=jax.ShapeDtypeStruct((t, ROW_W), F32),
        compiler_params=_params(("parallel",)),
        name="post_mix",
    )(x, og, od, w_out, g_cross.reshape(1, d), w_cq, kmem, vmem, w_co, g_ffn.reshape(1, d), w_r, b_r)


def _dispatch_body(fill_ref, dest_ref, src_ref, dst_ref, sem):
    n8 = DISPATCH_ROWS // SUBLANES
    i = pl.program_id(0)

    def dst_row(d):
        return dst_ref.at[lax.shift_right_logical(d, 3), pl.ds(d & (SUBLANES - 1), 1)]

    def issue(r8, carry):
        for u in range(SUBLANES):
            d = dest_ref[0, 0, r8 * SUBLANES + u]
            pltpu.make_async_copy(src_ref.at[r8, pl.ds(u, 1)], dst_row(d), sem.at[0]).start(priority=u % 2)
        return carry

    lax.fori_loop(0, n8, issue, 0)

    def fill_padding(start):
        def go(cp):
            cp.start() if start else cp.wait()

        def fill_tiles(t0, nt):
            go(pltpu.make_async_copy(src_ref.at[pl.ds(0, nt)], dst_ref.at[pl.ds(t0, nt)], sem.at[1]))

        def per_class(c, carry):
            lo = fill_ref[3 * c]
            mid = fill_ref[3 * c + 1]
            hi = fill_ref[3 * c + 2]
            lax.fori_loop(lo, mid, lambda r, cc: (go(pltpu.make_async_copy(src_ref.at[0, pl.ds(0, 1)], dst_row(r),
                                                                             sem.at[1])), cc)[1], 0)
            tiles = lax.shift_right_logical(hi - mid, 3)
            off = lax.shift_right_logical(mid, 3)
            for bit in reversed(range((MOE_ROWS // SUBLANES).bit_length() - 1)):
                has = (tiles & (1 << bit)) != 0

                @pl.when(has)
                def _(off=off, bit=bit):
                    fill_tiles(off, 1 << bit)

                off = off + jnp.where(has, 1 << bit, 0)
            return carry

        lax.fori_loop(0, N_CLASSES, per_class, 0)
        blk8 = MOE_ROWS // SUBLANES
        n_blocks = dst_ref.shape[0] // blk8
        lax.fori_loop(fill_ref[3 * N_CLASSES], n_blocks, lambda b, cc: (fill_tiles(b * blk8, blk8), cc)[1], 0)

    @pl.when(i == 0)
    def _():
        fill_padding(True)
        fill_padding(False)

    pltpu.make_async_copy(src_ref, dst_ref.at[pl.ds(0, n8)], sem.at[0]).wait()


def _dispatch(src, dest, fill, n_rows):
    t, w = src.shape
    assert t % DISPATCH_ROWS == 0 and DISPATCH_ROWS >= MOE_ROWS and SUBLANES == 8
    nb = t // DISPATCH_ROWS
    n8 = DISPATCH_ROWS // SUBLANES
    grid_spec = pltpu.PrefetchScalarGridSpec(
        num_scalar_prefetch=1,
        grid=(nb,),
        in_specs=[
            pl.BlockSpec((1, 1, DISPATCH_ROWS), lambda i, f: (i, 0, 0), memory_space=pltpu.SMEM),
            pl.BlockSpec((n8, SUBLANES, w), lambda i, f: (i, 0, 0)),
        ],
        out_specs=pl.BlockSpec(memory_space=pl.ANY),
        scratch_shapes=[pltpu.SemaphoreType.DMA((2,))],
    )
    out = pl.pallas_call(
        _dispatch_body,
        grid_spec=grid_spec,
        out_shape=jax.ShapeDtypeStruct((n_rows // SUBLANES, SUBLANES, w), src.dtype),
        compiler_params=_params(("arbitrary",)),
        name="dispatch",
    )(fill, dest.reshape(nb, 1, DISPATCH_ROWS), src.reshape(t // SUBLANES, SUBLANES, w))
    return out.reshape(n_rows, w)


def _expert_body(lo_ref, hi_ref, nvalid_ref, xs_ref, gf_ref, wg_a, wu_a, wd_a, wg_b, wu_b, wd_b, ys_ref, up_bf, dn_bf):
    i = pl.program_id(0)
    valid = i < nvalid_ref[0]
    prev = jnp.maximum(i - 1, 0)
    new_pair = (i == 0) | (lo_ref[i] != lo_ref[prev]) | (hi_ref[i] != hi_ref[prev])

    @pl.when(valid & new_pair)
    def _():
        for slot, w in enumerate((wg_a, wu_a, wg_b, wu_b)):
            up_bf[slot] = w[...].astype(BF16)
        for slot, w in enumerate((wd_a, wd_b)):
            dn_bf[slot] = w[...].astype(BF16)

    @pl.when(valid)
    def _():
        xb = xs_ref[...]
        h = _rms(xb[:, 0:D_MODEL], gf_ref[...]).astype(BF16)

        gate_a, up_a = _dot(h, up_bf[0]), _dot(h, up_bf[1])
        gate_b, up_b = _dot(h, up_bf[2]), _dot(h, up_bf[3])
        hid_a = (jax.nn.silu(gate_a) * up_a).astype(BF16)
        hid_b = (jax.nn.silu(gate_b) * up_b).astype(BF16)
        y_lo = xb[:, D_MODEL:D_MODEL + 1] * _dot(hid_a, dn_bf[0])
        ys_ref[...] = y_lo + xb[:, D_MODEL + 1:D_MODEL + 2] * _dot(hid_b, dn_bf[1])

    @pl.when(i >= nvalid_ref[0])
    def _():
        ys_ref[...] = jnp.zeros_like(ys_ref)


def _experts(xs, g_ffn, blk_lo, blk_hi, nvalid, w_g, w_u, w_d):
    n_rows = xs.shape[0]
    nb = n_rows // MOE_ROWS
    d, de = w_g.shape[1], w_g.shape[2]
    up = lambda sel: pl.BlockSpec((None, d, de), lambda i, lo, hi, nv: (sel(lo, hi)[i], 0, 0))
    down = lambda sel: pl.BlockSpec((None, de, d), lambda i, lo, hi, nv: (sel(lo, hi)[i], 0, 0))
    first = lambda lo, hi: lo
    second = lambda lo, hi: hi
    grid_spec = pltpu.PrefetchScalarGridSpec(
        num_scalar_prefetch=3,
        grid=(nb,),
        in_specs=[
            pl.BlockSpec((MOE_ROWS, ROW_W), lambda i, lo, hi, nv: (jnp.minimum(i, nv[0] - 1), 0)),
            pl.BlockSpec((1, d), lambda i, lo, hi, nv: (0, 0)),
            up(first), up(first), down(first), up(second), up(second), down(second),
        ],
        out_specs=pl.BlockSpec((MOE_ROWS, d), lambda i, lo, hi, nv: (i, 0)),
        scratch_shapes=[pltpu.VMEM((4, d, de), BF16), pltpu.VMEM((2, de, d), BF16)],
    )
    return pl.pallas_call(
        _expert_body,
        grid_spec=grid_spec,
        out_shape=jax.ShapeDtypeStruct((n_rows, d), F32),
        compiler_params=_params(("arbitrary",)),
        name="experts",
    )(blk_lo, blk_hi, nvalid, xs, g_ffn.reshape(1, d), w_g, w_u, w_d, w_g, w_u, w_d)


def _combine_body(cur_ref, nxt_ref, x2_ref, ys_ref, g_ref, o_ref, ybuf, sem):
    n8 = ROW_TILE // SUBLANES
    i = pl.program_id(0)
    last = pl.num_programs(0) - 1
    slot = i % 2

    def issue_tile(idx_ref, s):
        def issue(r8, carry):
            for u in range(SUBLANES):
                d = idx_ref[0, 0, r8 * SUBLANES + u]
                src = ys_ref.at[lax.shift_right_logical(d, 3), pl.ds(d & (SUBLANES - 1), 1)]
                pltpu.make_async_copy(src, ybuf.at[s, r8, pl.ds(u, 1)], sem.at[s]).start(priority=u % 2)
            return carry

        lax.fori_loop(0, n8, issue, 0)

    @pl.when(i == 0)
    def _():
        issue_tile(cur_ref, 0)

    @pl.when(i < last)
    def _():
        issue_tile(nxt_ref, 1 - slot)

    pltpu.make_async_copy(ys_ref.at[pl.ds(0, n8)], ybuf.at[slot], sem.at[slot]).wait()
    y = ybuf[slot].reshape(ROW_TILE, D_MODEL)
    o_ref[...] = _rms(x2_ref[...] + y, g_ref[...])


def _combine(x2e, ys, dest, g_final):
    t = x2e.shape[0]
    d = D_MODEL
    tm = ROW_TILE
    nt = t // tm
    idx = dest.reshape(nt, 1, tm)
    return pl.pallas_call(
        _combine_body,
        grid=(nt,),
        in_specs=[
            pl.BlockSpec((1, 1, tm), lambda i: (i, 0, 0), memory_space=pltpu.SMEM),
            pl.BlockSpec((1, 1, tm), lambda i: (jnp.minimum(i + 1, nt - 1), 0, 0), memory_space=pltpu.SMEM),
            pl.BlockSpec((tm, d), lambda i: (i, 0)),
            pl.BlockSpec(memory_space=pl.ANY),
            pl.BlockSpec((1, d), lambda i: (0, 0)),
        ],
        out_specs=pl.BlockSpec((tm, d), lambda i: (i, 0)),
        out_shape=jax.ShapeDtypeStruct((t, d), F32),
        scratch_shapes=[pltpu.VMEM((2, tm // SUBLANES, SUBLANES, d), F32), pltpu.SemaphoreType.DMA((2,))],
        compiler_params=_params(("arbitrary",)),
        name="combine",
    )(idx, idx, x2e, ys.reshape(ys.shape[0] // SUBLANES, SUBLANES, d), g_final.reshape(1, d))


def _dispatch_plan(cls, t):
    onehot = (cls[:, None] == jnp.arange(N_CLASSES, dtype=jnp.int32)[None, :]).astype(jnp.int32)
    ranks = jnp.cumsum(onehot, axis=0) - onehot
    counts = jnp.sum(onehot, axis=0)
    padded = ((counts + MOE_ROWS - 1) // MOE_ROWS) * MOE_ROWS
    pad_end = jnp.cumsum(padded)
    pad_start = pad_end - padded
    dest = jnp.sum(onehot * (ranks + pad_start[None, :]), axis=1).astype(jnp.int32)
    fill_lo = pad_start + counts
    fill_mid = jnp.minimum(((fill_lo + SUBLANES - 1) // SUBLANES) * SUBLANES, pad_end)
    nvalid = (pad_end[-1] // MOE_ROWS).astype(jnp.int32).reshape(1)
    fill = jnp.concatenate([jnp.stack([fill_lo, fill_mid, pad_end], axis=1).reshape(-1), nvalid]).astype(jnp.int32)
    n_blocks = t // MOE_ROWS + N_CLASSES
    blk_start = jnp.arange(n_blocks, dtype=jnp.int32) * MOE_ROWS
    blk_cls = jnp.minimum(jnp.sum((blk_start[:, None] >= pad_end[None, :]).astype(jnp.int32), axis=1), N_CLASSES - 1)
    grp = blk_cls // N_PAIRS
    pair_onehot = ((blk_cls % N_PAIRS)[:, None] == jnp.arange(N_PAIRS, dtype=jnp.int32)[None, :]).astype(jnp.int32)
    blk_lo = grp * EXPERTS_PER_GROUP + jnp.sum(pair_onehot * jnp.asarray(PAIR_LO, jnp.int32)[None, :], axis=1)
    blk_hi = grp * EXPERTS_PER_GROUP + jnp.sum(pair_onehot * jnp.asarray(PAIR_HI, jnp.int32)[None, :], axis=1)
    return dest, fill, blk_lo.astype(jnp.int32), blk_hi.astype(jnp.int32), nvalid, n_blocks * MOE_ROWS


def _regroup_w_in(w_in):
    splits = np.cumsum([GLA_QK_W, GLA_QK_W, GLA_V_W, GLA_RANK, GLA_V_W, DIFF_QK_W, DIFF_QK_W, DIFF_V_W])[:-1]
    q_g, k_g, v_g, a_lr, r_g, q_d, k_d, v_d = jnp.split(w_in, [int(i) for i in splits], axis=1)
    a_pad = jnp.pad(a_lr, ((0, 0), (0, A_PAD - GLA_RANK)))
    q_d = q_d * (DIFF_DQK ** -0.5 * math.log2(math.e))
    return jnp.concatenate([q_g, k_g, v_g, r_g, a_pad, q_d, k_d], axis=1).astype(BF16), v_d.T.astype(BF16)


def kernel(x, mem, norm_mix_g, w_in, w_gla_a2, b_gla_a, gla_norm_g, diff_norm_g, lambda_q1, lambda_k1, lambda_q2,
           lambda_k2, w_out, norm_cross_g, norm_mem_g, w_cq, w_ckv, w_co, norm_ffn_g, w_router_grp, b_router_grp,
           w_router_exp, b_router_exp, w_e_gate, w_e_up, w_e_down, norm_final_g):
    b_, s_, d_ = x.shape
    m_ = mem.shape[1]
    t = b_ * s_
    xf = x.reshape(t, d_)

    kmem, vmem = _norm_matmul(mem.reshape(b_ * m_, d_), norm_mem_g[0], w_ckv[0].astype(BF16), (d_, d_), "mem_kv")

    w_cols, w_vt = _regroup_w_in(w_in[0])
    gla_in, dqk, v_t = _in_proj(xf, norm_mix_g[0], w_cols, w_vt)

    w_a2p = jnp.pad(w_gla_a2[0], ((0, A_PAD - GLA_RANK), (0, 0))).astype(BF16)
    o_g = _gla(gla_in, w_a2p, b_gla_a[0], gla_norm_g[0], b_, s_)

    lam_params = jnp.stack([lambda_q1[0], lambda_k1[0], lambda_q2[0], lambda_k2[0]]).astype(F32)
    o_d = _diff_attn(dqk, v_t, lam_params, diff_norm_g[0], b_, s_)

    pad_r = LANES - N_EXPERTS - N_GROUPS
    w_r = jnp.pad(jnp.concatenate([w_router_exp[0], w_router_grp[0]], axis=1).T, ((0, pad_r), (0, 0))).astype(BF16)
    b_r = jnp.pad(jnp.concatenate([b_router_exp[0], b_router_grp[0]]), (0, pad_r)).reshape(LANES, 1)
    x2e = _post_mix(xf, o_g, o_d, w_out[0].astype(BF16), norm_cross_g[0], w_cq[0].astype(BF16), kmem, vmem,
                    w_co[0].astype(BF16), norm_ffn_g[0], w_r, b_r, s_, m_)

    cls = x2e[:, D_MODEL + 2].astype(jnp.int32)
    dest, fill, blk_lo, blk_hi, nvalid, n_rows = _dispatch_plan(cls, t)
    xs = _dispatch(x2e, dest, fill, n_rows)
    ys = _experts(xs, norm_ffn_g[0], blk_lo, blk_hi, nvalid, w_e_gate[0], w_e_up[0], w_e_down[0])

    out = _combine(x2e, ys, dest, norm_final_g)
    return out.reshape(b_, s_, d_)
```

```python
import math

import jax
import jax.numpy as jnp
import numpy as np
from jax import lax
from jax.experimental import pallas as pl
from jax.experimental.pallas import tpu as pltpu

F32 = jnp.float32
BF16 = jnp.bfloat16

EPS = 1e-6
D_MODEL = 1024
GLA_HEADS = 4
GLA_DK = 64
GLA_DV = 128
GLA_RANK = 16
GLA_TAU = 16.0
GLA_CHUNK = 64
GLA_QK_W = GLA_HEADS * GLA_DK
GLA_V_W = GLA_HEADS * GLA_DV
DIFF_HEADS = 4
DIFF_DQK = 64
DIFF_DV = 128
DIFF_QK_W = DIFF_HEADS * 2 * DIFF_DQK
DIFF_V_W = DIFF_HEADS * DIFF_DV
CROSS_HEADS = 4
CROSS_DH = D_MODEL // CROSS_HEADS
N_GROUPS = 4
EXPERTS_PER_GROUP = 4
N_EXPERTS = N_GROUPS * EXPERTS_PER_GROUP
LAMBDA_INIT = 0.8 - 0.6 * math.exp(-0.3 * 0)

LANES = 128
SUBLANES = 8
A_PAD = LANES
GLA_IN_W = 2 * GLA_QK_W + 2 * GLA_V_W + A_PAD
ROW_W = D_MODEL + LANES
PAIR_LO = (0, 0, 0, 1, 1, 2)
PAIR_HI = (1, 2, 3, 2, 3, 3)
N_PAIRS = len(PAIR_LO)
N_CLASSES = N_GROUPS * N_PAIRS

ROW_TILE = 512
MIX_TILE = 1024
SUB_TILE = 512
GLA_ROWS = 2048
ATT_TILE = 512
ATT_ONES = 16
MOE_ROWS = 512
DISPATCH_ROWS = 4096
COMBINE_ROWS = 1024
VMEM_LIMIT = 56 * 1024 * 1024


def _params(sem):
    return pltpu.CompilerParams(dimension_semantics=sem, vmem_limit_bytes=VMEM_LIMIT)


def _rms(x, g):
    ms = jnp.mean(x * x, axis=-1, keepdims=True)
    return x * lax.rsqrt(ms + EPS) * g


def _dot(a, b):
    return jnp.dot(a, b, preferred_element_type=F32)


def _dot_nt(a, b):
    return lax.dot_general(a, b, (((1,), (1,)), ((), ())), preferred_element_type=F32)


def _norm_matmul_body(x_ref, g_ref, w_ref, *o_refs):
    h = _rms(x_ref[...], g_ref[...]).astype(BF16)
    off = 0
    for o_ref in o_refs:
        n = o_ref.shape[-1]
        o_ref[...] = _dot(h, w_ref[:, off:off + n]).astype(o_ref.dtype)
        off += n


def _norm_matmul(x, g, w, splits, name):
    t, d = x.shape
    n = w.shape[1]
    assert sum(splits) == n and t % ROW_TILE == 0
    return pl.pallas_call(
        _norm_matmul_body,
        grid=(t // ROW_TILE,),
        in_specs=[
            pl.BlockSpec((ROW_TILE, d), lambda i: (i, 0)),
            pl.BlockSpec((1, d), lambda i: (0, 0)),
            pl.BlockSpec((d, n), lambda i: (0, 0)),
        ],
        out_specs=[pl.BlockSpec((ROW_TILE, s), lambda i: (i, 0)) for s in splits],
        out_shape=[jax.ShapeDtypeStruct((t, s), BF16) for s in splits],
        compiler_params=_params(("parallel",)),
        name=name,
    )(x, g.reshape(1, d), w)


def _in_proj_body(x_ref, g_ref, w_ref, wvt_ref, gla_ref, dqk_ref, vt_ref):
    h = _rms(x_ref[...], g_ref[...]).astype(BF16)
    gla_ref[...] = _dot(h, w_ref[:, 0:GLA_IN_W]).astype(gla_ref.dtype)
    dqk_ref[...] = _dot(h, w_ref[:, GLA_IN_W:]).astype(dqk_ref.dtype)
    vt_ref[...] = _dot_nt(wvt_ref[...], h).astype(vt_ref.dtype)


def _in_proj(x, g, w, w_vt):
    t, d = x.shape
    n = w.shape[1]
    tm = ROW_TILE
    assert n == GLA_IN_W + 2 * DIFF_QK_W and t % tm == 0
    return pl.pallas_call(
        _in_proj_body,
        grid=(t // tm,),
        in_specs=[
            pl.BlockSpec((tm, d), lambda i: (i, 0)),
            pl.BlockSpec((1, d), lambda i: (0, 0)),
            pl.BlockSpec((d, n), lambda i: (0, 0)),
            pl.BlockSpec((DIFF_V_W, d), lambda i: (0, 0)),
        ],
        out_specs=[
            pl.BlockSpec((tm, GLA_IN_W), lambda i: (i, 0)),
            pl.BlockSpec((tm, 2 * DIFF_QK_W), lambda i: (i, 0)),
            pl.BlockSpec((DIFF_V_W, tm), lambda i: (0, i)),
        ],
        out_shape=[
            jax.ShapeDtypeStruct((t, GLA_IN_W), BF16),
            jax.ShapeDtypeStruct((t, 2 * DIFF_QK_W), BF16),
            jax.ShapeDtypeStruct((DIFF_V_W, t), BF16),
        ],
        compiler_params=_params(("parallel",)),
        name="in_proj",
    )(x, g.reshape(1, d), w, w_vt)


def _chunk_prefix_operator():
    i = np.arange(2 * GLA_CHUNK)
    same_chunk = (i[:, None] // GLA_CHUNK) == (i[None, :] // GLA_CHUNK)
    return jnp.asarray(same_chunk & (i[None, :] <= i[:, None]), dtype=BF16)


def _gla_body(in_ref, tri_ref, wa_ref, ba_ref, gn_ref, o_ref, s_ref):
    c_ = GLA_CHUNK

    @pl.when(pl.program_id(1) == 0)
    def _():
        s_ref[...] = jnp.zeros_like(s_ref)

    q0, k0, v0, r0, a0 = 0, GLA_QK_W, 2 * GLA_QK_W, 2 * GLA_QK_W + GLA_V_W, 2 * GLA_QK_W + 2 * GLA_V_W
    g_ = 2 * c_
    n_g = GLA_ROWS // g_
    tri = tri_ref[...]
    causal = (lax.broadcasted_iota(jnp.int32, (c_, c_), 0) >= lax.broadcasted_iota(jnp.int32, (c_, c_), 1))
    first_half = lax.broadcasted_iota(jnp.int32, (g_, LANES), 1) < GLA_DK
    gn = gn_ref[...]

    def decays(g):
        rows = slice(g * g_, (g + 1) * g_)
        z = _dot(in_ref[rows, a0:a0 + A_PAD], wa_ref[...]) + ba_ref[...]
        log_a = jax.nn.log_sigmoid(z) / GLA_TAU
        la_hi = log_a.astype(BF16)
        la_lo = (log_a - la_hi.astype(F32)).astype(BF16)
        cum = _dot(tri, la_hi) + _dot(tri, la_lo)
        q = in_ref[rows, q0:q0 + GLA_QK_W].astype(F32)
        k = in_ref[rows, k0:k0 + GLA_QK_W].astype(F32)
        q_e = (q * (GLA_DK ** -0.5) * jnp.exp(cum)).astype(BF16)
        k_e = (k * jnp.exp(-cum)).astype(BF16)
        tots = [cum[(c + 1) * c_ - 1:(c + 1) * c_, :] for c in range(2)]
        k_s = jnp.concatenate([(k[c * c_:(c + 1) * c_] * jnp.exp(tots[c] - cum[c * c_:(c + 1) * c_])).astype(BF16)
                               for c in range(2)], axis=0)
        return q_e, k_e, k_s, tots

    def local_products(g, gates):
        q_e, k_e, k_s, _ = gates
        rows_g = slice(g * g_, (g + 1) * g_)
        out = []
        for h in range(GLA_HEADS):
            tile = slice((h // 2) * LANES, (h // 2 + 1) * LANES)
            head_lanes = first_half if h % 2 == 0 else ~first_half
            qe_m = jnp.where(head_lanes, q_e[:, tile], jnp.zeros((), BF16))
            ke_t = k_e[:, tile]
            ks_t = k_s[:, tile]
            v_h = in_ref[rows_g, v0 + h * GLA_DV:v0 + (h + 1) * GLA_DV]
            vt_h = v_h.astype(F32).T.astype(BF16)
            per_chunk = []
            for c in range(2):
                rows = slice(c * c_, (c + 1) * c_)
                att = jnp.where(causal, _dot_nt(qe_m[rows], ke_t[rows]), 0.0).astype(BF16)
                chunk_lanes = first_half if c == 0 else ~first_half
                kv_t = _dot(jnp.where(chunk_lanes, vt_h, jnp.zeros((), BF16)), ks_t)
                per_chunk.append((_dot(att, v_h[rows]), kv_t, qe_m[rows]))
            out.append(per_chunk)
        return out

    gates = [decays(0)]
    local = []
    for g in range(n_g):
        if g + 1 < n_g:
            gates.append(decays(g + 1))
        local.append(local_products(g, gates[g]))

    for h in range(GLA_HEADS):
        tile = slice((h // 2) * LANES, (h // 2 + 1) * LANES)
        st = s_ref[h]
        s_prev = []
        for g in range(n_g):
            for c in range(2):
                s_prev.append(st.astype(BF16))
                st = st * jnp.exp(gates[g][3][c][:, tile]) + local[g][h][c][1]
        s_ref[h] = st
        for g in range(n_g):
            r_h = in_ref[g * g_:(g + 1) * g_, r0 + h * GLA_DV:r0 + (h + 1) * GLA_DV].astype(F32)
            for c in range(2):
                o_intra, _, qe_c = local[g][h][c]
                o_h = o_intra + _dot_nt(qe_c, s_prev[2 * g + c])
                out_rows = slice(g * g_ + c * c_, g * g_ + (c + 1) * c_)
                o_ref[out_rows, h * GLA_DV:(h + 1) * GLA_DV] = (
                    _rms(o_h, gn) * jax.nn.silu(r_h[c * c_:(c + 1) * c_])).astype(o_ref.dtype)


def _gla(gla_in, w_a2p, b_a, gn, batch, seq):
    t = gla_in.shape[0]
    assert seq % GLA_ROWS == 0 and GLA_ROWS % (2 * GLA_CHUNK) == 0 and 2 * GLA_DK == LANES and 2 * GLA_CHUNK == LANES
    nsb = seq // GLA_ROWS
    return pl.pallas_call(
        _gla_body,
        grid=(batch, nsb),
        in_specs=[
            pl.BlockSpec((GLA_ROWS, GLA_IN_W), lambda b, j: (b * nsb + j, 0)),
            pl.BlockSpec((2 * GLA_CHUNK, 2 * GLA_CHUNK), lambda b, j: (0, 0)),
            pl.BlockSpec((A_PAD, GLA_QK_W), lambda b, j: (0, 0)),
            pl.BlockSpec((1, GLA_QK_W), lambda b, j: (0, 0)),
            pl.BlockSpec((1, GLA_DV), lambda b, j: (0, 0)),
        ],
        out_specs=pl.BlockSpec((GLA_ROWS, GLA_V_W), lambda b, j: (b * nsb + j, 0)),
        out_shape=jax.ShapeDtypeStruct((t, GLA_V_W), BF16),
        scratch_shapes=[pltpu.VMEM((GLA_HEADS, GLA_DV, LANES), F32)],
        compiler_params=_params(("parallel", "arbitrary")),
        name="gla",
    )(gla_in, _chunk_prefix_operator(), w_a2p, b_a.reshape(1, -1), gn.reshape(1, -1))


def _alibi_features(seq):
    j = np.arange(seq) % ATT_TILE
    j_lo = j % 256
    j_hi = j - j_lo
    kf = np.zeros((seq, 2 * DIFF_DQK), np.float32)
    for base in (0, DIFF_DQK):
        for f in range(3):
            kf[:, base + f] = j_lo
            kf[:, base + 3 + f] = j_hi
    slopes = 2.0 ** (-8.0 * np.arange(1, DIFF_HEADS + 1) / DIFF_HEADS)
    c = jnp.asarray(slopes * math.log2(math.e), dtype=F32)
    c1 = c.astype(BF16)
    c2 = (c - c1.astype(F32)).astype(BF16)
    c3 = (c - c1.astype(F32) - c2.astype(F32)).astype(BF16)
    terms = jnp.stack([c1, c2, c3, c1, c2, c3], axis=1)
    qf = jnp.zeros((DIFF_HEADS, 2 * DIFF_DQK), BF16)
    qf = qf.at[:, 0:6].set(terms).at[:, DIFF_DQK:DIFF_DQK + 6].set(terms)
    qf = jnp.repeat(qf, 8, axis=0)
    return jnp.asarray(kf, dtype=BF16), qf, c


def _diff_body(c_ref, q_ref, k_ref, vt_ref, qf_ref, kf_ref, lam_ref, gn_ref, o_ref, kaug_ref, vta_ref):
    tq = ATT_TILE
    h = pl.program_id(1)
    seq = k_ref.shape[0]
    nq = seq // tq
    c = c_ref[h]

    k = k_ref[...]
    kf = kf_ref[...]
    lane_k = lax.broadcasted_iota(jnp.int32, k.shape, 1)
    kaug_ref[0] = jnp.where(lane_k < DIFF_DQK, k, kf)
    kaug_ref[1] = jnp.where(lane_k >= DIFF_DQK, k, kf)
    for jj in range(nq):
        vta_ref[jj, 0:DIFF_DV, :] = vt_ref[:, jj * tq:(jj + 1) * tq]
        vta_ref[jj, DIFF_DV:, :] = jnp.ones((ATT_ONES, tq), BF16)

    lq1, lk1, lq2, lk2 = (lam_ref[i:i + 1, :] for i in range(4))
    lam = (jnp.exp(jnp.sum(lq1 * lk1, axis=-1, keepdims=True))
           - jnp.exp(jnp.sum(lq2 * lk2, axis=-1, keepdims=True)) + LAMBDA_INIT)
    row = lax.broadcasted_iota(jnp.int32, (tq, tq), 0)
    col = lax.broadcasted_iota(jnp.int32, (tq, tq), 1)
    visible = row <= col
    lane = lax.broadcasted_iota(jnp.int32, (tq, 2 * DIFF_DQK), 1)
    qf = jnp.broadcast_to(qf_ref[0:1, :], (tq, 2 * DIFF_DQK))

    steps = [(qi, j) for qi in range(nq) for j in range(qi + 1)]
    q_maps = {}

    def scores(qi, j):
        if qi not in q_maps:
            q = q_ref[qi * tq:(qi + 1) * tq, :]
            q_maps[qi] = (jnp.where(lane < DIFF_DQK, q, qf), jnp.where(lane >= DIFF_DQK, q, qf))
        return [_dot_nt(kaug_ref[m, j * tq:(j + 1) * tq, :], q_maps[qi][m]) for m in range(2)]

    m_run = [None, None]
    acc = [None, None]
    s_next = scores(*steps[0])
    for t, (qi, j) in enumerate(steps):
        s_cur = s_next
        if t + 1 < len(steps):
            s_next = scores(*steps[t + 1])
        vblk = vta_ref[j]
        base = c * float(j * tq)
        for m in range(2):
            s_t = s_cur[m]
            if j == qi:
                s_t = jnp.where(visible, s_t, -jnp.inf)
            blk_max = jnp.max(s_t, axis=0, keepdims=True) + base
            if j == 0:
                m_new = blk_max
                acc[m] = _dot(vblk, jnp.exp2(s_t - (m_new - base)).astype(BF16))
            else:
                m_new = jnp.maximum(m_run[m], blk_max)
                alpha = jnp.exp2(m_run[m] - m_new)
                acc[m] = alpha * acc[m] + _dot(vblk, jnp.exp2(s_t - (m_new - base)).astype(BF16))
            m_run[m] = m_new
        if j == qi:
            o_t = (acc[0][0:DIFF_DV] / acc[0][DIFF_DV:DIFF_DV + 1]
                   - lam * (acc[1][0:DIFF_DV] / acc[1][DIFF_DV:DIFF_DV + 1]))
            o = o_t.T
            o_ref[qi * tq:(qi + 1) * tq, :] = (_rms(o, gn_ref[...]) * (1.0 - LAMBDA_INIT)).astype(o_ref.dtype)


def _diff_attn(dqk, v_t, lam_params, gn, batch, seq):
    t = dqk.shape[0]
    tq = ATT_TILE
    assert seq % tq == 0
    nq = seq // tq
    kfeat, qfeat, c = _alibi_features(seq)
    rows = DIFF_DV + ATT_ONES
    grid_spec = pltpu.PrefetchScalarGridSpec(
        num_scalar_prefetch=1,
        grid=(batch, DIFF_HEADS),
        in_specs=[
            pl.BlockSpec((seq, 2 * DIFF_DQK), lambda b, h, s: (b, h)),
            pl.BlockSpec((seq, 2 * DIFF_DQK), lambda b, h, s: (b, DIFF_HEADS + h)),
            pl.BlockSpec((DIFF_DV, seq), lambda b, h, s: (h, b)),
            pl.BlockSpec((8, 2 * DIFF_DQK), lambda b, h, s: (h, 0)),
            pl.BlockSpec((seq, 2 * DIFF_DQK), lambda b, h, s: (0, 0)),
            pl.BlockSpec((4, DIFF_DQK), lambda b, h, s: (0, 0)),
            pl.BlockSpec((1, DIFF_DV), lambda b, h, s: (0, 0)),
        ],
        out_specs=pl.BlockSpec((seq, DIFF_DV), lambda b, h, s: (b, h)),
        scratch_shapes=[
            pltpu.VMEM((2, seq, 2 * DIFF_DQK), BF16),
            pltpu.VMEM((nq, rows, tq), BF16),
        ],
    )
    return pl.pallas_call(
        _diff_body,
        grid_spec=grid_spec,
        out_shape=jax.ShapeDtypeStruct((t, DIFF_V_W), BF16),
        compiler_params=_params(("parallel", "parallel")),
        name="diff_attn",
    )(c, dqk, dqk, v_t, qfeat, kfeat, lam_params, gn.reshape(1, -1))


def _route(logits):
    lane = lax.broadcasted_iota(jnp.int32, logits.shape, 0)
    big = jnp.int32(LANES)
    neg = -jnp.inf

    def first_argmax(vals, vmax):
        return jnp.min(jnp.where(vals == vmax, lane, big), axis=0, keepdims=True)

    is_grp = (lane >= N_EXPERTS) & (lane < N_EXPERTS + N_GROUPS)
    lg = jnp.where(is_grp, logits, neg)
    mg = jnp.max(lg, axis=0, keepdims=True)
    p_g = 1.0 / jnp.sum(jnp.exp(lg - mg), axis=0, keepdims=True)
    g_sel = first_argmax(lg, mg) - N_EXPERTS
    in_grp = (lane >= g_sel * EXPERTS_PER_GROUP) & (lane < (g_sel + 1) * EXPERTS_PER_GROUP)
    le = jnp.where(in_grp, logits, neg)
    m1 = jnp.max(le, axis=0, keepdims=True)
    i1 = first_argmax(le, m1)
    le2 = jnp.where(lane == i1, neg, le)
    m2 = jnp.max(le2, axis=0, keepdims=True)
    i2 = first_argmax(le2, m2)
    den = jnp.sum(jnp.exp(le - m1), axis=0, keepdims=True)
    w1 = 1.0 / den
    w2 = jnp.exp(m2 - m1) / den
    wsum = w1 + w2
    g1 = p_g * w1 / wsum
    g2 = p_g * w2 / wsum
    first_is_lo = i1 < i2
    g_lo = jnp.where(first_is_lo, g1, g2)
    g_hi = jnp.where(first_is_lo, g2, g1)
    a = jnp.minimum(i1, i2) - g_sel * EXPERTS_PER_GROUP
    b = jnp.maximum(i1, i2) - g_sel * EXPERTS_PER_GROUP
    pair = 3 * a - jnp.where(a == 2, 1, 0) + (b - a - 1)
    cls = (g_sel * N_PAIRS + pair).astype(F32)
    out = jnp.where(lane == 0, g_lo, jnp.where(lane == 1, g_hi, cls))
    return jnp.where(lane < 3, out, 0.0).T


def _post_mix_body(x_ref, og_ref, od_ref, wo_ref, gc_ref, wq_ref, km_ref, vm_ref, wco_ref, gf_ref, wr_ref, br_ref,
                   x2e_ref):
    tiles = [slice(j * SUB_TILE, (j + 1) * SUB_TILE) for j in range(MIX_TILE // SUB_TILE)]

    def out_proj(rows):
        return x_ref[rows] + _dot(jnp.concatenate([og_ref[rows], od_ref[rows]], axis=1), wo_ref[...])

    def cross_query(x1):
        return _dot(_rms(x1, gc_ref[...]).astype(BF16), wq_ref[...]).astype(BF16)

    def cross_attend(qc):
        heads = []
        for h in range(CROSS_HEADS):
            sl = slice(h * CROSS_DH, (h + 1) * CROSS_DH)
            s = _dot_nt(qc[:, sl], km_ref[:, sl]) * (CROSS_DH ** -0.5)
            s = s - jnp.max(s, axis=-1, keepdims=True)
            p = jnp.exp(s)
            p = p / jnp.sum(p, axis=-1, keepdims=True)
            heads.append(_dot(p.astype(BF16), vm_ref[:, sl]).astype(BF16))
        return jnp.concatenate(heads, axis=-1)

    def finish(rows, x2):
        h3 = _rms(x2, gf_ref[...]).astype(BF16)
        x2e_ref[rows, 0:D_MODEL] = x2
        x2e_ref[rows, D_MODEL:] = _route(_dot_nt(wr_ref[...], h3) + br_ref[...])

    x1 = [out_proj(rows) for rows in tiles]
    qc = [cross_query(v) for v in x1]
    oc = [cross_attend(v) for v in qc]
    x2 = [a + _dot(b, wco_ref[...]) for a, b in zip(x1, oc)]
    for rows, v in zip(tiles, x2):
        finish(rows, v)


def _post_mix(x, og, od, w_out, g_cross, w_cq, kmem, vmem, w_co, g_ffn, w_r, b_r, seq, mem_len):
    t, d = x.shape
    tm = MIX_TILE
    assert seq % tm == 0
    per_b = seq // tm
    full = lambda shape: pl.BlockSpec(shape, lambda i: (0, 0))
    rows = lambda w: pl.BlockSpec((tm, w), lambda i: (i, 0))
    return pl.pallas_call(
        _post_mix_body,
        grid=(t // tm,),
        in_specs=[
            rows(d), rows(GLA_V_W), rows(DIFF_V_W),
            full((d, d)), full((1, d)), full((d, d)),
            pl.BlockSpec((mem_len, d), lambda i: (i // per_b, 0)),
            pl.BlockSpec((mem_len, d), lambda i: (i // per_b, 0)),
            full((d, d)), full((1, d)), full((LANES, d)), full((LANES, 1)),
        ],
        out_specs=rows(ROW_W),
        out_shape=jax.ShapeDtypeStruct((t, ROW_W), F32),
        compiler_params=_params(("parallel",)),
        name="post_mix",
    )(x, og, od, w_out, g_cross.reshape(1, d), w_cq, kmem, vmem, w_co, g_ffn.reshape(1, d), w_r, b_r)


def _dispatch_body(fill_ref, dest_ref, src_ref, dst_ref, sem):
    n8 = DISPATCH_ROWS // SUBLANES
    i = pl.program_id(0)

    def dst_row(d):
        return dst_ref.at[lax.shift_right_logical(d, 3), pl.ds(d & (SUBLANES - 1), 1)]

    def issue(r8, carry):
        for u in range(SUBLANES):
            d = dest_ref[0, 0, r8 * SUBLANES + u]
            pltpu.make_async_copy(src_ref.at[r8, pl.ds(u, 1)], dst_row(d), sem.at[0]).start(priority=u % 2)
        return carry

    lax.fori_loop(0, n8, issue, 0)

    def fill_padding(start):
        def go(cp):
            cp.start() if start else cp.wait()

        def fill_tiles(t0, nt):
            go(pltpu.make_async_copy(src_ref.at[pl.ds(0, nt)], dst_ref.at[pl.ds(t0, nt)], sem.at[1]))

        def per_class(c, carry):
            lo = fill_ref[3 * c]
            mid = fill_ref[3 * c + 1]
            hi = fill_ref[3 * c + 2]
            lax.fori_loop(lo, mid, lambda r, cc: (go(pltpu.make_async_copy(src_ref.at[0, pl.ds(0, 1)], dst_row(r),
                                                                             sem.at[1])), cc)[1], 0)
            tiles = lax.shift_right_logical(hi - mid, 3)
            off = lax.shift_right_logical(mid, 3)
            for bit in reversed(range((MOE_ROWS // SUBLANES).bit_length() - 1)):
                has = (tiles & (1 << bit)) != 0

                @pl.when(has)
                def _(off=off, bit=bit):
                    fill_tiles(off, 1 << bit)

                off = off + jnp.where(has, 1 << bit, 0)
            return carry

        lax.fori_loop(0, N_CLASSES, per_class, 0)
        blk8 = MOE_ROWS // SUBLANES
        n_blocks = dst_ref.shape[0] // blk8
        lax.fori_loop(fill_ref[3 * N_CLASSES], n_blocks, lambda b, cc: (fill_tiles(b * blk8, blk8), cc)[1], 0)

    @pl.when(i == 0)
    def _():
        fill_padding(True)
        fill_padding(False)

    pltpu.make_async_copy(src_ref, dst_ref.at[pl.ds(0, n8)], sem.at[0]).wait()


def _dispatch(src, dest, fill, n_rows):
    t, w = src.shape
    assert t % DISPATCH_ROWS == 0 and DISPATCH_ROWS >= MOE_ROWS and SUBLANES == 8
    nb = t // DISPATCH_ROWS
    n8 = DISPATCH_ROWS // SUBLANES
    grid_spec = pltpu.PrefetchScalarGridSpec(
        num_scalar_prefetch=1,
        grid=(nb,),
        in_specs=[
            pl.BlockSpec((1, 1, DISPATCH_ROWS), lambda i, f: (i, 0, 0), memory_space=pltpu.SMEM),
            pl.BlockSpec((n8, SUBLANES, w), lambda i, f: (i, 0, 0)),
        ],
        out_specs=pl.BlockSpec(memory_space=pl.ANY),
        scratch_shapes=[pltpu.SemaphoreType.DMA((2,))],
    )
    out = pl.pallas_call(
        _dispatch_body,
        grid_spec=grid_spec,
        out_shape=jax.ShapeDtypeStruct((n_rows // SUBLANES, SUBLANES, w), src.dtype),
        compiler_params=_params(("arbitrary",)),
        name="dispatch",
    )(fill, dest.reshape(nb, 1, DISPATCH_ROWS), src.reshape(t // SUBLANES, SUBLANES, w))
    return out.reshape(n_rows, w)


def _expert_body(lo_ref, hi_ref, nvalid_ref, xs_ref, gf_ref, wg_a, wu_a, wd_a, wg_b, wu_b, wd_b, ys_ref, up_bf, dn_bf):
    i = pl.program_id(0)
    valid = i < nvalid_ref[0]
    prev = jnp.maximum(i - 1, 0)
    new_pair = (i == 0) | (lo_ref[i] != lo_ref[prev]) | (hi_ref[i] != hi_ref[prev])

    @pl.when(valid & new_pair)
    def _():
        for slot, w in enumerate((wg_a, wu_a, wg_b, wu_b)):
            up_bf[slot] = w[...].astype(BF16)
        for slot, w in enumerate((wd_a, wd_b)):
            dn_bf[slot] = w[...].astype(BF16)

    @pl.when(valid)
    def _():
        xb = xs_ref[...]
        h = _rms(xb[:, 0:D_MODEL], gf_ref[...]).astype(BF16)

        gate_a, up_a = _dot(h, up_bf[0]), _dot(h, up_bf[1])
        gate_b, up_b = _dot(h, up_bf[2]), _dot(h, up_bf[3])
        hid_a = (jax.nn.silu(gate_a) * up_a).astype(BF16)
        hid_b = (jax.nn.silu(gate_b) * up_b).astype(BF16)
        y_lo = xb[:, D_MODEL:D_MODEL + 1] * _dot(hid_a, dn_bf[0])
        ys_ref[...] = y_lo + xb[:, D_MODEL + 1:D_MODEL + 2] * _dot(hid_b, dn_bf[1])

    @pl.when(i >= nvalid_ref[0])
    def _():
        ys_ref[...] = jnp.zeros_like(ys_ref)


def _experts(xs, g_ffn, blk_lo, blk_hi, nvalid, w_g, w_u, w_d):
    n_rows = xs.shape[0]
    nb = n_rows // MOE_ROWS
    d, de = w_g.shape[1], w_g.shape[2]
    up = lambda sel: pl.BlockSpec((None, d, de), lambda i, lo, hi, nv: (sel(lo, hi)[i], 0, 0))
    down = lambda sel: pl.BlockSpec((None, de, d), lambda i, lo, hi, nv: (sel(lo, hi)[i], 0, 0))
    first = lambda lo, hi: lo
    second = lambda lo, hi: hi
    grid_spec = pltpu.PrefetchScalarGridSpec(
        num_scalar_prefetch=3,
        grid=(nb,),
        in_specs=[
            pl.BlockSpec((MOE_ROWS, ROW_W), lambda i, lo, hi, nv: (jnp.minimum(i, nv[0] - 1), 0)),
            pl.BlockSpec((1, d), lambda i, lo, hi, nv: (0, 0)),
            up(first), up(first), down(first), up(second), up(second), down(second),
        ],
        out_specs=pl.BlockSpec((MOE_ROWS, d), lambda i, lo, hi, nv: (i, 0)),
        scratch_shapes=[pltpu.VMEM((4, d, de), BF16), pltpu.VMEM((2, de, d), BF16)],
    )
    return pl.pallas_call(
        _expert_body,
        grid_spec=grid_spec,
        out_shape=jax.ShapeDtypeStruct((n_rows, d), F32),
        compiler_params=_params(("arbitrary",)),
        name="experts",
    )(blk_lo, blk_hi, nvalid, xs, g_ffn.reshape(1, d), w_g, w_u, w_d, w_g, w_u, w_d)


def _combine_body(cur_ref, nxt_ref, x2_ref, ys_ref, g_ref, o_ref, ybuf, sem):
    n8 = COMBINE_ROWS // SUBLANES
    i = pl.program_id(0)
    last = pl.num_programs(0) - 1
    slot = i % 2

    def issue_tile(idx_ref, s):
        def issue(r8, carry):
            for u in range(SUBLANES):
                d = idx_ref[0, 0, r8 * SUBLANES + u]
                src = ys_ref.at[lax.shift_right_logical(d, 3), pl.ds(d & (SUBLANES - 1), 1)]
                pltpu.make_async_copy(src, ybuf.at[s, r8, pl.ds(u, 1)], sem.at[s]).start(priority=u % 2)
            return carry

        lax.fori_loop(0, n8, issue, 0)

    @pl.when(i == 0)
    def _():
        issue_tile(cur_ref, 0)

    @pl.when(i < last)
    def _():
        issue_tile(nxt_ref, 1 - slot)

    pltpu.make_async_copy(ys_ref.at[pl.ds(0, n8)], ybuf.at[slot], sem.at[slot]).wait()
    y = ybuf[slot].reshape(COMBINE_ROWS, D_MODEL)
    o_ref[...] = _rms(x2_ref[...] + y, g_ref[...])


def _combine(x2e, ys, dest, g_final):
    t = x2e.shape[0]
    d = D_MODEL
    tm = COMBINE_ROWS
    nt = t // tm
    idx = dest.reshape(nt, 1, tm)
    return pl.pallas_call(
        _combine_body,
        grid=(nt,),
        in_specs=[
            pl.BlockSpec((1, 1, tm), lambda i: (i, 0, 0), memory_space=pltpu.SMEM),
            pl.BlockSpec((1, 1, tm), lambda i: (jnp.minimum(i + 1, nt - 1), 0, 0), memory_space=pltpu.SMEM),
            pl.BlockSpec((tm, d), lambda i: (i, 0)),
            pl.BlockSpec(memory_space=pl.ANY),
            pl.BlockSpec((1, d), lambda i: (0, 0)),
        ],
        out_specs=pl.BlockSpec((tm, d), lambda i: (i, 0)),
        out_shape=jax.ShapeDtypeStruct((t, d), F32),
        scratch_shapes=[pltpu.VMEM((2, tm // SUBLANES, SUBLANES, d), F32), pltpu.SemaphoreType.DMA((2,))],
        compiler_params=_params(("arbitrary",)),
        name="combine",
    )(idx, idx, x2e, ys.reshape(ys.shape[0] // SUBLANES, SUBLANES, d), g_final.reshape(1, d))


def _dispatch_plan(cls, t):
    onehot = (cls[:, None] == jnp.arange(N_CLASSES, dtype=jnp.int32)[None, :]).astype(jnp.int32)
    ranks = jnp.cumsum(onehot, axis=0) - onehot
    counts = jnp.sum(onehot, axis=0)
    padded = ((counts + MOE_ROWS - 1) // MOE_ROWS) * MOE_ROWS
    pad_end = jnp.cumsum(padded)
    pad_start = pad_end - padded
    dest = jnp.sum(onehot * (ranks + pad_start[None, :]), axis=1).astype(jnp.int32)
    fill_lo = pad_start + counts
    fill_mid = jnp.minimum(((fill_lo + SUBLANES - 1) // SUBLANES) * SUBLANES, pad_end)
    nvalid = (pad_end[-1] // MOE_ROWS).astype(jnp.int32).reshape(1)
    fill = jnp.concatenate([jnp.stack([fill_lo, fill_mid, pad_end], axis=1).reshape(-1), nvalid]).astype(jnp.int32)
    n_blocks = t // MOE_ROWS + N_CLASSES
    blk_start = jnp.arange(n_blocks, dtype=jnp.int32) * MOE_ROWS
    blk_cls = jnp.minimum(jnp.sum((blk_start[:, None] >= pad_end[None, :]).astype(jnp.int32), axis=1), N_CLASSES - 1)
    grp = blk_cls // N_PAIRS
    pair_onehot = ((blk_cls % N_PAIRS)[:, None] == jnp.arange(N_PAIRS, dtype=jnp.int32)[None, :]).astype(jnp.int32)
    blk_lo = grp * EXPERTS_PER_GROUP + jnp.sum(pair_onehot * jnp.asarray(PAIR_LO, jnp.int32)[None, :], axis=1)
    blk_hi = grp * EXPERTS_PER_GROUP + jnp.sum(pair_onehot * jnp.asarray(PAIR_HI, jnp.int32)[None, :], axis=1)
    return dest, fill, blk_lo.astype(jnp.int32), blk_hi.astype(jnp.int32), nvalid, n_blocks * MOE_ROWS


def _regroup_w_in(w_in):
    splits = np.cumsum([GLA_QK_W, GLA_QK_W, GLA_V_W, GLA_RANK, GLA_V_W, DIFF_QK_W, DIFF_QK_W, DIFF_V_W])[:-1]
    q_g, k_g, v_g, a_lr, r_g, q_d, k_d, v_d = jnp.split(w_in, [int(i) for i in splits], axis=1)
    a_pad = jnp.pad(a_lr, ((0, 0), (0, A_PAD - GLA_RANK)))
    q_d = q_d * (DIFF_DQK ** -0.5 * math.log2(math.e))
    return jnp.concatenate([q_g, k_g, v_g, r_g, a_pad, q_d, k_d], axis=1).astype(BF16), v_d.T.astype(BF16)


def kernel(x, mem, norm_mix_g, w_in, w_gla_a2, b_gla_a, gla_norm_g, diff_norm_g, lambda_q1, lambda_k1, lambda_q2,
           lambda_k2, w_out, norm_cross_g, norm_mem_g, w_cq, w_ckv, w_co, norm_ffn_g, w_router_grp, b_router_grp,
           w_router_exp, b_router_exp, w_e_gate, w_e_up, w_e_down, norm_final_g):
    b_, s_, d_ = x.shape
    m_ = mem.shape[1]
    t = b_ * s_
    xf = x.reshape(t, d_)

    kmem, vmem = _norm_matmul(mem.reshape(b_ * m_, d_), norm_mem_g[0], w_ckv[0].astype(BF16), (d_, d_), "mem_kv")

    w_cols, w_vt = _regroup_w_in(w_in[0])
    gla_in, dqk, v_t = _in_proj(xf, norm_mix_g[0], w_cols, w_vt)

    w_a2p = jnp.pad(w_gla_a2[0], ((0, A_PAD - GLA_RANK), (0, 0))).astype(BF16)
    o_g = _gla(gla_in, w_a2p, b_gla_a[0], gla_norm_g[0], b_, s_)

    lam_params = jnp.stack([lambda_q1[0], lambda_k1[0], lambda_q2[0], lambda_k2[0]]).astype(F32)
    o_d = _diff_attn(dqk, v_t, lam_params, diff_norm_g[0], b_, s_)

    pad_r = LANES - N_EXPERTS - N_GROUPS
    w_r = jnp.pad(jnp.concatenate([w_router_exp[0], w_router_grp[0]], axis=1).T, ((0, pad_r), (0, 0))).astype(BF16)
    b_r = jnp.pad(jnp.concatenate([b_router_exp[0], b_router_grp[0]]), (0, pad_r)).reshape(LANES, 1)
    x2e = _post_mix(xf, o_g, o_d, w_out[0].astype(BF16), norm_cross_g[0], w_cq[0].astype(BF16), kmem, vmem,
                    w_co[0].astype(BF16), norm_ffn_g[0], w_r, b_r, s_, m_)

    cls = x2e[:, D_MODEL + 2].astype(jnp.int32)
    dest, fill, blk_lo, blk_hi, nvalid, n_rows = _dispatch_plan(cls, t)
    xs = _dispatch(x2e, dest, fill, n_rows)
    ys = _experts(xs, norm_ffn_g[0], blk_lo, blk_hi, nvalid, w_e_gate[0], w_e_up[0], w_e_down[0])

    out = _combine(x2e, ys, dest, norm_final_g)
    return out.reshape(b_, s_, d_)
```

```python
import math

import jax
import jax.numpy as jnp
import numpy as np
from jax import lax
from jax.experimental import pallas as pl
from jax.experimental.pallas import tpu as pltpu

F32 = jnp.float32
BF16 = jnp.bfloat16

EPS = 1e-6
D_MODEL = 1024
GLA_HEADS = 4
GLA_DK = 64
GLA_DV = 128
GLA_RANK = 16
GLA_TAU = 16.0
GLA_CHUNK = 64
GLA_QK_W = GLA_HEADS * GLA_DK
GLA_V_W = GLA_HEADS * GLA_DV
DIFF_HEADS = 4
DIFF_DQK = 64
DIFF_DV = 128
DIFF_QK_W = DIFF_HEADS * 2 * DIFF_DQK
DIFF_V_W = DIFF_HEADS * DIFF_DV
CROSS_HEADS = 4
CROSS_DH = D_MODEL // CROSS_HEADS
N_GROUPS = 4
EXPERTS_PER_GROUP = 4
N_EXPERTS = N_GROUPS * EXPERTS_PER_GROUP
LAMBDA_INIT = 0.8 - 0.6 * math.exp(-0.3 * 0)

LANES = 128
SUBLANES = 8
A_PAD = LANES
GLA_IN_W = 2 * GLA_QK_W + 2 * GLA_V_W + A_PAD
ROW_W = D_MODEL + LANES
PAIR_LO = (0, 0, 0, 1, 1, 2)
PAIR_HI = (1, 2, 3, 2, 3, 3)
N_PAIRS = len(PAIR_LO)
N_CLASSES = N_GROUPS * N_PAIRS

ROW_TILE = 512
MIX_TILE = 1024
SUB_TILE = 512
GLA_ROWS = 2048
ATT_TILE = 512
ATT_ONES = 16
MOE_ROWS = 512
DISPATCH_ROWS = 4096
COMBINE_ROWS = 1024
VMEM_LIMIT = 56 * 1024 * 1024


def _params(sem):
    return pltpu.CompilerParams(dimension_semantics=sem, vmem_limit_bytes=VMEM_LIMIT)


def _rms(x, g):
    ms = jnp.mean(x * x, axis=-1, keepdims=True)
    return x * lax.rsqrt(ms + EPS) * g


def _dot(a, b):
    return jnp.dot(a, b, preferred_element_type=F32)


def _dot_nt(a, b):
    return lax.dot_general(a, b, (((1,), (1,)), ((), ())), preferred_element_type=F32)


def _norm_matmul_body(x_ref, g_ref, w_ref, *o_refs):
    h = _rms(x_ref[...], g_ref[...]).astype(BF16)
    off = 0
    for o_ref in o_refs:
        n = o_ref.shape[-1]
        o_ref[...] = _dot(h, w_ref[:, off:off + n]).astype(o_ref.dtype)
        off += n


def _norm_matmul(x, g, w, splits, name):
    t, d = x.shape
    n = w.shape[1]
    assert sum(splits) == n and t % ROW_TILE == 0
    return pl.pallas_call(
        _norm_matmul_body,
        grid=(t // ROW_TILE,),
        in_specs=[
            pl.BlockSpec((ROW_TILE, d), lambda i: (i, 0)),
            pl.BlockSpec((1, d), lambda i: (0, 0)),
            pl.BlockSpec((d, n), lambda i: (0, 0)),
        ],
        out_specs=[pl.BlockSpec((ROW_TILE, s), lambda i: (i, 0)) for s in splits],
        out_shape=[jax.ShapeDtypeStruct((t, s), BF16) for s in splits],
        compiler_params=_params(("parallel",)),
        name=name,
    )(x, g.reshape(1, d), w)


def _in_proj_body(x_ref, g_ref, w_ref, wvt_ref, gla_ref, dqk_ref, vt_ref):
    h = _rms(x_ref[...], g_ref[...]).astype(BF16)
    gla_ref[...] = _dot(h, w_ref[:, 0:GLA_IN_W]).astype(gla_ref.dtype)
    dqk_ref[...] = _dot(h, w_ref[:, GLA_IN_W:]).astype(dqk_ref.dtype)
    vt_ref[...] = _dot_nt(wvt_ref[...], h).astype(vt_ref.dtype)


def _in_proj(x, g, w, w_vt):
    t, d = x.shape
    n = w.shape[1]
    tm = ROW_TILE
    assert n == GLA_IN_W + 2 * DIFF_QK_W and t % tm == 0
    return pl.pallas_call(
        _in_proj_body,
        grid=(t // tm,),
        in_specs=[
            pl.BlockSpec((tm, d), lambda i: (i, 0)),
            pl.BlockSpec((1, d), lambda i: (0, 0)),
            pl.BlockSpec((d, n), lambda i: (0, 0)),
            pl.BlockSpec((DIFF_V_W, d), lambda i: (0, 0)),
        ],
        out_specs=[
            pl.BlockSpec((tm, GLA_IN_W), lambda i: (i, 0)),
            pl.BlockSpec((tm, 2 * DIFF_QK_W), lambda i: (i, 0)),
            pl.BlockSpec((DIFF_V_W, tm), lambda i: (0, i)),
        ],
        out_shape=[
            jax.ShapeDtypeStruct((t, GLA_IN_W), BF16),
            jax.ShapeDtypeStruct((t, 2 * DIFF_QK_W), BF16),
            jax.ShapeDtypeStruct((DIFF_V_W, t), BF16),
        ],
        compiler_params=_params(("parallel",)),
        name="in_proj",
    )(x, g.reshape(1, d), w, w_vt)


def _chunk_prefix_operator():
    i = np.arange(2 * GLA_CHUNK)
    same_chunk = (i[:, None] // GLA_CHUNK) == (i[None, :] // GLA_CHUNK)
    return jnp.asarray(same_chunk & (i[None, :] <= i[:, None]), dtype=BF16)


def _gla_body(in_ref, tri_ref, wa_ref, ba_ref, gn_ref, o_ref, s_ref):
    c_ = GLA_CHUNK

    @pl.when(pl.program_id(1) == 0)
    def _():
        s_ref[...] = jnp.zeros_like(s_ref)

    q0, k0, v0, r0, a0 = 0, GLA_QK_W, 2 * GLA_QK_W, 2 * GLA_QK_W + GLA_V_W, 2 * GLA_QK_W + 2 * GLA_V_W
    g_ = 2 * c_
    n_g = GLA_ROWS // g_
    tri = tri_ref[...]
    causal = (lax.broadcasted_iota(jnp.int32, (c_, c_), 0) >= lax.broadcasted_iota(jnp.int32, (c_, c_), 1))
    first_half = lax.broadcasted_iota(jnp.int32, (g_, LANES), 1) < GLA_DK
    gn = gn_ref[...]

    def decays(g):
        rows = slice(g * g_, (g + 1) * g_)
        z = _dot(in_ref[rows, a0:a0 + A_PAD], wa_ref[...]) + ba_ref[...]
        log_a = jax.nn.log_sigmoid(z) / GLA_TAU
        la_hi = log_a.astype(BF16)
        la_lo = (log_a - la_hi.astype(F32)).astype(BF16)
        cum = _dot(tri, la_hi) + _dot(tri, la_lo)
        q = in_ref[rows, q0:q0 + GLA_QK_W].astype(F32)
        k = in_ref[rows, k0:k0 + GLA_QK_W].astype(F32)
        q_e = (q * (GLA_DK ** -0.5) * jnp.exp(cum)).astype(BF16)
        k_e = (k * jnp.exp(-cum)).astype(BF16)
        tots = [cum[(c + 1) * c_ - 1:(c + 1) * c_, :] for c in range(2)]
        k_s = jnp.concatenate([(k[c * c_:(c + 1) * c_] * jnp.exp(tots[c] - cum[c * c_:(c + 1) * c_])).astype(BF16)
                               for c in range(2)], axis=0)
        return q_e, k_e, k_s, tots

    def local_products(g, gates):
        q_e, k_e, k_s, _ = gates
        rows_g = slice(g * g_, (g + 1) * g_)
        out = []
        for h in range(GLA_HEADS):
            tile = slice((h // 2) * LANES, (h // 2 + 1) * LANES)
            head_lanes = first_half if h % 2 == 0 else ~first_half
            qe_m = jnp.where(head_lanes, q_e[:, tile], jnp.zeros((), BF16))
            ke_t = k_e[:, tile]
            ks_t = k_s[:, tile]
            v_h = in_ref[rows_g, v0 + h * GLA_DV:v0 + (h + 1) * GLA_DV]
            vt_h = v_h.astype(F32).T.astype(BF16)
            per_chunk = []
            for c in range(2):
                rows = slice(c * c_, (c + 1) * c_)
                att = jnp.where(causal, _dot_nt(qe_m[rows], ke_t[rows]), 0.0).astype(BF16)
                chunk_lanes = first_half if c == 0 else ~first_half
                kv_t = _dot(jnp.where(chunk_lanes, vt_h, jnp.zeros((), BF16)), ks_t)
                per_chunk.append((_dot(att, v_h[rows]), kv_t, qe_m[rows]))
            out.append(per_chunk)
        return out

    gates = [decays(0)]
    local = []
    for g in range(n_g):
        if g + 1 < n_g:
            gates.append(decays(g + 1))
        local.append(local_products(g, gates[g]))

    for h in range(GLA_HEADS):
        tile = slice((h // 2) * LANES, (h // 2 + 1) * LANES)
        st = s_ref[h]
        s_prev = []
        for g in range(n_g):
            for c in range(2):
                s_prev.append(st.astype(BF16))
                st = st * jnp.exp(gates[g][3][c][:, tile]) + local[g][h][c][1]
        s_ref[h] = st
        for g in range(n_g):
            r_h = in_ref[g * g_:(g + 1) * g_, r0 + h * GLA_DV:r0 + (h + 1) * GLA_DV].astype(F32)
            for c in range(2):
                o_intra, _, qe_c = local[g][h][c]
                o_h = o_intra + _dot_nt(qe_c, s_prev[2 * g + c])
                out_rows = slice(g * g_ + c * c_, g * g_ + (c + 1) * c_)
                o_ref[out_rows, h * GLA_DV:(h + 1) * GLA_DV] = (
                    _rms(o_h, gn) * jax.nn.silu(r_h[c * c_:(c + 1) * c_])).astype(o_ref.dtype)


def _gla(gla_in, w_a2p, b_a, gn, batch, seq):
    t = gla_in.shape[0]
    assert seq % GLA_ROWS == 0 and GLA_ROWS % (2 * GLA_CHUNK) == 0 and 2 * GLA_DK == LANES and 2 * GLA_CHUNK == LANES
    nsb = seq // GLA_ROWS
    return pl.pallas_call(
        _gla_body,
        grid=(batch, nsb),
        in_specs=[
            pl.BlockSpec((GLA_ROWS, GLA_IN_W), lambda b, j: (b * nsb + j, 0)),
            pl.BlockSpec((2 * GLA_CHUNK, 2 * GLA_CHUNK), lambda b, j: (0, 0)),
            pl.BlockSpec((A_PAD, GLA_QK_W), lambda b, j: (0, 0)),
            pl.BlockSpec((1, GLA_QK_W), lambda b, j: (0, 0)),
            pl.BlockSpec((1, GLA_DV), lambda b, j: (0, 0)),
        ],
        out_specs=pl.BlockSpec((GLA_ROWS, GLA_V_W), lambda b, j: (b * nsb + j, 0)),
        out_shape=jax.ShapeDtypeStruct((t, GLA_V_W), BF16),
        scratch_shapes=[pltpu.VMEM((GLA_HEADS, GLA_DV, LANES), F32)],
        compiler_params=_params(("parallel", "arbitrary")),
        name="gla",
    )(gla_in, _chunk_prefix_operator(), w_a2p, b_a.reshape(1, -1), gn.reshape(1, -1))


def _alibi_features(seq):
    j = np.arange(seq) % ATT_TILE
    j_lo = j % 256
    j_hi = j - j_lo
    kf = np.zeros((seq, 2 * DIFF_DQK), np.float32)
    for base in (0, DIFF_DQK):
        for f in range(3):
            kf[:, base + f] = j_lo
            kf[:, base + 3 + f] = j_hi
    slopes = 2.0 ** (-8.0 * np.arange(1, DIFF_HEADS + 1) / DIFF_HEADS)
    c = jnp.asarray(slopes * math.log2(math.e), dtype=F32)
    c1 = c.astype(BF16)
    c2 = (c - c1.astype(F32)).astype(BF16)
    c3 = (c - c1.astype(F32) - c2.astype(F32)).astype(BF16)
    terms = jnp.stack([c1, c2, c3, c1, c2, c3], axis=1)
    qf = jnp.zeros((DIFF_HEADS, 2 * DIFF_DQK), BF16)
    qf = qf.at[:, 0:6].set(terms).at[:, DIFF_DQK:DIFF_DQK + 6].set(terms)
    qf = jnp.repeat(qf, 8, axis=0)
    return jnp.asarray(kf, dtype=BF16), qf, c


def _diff_body(c_ref, q_ref, k_ref, vt_ref, qf_ref, kf_ref, lam_ref, gn_ref, o_ref, kaug_ref, vta_ref):
    tq = ATT_TILE
    h = pl.program_id(1)
    seq = k_ref.shape[0]
    nq = seq // tq
    c = c_ref[h]

    k = k_ref[...]
    kf = kf_ref[...]
    lane_k = lax.broadcasted_iota(jnp.int32, k.shape, 1)
    kaug_ref[0] = jnp.where(lane_k < DIFF_DQK, k, kf)
    kaug_ref[1] = jnp.where(lane_k >= DIFF_DQK, k, kf)
    for jj in range(nq):
        vta_ref[jj, 0:DIFF_DV, :] = vt_ref[:, jj * tq:(jj + 1) * tq]
        vta_ref[jj, DIFF_DV:, :] = jnp.ones((ATT_ONES, tq), BF16)

    lq1, lk1, lq2, lk2 = (lam_ref[i:i + 1, :] for i in range(4))
    lam = (jnp.exp(jnp.sum(lq1 * lk1, axis=-1, keepdims=True))
           - jnp.exp(jnp.sum(lq2 * lk2, axis=-1, keepdims=True)) + LAMBDA_INIT)
    row = lax.broadcasted_iota(jnp.int32, (tq, tq), 0)
    col = lax.broadcasted_iota(jnp.int32, (tq, tq), 1)
    visible = row <= col
    lane = lax.broadcasted_iota(jnp.int32, (tq, 2 * DIFF_DQK), 1)
    qf = jnp.broadcast_to(qf_ref[0:1, :], (tq, 2 * DIFF_DQK))

    steps = [(qi, j) for qi in range(nq) for j in range(qi + 1)]
    q_maps = {}

    def scores(qi, j):
        if qi not in q_maps:
            q = q_ref[qi * tq:(qi + 1) * tq, :]
            q_maps[qi] = (jnp.where(lane < DIFF_DQK, q, qf), jnp.where(lane >= DIFF_DQK, q, qf))
        return [_dot_nt(kaug_ref[m, j * tq:(j + 1) * tq, :], q_maps[qi][m]) for m in range(2)]

    m_run = [None, None]
    acc = [None, None]
    s_next = scores(*steps[0])
    for t, (qi, j) in enumerate(steps):
        s_cur = s_next
        if t + 1 < len(steps):
            s_next = scores(*steps[t + 1])
        vblk = vta_ref[j]
        base = c * float(j * tq)
        for m in range(2):
            s_t = s_cur[m]
            if j == qi:
                s_t = jnp.where(visible, s_t, -jnp.inf)
            blk_max = jnp.max(s_t, axis=0, keepdims=True) + base
            if j == 0:
                m_new = blk_max
                acc[m] = _dot(vblk, jnp.exp2(s_t - (m_new - base)).astype(BF16))
            else:
                m_new = jnp.maximum(m_run[m], blk_max)
                alpha = jnp.exp2(m_run[m] - m_new)
                acc[m] = alpha * acc[m] + _dot(vblk, jnp.exp2(s_t - (m_new - base)).astype(BF16))
            m_run[m] = m_new
        if j == qi:
            o_t = (acc[0][0:DIFF_DV] / acc[0][DIFF_DV:DIFF_DV + 1]
                   - lam * (acc[1][0:DIFF_DV] / acc[1][DIFF_DV:DIFF_DV + 1]))
            o = o_t.T
            o_ref[qi * tq:(qi + 1) * tq, :] = (_rms(o, gn_ref[...]) * (1.0 - LAMBDA_INIT)).astype(o_ref.dtype)


def _diff_attn(dqk, v_t, lam_params, gn, batch, seq):
    t = dqk.shape[0]
    tq = ATT_TILE
    assert seq % tq == 0
    nq = seq // tq
    kfeat, qfeat, c = _alibi_features(seq)
    rows = DIFF_DV + ATT_ONES
    grid_spec = pltpu.PrefetchScalarGridSpec(
        num_scalar_prefetch=1,
        grid=(batch, DIFF_HEADS),
        in_specs=[
            pl.BlockSpec((seq, 2 * DIFF_DQK), lambda b, h, s: (b, h)),
            pl.BlockSpec((seq, 2 * DIFF_DQK), lambda b, h, s: (b, DIFF_HEADS + h)),
            pl.BlockSpec((DIFF_DV, seq), lambda b, h, s: (h, b)),
            pl.BlockSpec((8, 2 * DIFF_DQK), lambda b, h, s: (h, 0)),
            pl.BlockSpec((seq, 2 * DIFF_DQK), lambda b, h, s: (0, 0)),
            pl.BlockSpec((4, DIFF_DQK), lambda b, h, s: (0, 0)),
            pl.BlockSpec((1, DIFF_DV), lambda b, h, s: (0, 0)),
        ],
        out_specs=pl.BlockSpec((seq, DIFF_DV), lambda b, h, s: (b, h)),
        scratch_shapes=[
            pltpu.VMEM((2, seq, 2 * DIFF_DQK), BF16),
            pltpu.VMEM((nq, rows, tq), BF16),
        ],
    )
    return pl.pallas_call(
        _diff_body,
        grid_spec=grid_spec,
        out_shape=jax.ShapeDtypeStruct((t, DIFF_V_W), BF16),
        compiler_params=_params(("parallel", "parallel")),
        name="diff_attn",
    )(c, dqk, dqk, v_t, qfeat, kfeat, lam_params, gn.reshape(1, -1))


def _route(logits):
    lane = lax.broadcasted_iota(jnp.int32, logits.shape, 0)
    big = jnp.int32(LANES)
    neg = -jnp.inf

    def first_argmax(vals, vmax):
        return jnp.min(jnp.where(vals == vmax, lane, big), axis=0, keepdims=True)

    is_grp = (lane >= N_EXPERTS) & (lane < N_EXPERTS + N_GROUPS)
    lg = jnp.where(is_grp, logits, neg)
    mg = jnp.max(lg, axis=0, keepdims=True)
    p_g = 1.0 / jnp.sum(jnp.exp(lg - mg), axis=0, keepdims=True)
    g_sel = first_argmax(lg, mg) - N_EXPERTS
    in_grp = (lane >= g_sel * EXPERTS_PER_GROUP) & (lane < (g_sel + 1) * EXPERTS_PER_GROUP)
    le = jnp.where(in_grp, logits, neg)
    m1 = jnp.max(le, axis=0, keepdims=True)
    i1 = first_argmax(le, m1)
    le2 = jnp.where(lane == i1, neg, le)
    m2 = jnp.max(le2, axis=0, keepdims=True)
    i2 = first_argmax(le2, m2)
    den = jnp.sum(jnp.exp(le - m1), axis=0, keepdims=True)
    w1 = 1.0 / den
    w2 = jnp.exp(m2 - m1) / den
    wsum = w1 + w2
    g1 = p_g * w1 / wsum
    g2 = p_g * w2 / wsum
    first_is_lo = i1 < i2
    g_lo = jnp.where(first_is_lo, g1, g2)
    g_hi = jnp.where(first_is_lo, g2, g1)
    a = jnp.minimum(i1, i2) - g_sel * EXPERTS_PER_GROUP
    b = jnp.maximum(i1, i2) - g_sel * EXPERTS_PER_GROUP
    pair = 3 * a - jnp.where(a == 2, 1, 0) + (b - a - 1)
    cls = (g_sel * N_PAIRS + pair).astype(F32)
    out = jnp.where(lane == 0, g_lo, jnp.where(lane == 1, g_hi, cls))
    return jnp.where(lane < 3, out, 0.0).T


def _post_mix_body(x_ref, og_ref, od_ref, wo_ref, gc_ref, wq_ref, km_ref, vm_ref, wco_ref, gf_ref, wr_ref, br_ref,
                   x2e_ref):
    tiles = [slice(j * SUB_TILE, (j + 1) * SUB_TILE) for j in range(MIX_TILE // SUB_TILE)]

    def out_proj(rows):
        return x_ref[rows] + _dot(jnp.concatenate([og_ref[rows], od_ref[rows]], axis=1), wo_ref[...])

    def cross_query(x1):
        return _dot(_rms(x1, gc_ref[...]).astype(BF16), wq_ref[...]).astype(BF16)

    def cross_attend(qc):
        heads = []
        for h in range(CROSS_HEADS):
            sl = slice(h * CROSS_DH, (h + 1) * CROSS_DH)
            s = _dot_nt(qc[:, sl], km_ref[:, sl]) * (CROSS_DH ** -0.5)
            s = s - jnp.max(s, axis=-1, keepdims=True)
            p = jnp.exp(s)
            p = p / jnp.sum(p, axis=-1, keepdims=True)
            heads.append(_dot(p.astype(BF16), vm_ref[:, sl]).astype(BF16))
        return jnp.concatenate(heads, axis=-1)

    def finish(rows, x2):
        h3 = _rms(x2, gf_ref[...]).astype(BF16)
        x2e_ref[rows, 0:D_MODEL] = x2
        x2e_ref[rows, D_MODEL:] = _route(_dot_nt(wr_ref[...], h3) + br_ref[...])

    x1 = [out_proj(rows) for rows in tiles]
    qc = [cross_query(v) for v in x1]
    oc = [cross_attend(v) for v in qc]
    x2 = [a + _dot(b, wco_ref[...]) for a, b in zip(x1, oc)]
    for rows, v in zip(tiles, x2):
        finish(rows, v)


def _post_mix(x, og, od, w_out, g_cross, w_cq, kmem, vmem, w_co, g_ffn, w_r, b_r, seq, mem_len):
    t, d = x.shape
    tm = MIX_TILE
    assert seq % tm == 0
    per_b = seq // tm
    full = lambda shape: pl.BlockSpec(shape, lambda i: (0, 0), pipeline_mode=pl.Buffered(1))
    rows = lambda w: pl.BlockSpec((tm, w), lambda i: (i, 0))
    return pl.pallas_call(
        _post_mix_body,
        grid=(t // tm,),
        in_specs=[
            rows(d), rows(GLA_V_W), rows(DIFF_V_W),
            full((d, d)), full((1, d)), full((d, d)),
            pl.BlockSpec((mem_len, d), lambda i: (i // per_b, 0)),
            pl.BlockSpec((mem_len, d), lambda i: (i // per_b, 0)),
            full((d, d)), full((1, d)), full((LANES, d)), full((LANES, 1)),
        ],
        out_specs=rows(ROW_W),
        out_shape=jax.ShapeDtypeStruct((t, ROW_W), F32),
        compiler_params=_params(("parallel",)),
        name="post_mix",
    )(x, og, od, w_out, g_cross.reshape(1, d), w_cq, kmem, vmem, w_co, g_ffn.reshape(1, d), w_r, b_r)


def _dispatch_body(fill_ref, dest_ref, src_ref, dst_ref, sem):
    n8 = DISPATCH_ROWS // SUBLANES
    i = pl.program_id(0)

    def dst_row(d):
        return dst_ref.at[lax.shift_right_logical(d, 3), pl.ds(d & (SUBLANES - 1), 1)]

    def issue(r8, carry):
        for u in range(SUBLANES):
            d = dest_ref[0, 0, r8 * SUBLANES + u]
            pltpu.make_async_copy(src_ref.at[r8, pl.ds(u, 1)], dst_row(d), sem.at[0]).start(priority=u % 2)
        return carry

    lax.fori_loop(0, n8, issue, 0)

    def fill_padding(start):
        def go(cp):
            cp.start() if start else cp.wait()

        def fill_tiles(t0, nt):
            go(pltpu.make_async_copy(src_ref.at[pl.ds(0, nt)], dst_ref.at[pl.ds(t0, nt)], sem.at[1]))

        def per_class(c, carry):
            lo = fill_ref[3 * c]
            mid = fill_ref[3 * c + 1]
            hi = fill_ref[3 * c + 2]
            lax.fori_loop(lo, mid, lambda r, cc: (go(pltpu.make_async_copy(src_ref.at[0, pl.ds(0, 1)], dst_row(r),
                                                                             sem.at[1])), cc)[1], 0)
            tiles = lax.shift_right_logical(hi - mid, 3)
            off = lax.shift_right_logical(mid, 3)
            for bit in reversed(range((MOE_ROWS // SUBLANES).bit_length() - 1)):
                has = (tiles & (1 << bit)) != 0

                @pl.when(has)
                def _(off=off, bit=bit):
                    fill_tiles(off, 1 << bit)

                off = off + jnp.where(has, 1 << bit, 0)
            return carry

        lax.fori_loop(0, N_CLASSES, per_class, 0)
        blk8 = MOE_ROWS // SUBLANES
        n_blocks = dst_ref.shape[0] // blk8
        lax.fori_loop(fill_ref[3 * N_CLASSES], n_blocks, lambda b, cc: (fill_tiles(b * blk8, blk8), cc)[1], 0)

    @pl.when(i == 0)
    def _():
        fill_padding(True)
        fill_padding(False)

    pltpu.make_async_copy(src_ref, dst_ref.at[pl.ds(0, n8)], sem.at[0]).wait()


def _dispatch(src, dest, fill, n_rows):
    t, w = src.shape
    assert t % DISPATCH_ROWS == 0 and DISPATCH_ROWS >= MOE_ROWS and SUBLANES == 8
    nb = t // DISPATCH_ROWS
    n8 = DISPATCH_ROWS // SUBLANES
    grid_spec = pltpu.PrefetchScalarGridSpec(
        num_scalar_prefetch=1,
        grid=(nb,),
        in_specs=[
            pl.BlockSpec((1, 1, DISPATCH_ROWS), lambda i, f: (i, 0, 0), memory_space=pltpu.SMEM),
            pl.BlockSpec((n8, SUBLANES, w), lambda i, f: (i, 0, 0)),
        ],
        out_specs=pl.BlockSpec(memory_space=pl.ANY),
        scratch_shapes=[pltpu.SemaphoreType.DMA((2,))],
    )
    out = pl.pallas_call(
        _dispatch_body,
        grid_spec=grid_spec,
        out_shape=jax.ShapeDtypeStruct((n_rows // SUBLANES, SUBLANES, w), src.dtype),
        compiler_params=_params(("arbitrary",)),
        name="dispatch",
    )(fill, dest.reshape(nb, 1, DISPATCH_ROWS), src.reshape(t // SUBLANES, SUBLANES, w))
    return out.reshape(n_rows, w)


def _expert_body(lo_ref, hi_ref, nvalid_ref, xs_ref, gf_ref, wg_a, wu_a, wd_a, wg_b, wu_b, wd_b, ys_ref, up_bf, dn_bf):
    i = pl.program_id(0)
    valid = i < nvalid_ref[0]
    prev = jnp.maximum(i - 1, 0)
    new_pair = (i == 0) | (lo_ref[i] != lo_ref[prev]) | (hi_ref[i] != hi_ref[prev])

    @pl.when(valid & new_pair)
    def _():
        for slot, w in enumerate((wg_a, wu_a, wg_b, wu_b)):
            up_bf[slot] = w[...].astype(BF16)
        for slot, w in enumerate((wd_a, wd_b)):
            dn_bf[slot] = w[...].astype(BF16)

    @pl.when(valid)
    def _():
        xb = xs_ref[...]
        h = _rms(xb[:, 0:D_MODEL], gf_ref[...]).astype(BF16)

        gate_a, up_a = _dot(h, up_bf[0]), _dot(h, up_bf[1])
        gate_b, up_b = _dot(h, up_bf[2]), _dot(h, up_bf[3])
        hid_a = (jax.nn.silu(gate_a) * up_a).astype(BF16)
        hid_b = (jax.nn.silu(gate_b) * up_b).astype(BF16)
        y_lo = xb[:, D_MODEL:D_MODEL + 1] * _dot(hid_a, dn_bf[0])
        ys_ref[...] = y_lo + xb[:, D_MODEL + 1:D_MODEL + 2] * _dot(hid_b, dn_bf[1])

    @pl.when(i >= nvalid_ref[0])
    def _():
        ys_ref[...] = jnp.zeros_like(ys_ref)


def _experts(xs, g_ffn, blk_lo, blk_hi, nvalid, w_g, w_u, w_d):
    n_rows = xs.shape[0]
    nb = n_rows // MOE_ROWS
    d, de = w_g.shape[1], w_g.shape[2]
    up = lambda sel: pl.BlockSpec((None, d, de), lambda i, lo, hi, nv: (sel(lo, hi)[i], 0, 0))
    down = lambda sel: pl.BlockSpec((None, de, d), lambda i, lo, hi, nv: (sel(lo, hi)[i], 0, 0))
    first = lambda lo, hi: lo
    second = lambda lo, hi: hi
    grid_spec = pltpu.PrefetchScalarGridSpec(
        num_scalar_prefetch=3,
        grid=(nb,),
        in_specs=[
            pl.BlockSpec((MOE_ROWS, ROW_W), lambda i, lo, hi, nv: (jnp.minimum(i, nv[0] - 1), 0)),
            pl.BlockSpec((1, d), lambda i, lo, hi, nv: (0, 0)),
            up(first), up(first), down(first), up(second), up(second), down(second),
        ],
        out_specs=pl.BlockSpec((MOE_ROWS, d), lambda i, lo, hi, nv: (i, 0)),
        scratch_shapes=[pltpu.VMEM((4, d, de), BF16), pltpu.VMEM((2, de, d), BF16)],
    )
    return pl.pallas_call(
        _expert_body,
        grid_spec=grid_spec,
        out_shape=jax.ShapeDtypeStruct((n_rows, d), F32),
        compiler_params=_params(("arbitrary",)),
        name="experts",
    )(blk_lo, blk_hi, nvalid, xs, g_ffn.reshape(1, d), w_g, w_u, w_d, w_g, w_u, w_d)


def _combine_body(cur_ref, nxt_ref, x2_ref, ys_ref, g_ref, o_ref, ybuf, sem):
    n8 = COMBINE_ROWS // SUBLANES
    i = pl.program_id(0)
    last = pl.num_programs(0) - 1
    slot = i % 2

    def issue_tile(idx_ref, s):
        def issue(r8, carry):
            for u in range(SUBLANES):
                d = idx_ref[0, 0, r8 * SUBLANES + u]
                src = ys_ref.at[lax.shift_right_logical(d, 3), pl.ds(d & (SUBLANES - 1), 1)]
                pltpu.make_async_copy(src, ybuf.at[s, r8, pl.ds(u, 1)], sem.at[s]).start(priority=u % 2)
            return carry

        lax.fori_loop(0, n8, issue, 0)

    @pl.when(i == 0)
    def _():
        issue_tile(cur_ref, 0)

    @pl.when(i < last)
    def _():
        issue_tile(nxt_ref, 1 - slot)

    pltpu.make_async_copy(ys_ref.at[pl.ds(0, n8)], ybuf.at[slot], sem.at[slot]).wait()
    y = ybuf[slot].reshape(COMBINE_ROWS, D_MODEL)
    o_ref[...] = _rms(x2_ref[...] + y, g_ref[...])


def _combine(x2e, ys, dest, g_final):
    t = x2e.shape[0]
    d = D_MODEL
    tm = COMBINE_ROWS
    nt = t // tm
    idx = dest.reshape(nt, 1, tm)
    return pl.pallas_call(
        _combine_body,
        grid=(nt,),
        in_specs=[
            pl.BlockSpec((1, 1, tm), lambda i: (i, 0, 0), memory_space=pltpu.SMEM),
            pl.BlockSpec((1, 1, tm), lambda i: (jnp.minimum(i + 1, nt - 1), 0, 0), memory_space=pltpu.SMEM),
            pl.BlockSpec((tm, d), lambda i: (i, 0)),
            pl.BlockSpec(memory_space=pl.ANY),
            pl.BlockSpec((1, d), lambda i: (0, 0)),
        ],
        out_specs=pl.BlockSpec((tm, d), lambda i: (i, 0)),
        out_shape=jax.ShapeDtypeStruct((t, d), F32),
        scratch_shapes=[pltpu.VMEM((2, tm // SUBLANES, SUBLANES, d), F32), pltpu.SemaphoreType.DMA((2,))],
        compiler_params=_params(("arbitrary",)),
        name="combine",
    )(idx, idx, x2e, ys.reshape(ys.shape[0] // SUBLANES, SUBLANES, d), g_final.reshape(1, d))


def _dispatch_plan(cls, t):
    onehot = (cls[:, None] == jnp.arange(N_CLASSES, dtype=jnp.int32)[None, :]).astype(jnp.int32)
    ranks = jnp.cumsum(onehot, axis=0) - onehot
    counts = jnp.sum(onehot, axis=0)
    padded = ((counts + MOE_ROWS - 1) // MOE_ROWS) * MOE_ROWS
    pad_end = jnp.cumsum(padded)
    pad_start = pad_end - padded
    dest = jnp.sum(onehot * (ranks + pad_start[None, :]), axis=1).astype(jnp.int32)
    fill_lo = pad_start + counts
    fill_mid = jnp.minimum(((fill_lo + SUBLANES - 1) // SUBLANES) * SUBLANES, pad_end)
    nvalid = (pad_end[-1] // MOE_ROWS).astype(jnp.int32).reshape(1)
    fill = jnp.concatenate([jnp.stack([fill_lo, fill_mid, pad_end], axis=1).reshape(-1), nvalid]).astype(jnp.int32)
    n_blocks = t // MOE_ROWS + N_CLASSES
    blk_start = jnp.arange(n_blocks, dtype=jnp.int32) * MOE_ROWS
    blk_cls = jnp.minimum(jnp.sum((blk_start[:, None] >= pad_end[None, :]).astype(jnp.int32), axis=1), N_CLASSES - 1)
    grp = blk_cls // N_PAIRS
    pair_onehot = ((blk_cls % N_PAIRS)[:, None] == jnp.arange(N_PAIRS, dtype=jnp.int32)[None, :]).astype(jnp.int32)
    blk_lo = grp * EXPERTS_PER_GROUP + jnp.sum(pair_onehot * jnp.asarray(PAIR_LO, jnp.int32)[None, :], axis=1)
    blk_hi = grp * EXPERTS_PER_GROUP + jnp.sum(pair_onehot * jnp.asarray(PAIR_HI, jnp.int32)[None, :], axis=1)
    return dest, fill, blk_lo.astype(jnp.int32), blk_hi.astype(jnp.int32), nvalid, n_blocks * MOE_ROWS


def _regroup_w_in(w_in):
    splits = np.cumsum([GLA_QK_W, GLA_QK_W, GLA_V_W, GLA_RANK, GLA_V_W, DIFF_QK_W, DIFF_QK_W, DIFF_V_W])[:-1]
    q_g, k_g, v_g, a_lr, r_g, q_d, k_d, v_d = jnp.split(w_in, [int(i) for i in splits], axis=1)
    a_pad = jnp.pad(a_lr, ((0, 0), (0, A_PAD - GLA_RANK)))
    q_d = q_d * (DIFF_DQK ** -0.5 * math.log2(math.e))
    return jnp.concatenate([q_g, k_g, v_g, r_g, a_pad, q_d, k_d], axis=1).astype(BF16), v_d.T.astype(BF16)


def kernel(x, mem, norm_mix_g, w_in, w_gla_a2, b_gla_a, gla_norm_g, diff_norm_g, lambda_q1, lambda_k1, lambda_q2,
           lambda_k2, w_out, norm_cross_g, norm_mem_g, w_cq, w_ckv, w_co, norm_ffn_g, w_router_grp, b_router_grp,
           w_router_exp, b_router_exp, w_e_gate, w_e_up, w_e_down, norm_final_g):
    b_, s_, d_ = x.shape
    m_ = mem.shape[1]
    t = b_ * s_
    xf = x.reshape(t, d_)

    kmem, vmem = _norm_matmul(mem.reshape(b_ * m_, d_), norm_mem_g[0], w_ckv[0].astype(BF16), (d_, d_), "mem_kv")

    w_cols, w_vt = _regroup_w_in(w_in[0])
    gla_in, dqk, v_t = _in_proj(xf, norm_mix_g[0], w_cols, w_vt)

    w_a2p = jnp.pad(w_gla_a2[0], ((0, A_PAD - GLA_RANK), (0, 0))).astype(BF16)
    o_g = _gla(gla_in, w_a2p, b_gla_a[0], gla_norm_g[0], b_, s_)

    lam_params = jnp.stack([lambda_q1[0], lambda_k1[0], lambda_q2[0], lambda_k2[0]]).astype(F32)
    o_d = _diff_attn(dqk, v_t, lam_params, diff_norm_g[0], b_, s_)

    pad_r = LANES - N_EXPERTS - N_GROUPS
    w_r = jnp.pad(jnp.concatenate([w_router_exp[0], w_router_grp[0]], axis=1).T, ((0, pad_r), (0, 0))).astype(BF16)
    b_r = jnp.pad(jnp.concatenate([b_router_exp[0], b_router_grp[0]]), (0, pad_r)).reshape(LANES, 1)
    x2e = _post_mix(xf, o_g, o_d, w_out[0].astype(BF16), norm_cross_g[0], w_cq[0].astype(BF16), kmem, vmem,
                    w_co[0].astype(BF16), norm_ffn_g[0], w_r, b_r, s_, m_)

    cls = x2e[:, D_MODEL + 2].astype(jnp.int32)
    dest, fill, blk_lo, blk_hi, nvalid, n_rows = _dispatch_plan(cls, t)
    xs = _dispatch(x2e, dest, fill, n_rows)
    ys = _experts(xs, norm_ffn_g[0], blk_lo, blk_hi, nvalid, w_e_gate[0], w_e_up[0], w_e_down[0])

    out = _combine(x2e, ys, dest, norm_final_g)
    return out.reshape(b_, s_, d_)
```
